```python
import jax, jax.numpy as jnp
from jax import lax
import numpy as np

D_MODEL = 1024
BATCH = 8
SEQ = 2048
DEPTH = 2

HEAD_DIM = 64
DIL_PATTERNS = ((128, 1), (512, 4), (2048, 16))
DIL_HEADS = 4
DIL_WIDTH = len(DIL_PATTERNS) * DIL_HEADS * HEAD_DIM
DIL_OUT = DIL_HEADS * HEAD_DIM
SWA_WINDOW = 128
SWA_Q_HEADS = 8
SWA_KV_HEADS = 2
SWA_GROUP = SWA_Q_HEADS // SWA_KV_HEADS
HGRN_HEADS = 4
HGRN_DK = 128
HGRN_DV = 128
HGRN_CHUNK = 32
N_BRANCH = 3
D_FF = 4 * D_MODEL
CONV_WIDTH = 3
BAND_BLOCK = 128
EPS = 1e-6
SPLITS = (DIL_WIDTH, DIL_WIDTH, DIL_WIDTH,
          SWA_Q_HEADS * HEAD_DIM, SWA_KV_HEADS * HEAD_DIM, SWA_KV_HEADS * HEAD_DIM,
          HGRN_HEADS * HGRN_DK, HGRN_HEADS * HGRN_DK, HGRN_HEADS * HGRN_DV, HGRN_HEADS * HGRN_DV,
          N_BRANCH * D_MODEL)
D_IN = sum(SPLITS)

kernel_name = "hybrid_dilated_swa_hgrn2_block"


def rms_norm(x, g):
    xf = x.astype(jnp.float32)
    y = xf * lax.rsqrt(jnp.mean(xf * xf, axis=-1, keepdims=True) + EPS)
    return (y * g.astype(jnp.float32)).astype(x.dtype)


def banded_attention(q, k, v, max_dist, sink=None):
    Bsz, L, Hkv, G, Dh = q.shape
    nb = -(-L // BAND_BLOCK)
    pad = nb * BAND_BLOCK - L
    if pad:
        q = jnp.pad(q, ((0, 0), (0, pad), (0, 0), (0, 0), (0, 0)))
        k = jnp.pad(k, ((0, 0), (0, pad), (0, 0), (0, 0)))
        v = jnp.pad(v, ((0, 0), (0, pad), (0, 0), (0, 0)))
    qb = q.reshape(Bsz, nb, BAND_BLOCK, Hkv, G, Dh)
    kb = k.reshape(Bsz, nb, BAND_BLOCK, Hkv, Dh)
    vb = v.reshape(Bsz, nb, BAND_BLOCK, Hkv, Dh)
    shift = lambda t: jnp.pad(t, ((0, 0), (1, 0), (0, 0), (0, 0), (0, 0)))[:, :-1]
    kcat = jnp.concatenate([shift(kb), kb], axis=2)
    vcat = jnp.concatenate([shift(vb), vb], axis=2)
    s = jnp.einsum('bnqhgd,bnkhd->bnhgqk', qb, kcat).astype(jnp.float32) * (Dh ** -0.5)
    qi = jnp.arange(BAND_BLOCK)[:, None]
    ki = jnp.arange(2 * BAND_BLOCK)[None, :]
    dist = BAND_BLOCK + qi - ki
    band = (dist >= 0) & (dist <= max_dist)
    not_before_start = (jnp.arange(nb)[:, None, None] > 0) | (ki >= BAND_BLOCK)[None]
    valid = band[None] & not_before_start
    s = jnp.where(valid[None, :, None, None], s, -jnp.inf)
    m = jnp.max(s, axis=-1)
    if sink is not None:
        sk = sink.astype(jnp.float32)[None, None, :, :, None]
        m = jnp.maximum(m, sk)
    p = jnp.exp(s - m[..., None])
    den = jnp.sum(p, axis=-1)
    if sink is not None:
        den = den + jnp.exp(sk - m)
    o = jnp.einsum('bnhgqk,bnkhd->bnqhgd', p, vcat.astype(jnp.float32))
    o = o / jnp.moveaxis(den, -1, 2)[..., None]
    lse = jnp.moveaxis(m + jnp.log(den), -1, 2)
    o = o.reshape(Bsz, nb * BAND_BLOCK, Hkv, G, Dh)[:, :L].astype(q.dtype)
    lse = lse.reshape(Bsz, nb * BAND_BLOCK, Hkv, G)[:, :L]
    return o, lse


def dilated_group(q, k, v, window, dilation):
    Bsz, S, H, Dh = q.shape
    L = S // dilation
    fold = lambda t: t.reshape(Bsz, L, dilation * H, Dh)
    o, lse = banded_attention(fold(q)[:, :, :, None], fold(k), fold(v), window // dilation)
    return o.reshape(Bsz, S, H, Dh), lse.reshape(Bsz, S, H)


def hgrn2(q, fz, i, lb):
    Bsz, S, H, Dk = q.shape
    Dv = i.shape[-1]
    C = HGRN_CHUNK
    N = S // C
    q = jax.nn.silu(q.astype(jnp.float32))
    fz = fz.astype(jnp.float32)
    lb = lb.astype(jnp.float32)
    logf = jnp.logaddexp(jnp.log(lb), jnp.log1p(-lb) + jax.nn.log_sigmoid(fz))
    k = (1.0 - lb) * jax.nn.sigmoid(-fz)
    rs = lambda t: t.reshape(Bsz, N, C, H, t.shape[-1])
    q, k, logf, v = rs(q), rs(k), rs(logf), rs(i.astype(jnp.float32))
    b = jnp.cumsum(logf, axis=2)
    b_last = b[:, :, -1]
    qe = q * jnp.exp(b)
    ke = k * jnp.exp(-b)
    kd = k * jnp.exp(b_last[:, :, None] - b)
    causal = jnp.tril(jnp.ones((C, C), dtype=bool))
    att = jnp.einsum('bnthk,bnshk->bnhts', qe, ke)
    att = jnp.where(causal, att, 0.0)
    o_intra = jnp.einsum('bnhts,bnshv->bnthv', att, v)
    U = jnp.einsum('bnshk,bnshv->bnhkv', kd, v)
    decay = jnp.exp(b_last)

    def step(state, inp):
        d, u = inp
        return d[..., None] * state + u, state

    init = jnp.zeros((Bsz, H, Dk, Dv), jnp.float32)
    _, s_prev = lax.scan(step, init, (jnp.moveaxis(decay, 1, 0), jnp.moveaxis(U, 1, 0)))
    s_prev = jnp.moveaxis(s_prev, 0, 1)
    o_inter = jnp.einsum('bnthk,bnhkv->bnthv', qe, s_prev)
    return (o_intra + o_inter).reshape(Bsz, S, H, Dv)


def token_mixer(hn, w_in, sinks, lb, out_gain, w_a, w_b, w_c, w_o):
    Bsz, S, _ = hn.shape
    idx = np.cumsum(np.array(SPLITS))[:-1].tolist()
    proj = hn @ w_in
    aq, ak, av, bq, bk, bv, cq, cf, ci, cg, gates = jnp.split(proj, idx, axis=-1)
    grp = lambda t: t.reshape(Bsz, S, len(DIL_PATTERNS), DIL_HEADS, HEAD_DIM)
    aq, ak, av = grp(aq), grp(ak), grp(av)
    outs, lses = [], []
    for gi, (win, dil) in enumerate(DIL_PATTERNS):
        o, lse = dilated_group(aq[:, :, gi], ak[:, :, gi], av[:, :, gi], win, dil)
        outs.append(o)
        lses.append(lse)
    alpha = jax.nn.softmax(jnp.stack(lses, 0), axis=0)
    oa = jnp.sum(alpha[..., None] * jnp.stack(outs, 0).astype(jnp.float32), axis=0)
    ya = oa.astype(hn.dtype).reshape(Bsz, S, DIL_OUT) @ w_a
    ob, _ = banded_attention(bq.reshape(Bsz, S, SWA_KV_HEADS, SWA_GROUP, HEAD_DIM),
                             bk.reshape(Bsz, S, SWA_KV_HEADS, HEAD_DIM),
                             bv.reshape(Bsz, S, SWA_KV_HEADS, HEAD_DIM),
                             SWA_WINDOW - 1, sink=sinks.reshape(SWA_KV_HEADS, SWA_GROUP))
    yb = ob.reshape(Bsz, S, SWA_Q_HEADS * HEAD_DIM) @ w_b
    hd = lambda t, d: t.reshape(Bsz, S, HGRN_HEADS, d)
    oc = hgrn2(hd(cq, HGRN_DK), hd(cf, HGRN_DK), hd(ci, HGRN_DV), lb.reshape(HGRN_HEADS, HGRN_DK))
    oc = rms_norm(oc.astype(hn.dtype), out_gain) * jax.nn.silu(hd(cg, HGRN_DV))
    yc = oc.reshape(Bsz, S, HGRN_HEADS * HGRN_DV) @ w_c
    g = jax.nn.sigmoid(gates.reshape(Bsz, S, N_BRANCH, D_MODEL))
    mix = g[:, :, 0] * ya + g[:, :, 1] * yb + g[:, :, 2] * yc
    return mix @ w_o


def causal_dwconv(u, w, b):
    K, C = w.shape
    y = lax.conv_general_dilated(u, w.astype(u.dtype)[:, None, :], window_strides=(1,),
                                 padding=[(K - 1, 0)], dimension_numbers=('NWC', 'WIO', 'NWC'),
                                 feature_group_count=C)
    return y + b.astype(u.dtype)


def conv_ffn(hn, w_up, conv_w, conv_b, w_down):
    ab = causal_dwconv(hn @ w_up, conv_w, conv_b)
    a, b = jnp.split(ab, 2, axis=-1)
    return (jax.nn.gelu(a, approximate=True) * b) @ w_down


def setup_inputs(seed: int = 0) -> dict:
    key = jax.random.key(seed)
    ks = jax.random.split(key, 18)
    f32 = jnp.float32
    nrm = lambda k, shape, scale: jax.random.normal(k, shape, f32) * scale
    gain = lambda k, shape: 1.0 + 0.02 * jax.random.normal(k, shape, f32)
    return {
        "x": jax.random.normal(ks[0], (BATCH, SEQ, D_MODEL), f32),
        "norm_pre_mix": gain(ks[1], (DEPTH, D_MODEL)),
        "norm_post_mix": gain(ks[2], (DEPTH, D_MODEL)),
        "norm_pre_ffn": gain(ks[3], (DEPTH, D_MODEL)),
        "norm_post_ffn": gain(ks[4], (DEPTH, D_MODEL)),
        "w_in": nrm(ks[5], (DEPTH, D_MODEL, D_IN), D_MODEL ** -0.5),
        "attn_sinks": nrm(ks[6], (DEPTH, SWA_Q_HEADS), 0.5),
        "hgrn_lb_logits": nrm(ks[7], (DEPTH, HGRN_HEADS * HGRN_DK), 1.0),
        "hgrn_out_norm": gain(ks[8], (DEPTH, HGRN_DV)),
        "w_branch_a": nrm(ks[9], (DEPTH, DIL_OUT, D_MODEL), DIL_OUT ** -0.5),
        "w_branch_b": nrm(ks[10], (DEPTH, SWA_Q_HEADS * HEAD_DIM, D_MODEL), (SWA_Q_HEADS * HEAD_DIM) ** -0.5),
        "w_branch_c": nrm(ks[11], (DEPTH, HGRN_HEADS * HGRN_DV, D_MODEL), (HGRN_HEADS * HGRN_DV) ** -0.5),
        "w_out": nrm(ks[12], (DEPTH, D_MODEL, D_MODEL), D_MODEL ** -0.5),
        "w_ffn_up": nrm(ks[13], (DEPTH, D_MODEL, 2 * D_FF), D_MODEL ** -0.5),
        "ffn_conv_w": nrm(ks[14], (DEPTH, CONV_WIDTH, 2 * D_FF), CONV_WIDTH ** -0.5),
        "ffn_conv_b": nrm(ks[15], (DEPTH, 2 * D_FF), 0.01),
        "w_ffn_down": nrm(ks[16], (DEPTH, D_FF, D_MODEL), D_FF ** -0.5),
    }


def reference(x, norm_pre_mix, norm_post_mix, norm_pre_ffn, norm_post_ffn, w_in, attn_sinks,
              hgrn_lb_logits, hgrn_out_norm, w_branch_a, w_branch_b, w_branch_c, w_out,
              w_ffn_up, ffn_conv_w, ffn_conv_b, w_ffn_down):
    lbs = jnp.cumsum(jax.nn.softmax(hgrn_lb_logits.astype(jnp.float32), axis=0), axis=0)
    lbs = lbs - lbs[0:1]
    h = x
    for l in range(DEPTH):
        mix = token_mixer(rms_norm(h, norm_pre_mix[l]), w_in[l], attn_sinks[l], lbs[l],
                          hgrn_out_norm[l], w_branch_a[l], w_branch_b[l], w_branch_c[l], w_out[l])
        h = h + rms_norm(mix, norm_post_mix[l])
        ff = conv_ffn(rms_norm(h, norm_pre_ffn[l]), w_ffn_up[l], ffn_conv_w[l], ffn_conv_b[l], w_ffn_down[l])
        h = h + rms_norm(ff, norm_post_ffn[l])
    return h
```

```python
import functools

import jax
import jax.numpy as jnp
from jax import lax
from jax.experimental import pallas as pl
from jax.experimental.pallas import tpu as pltpu

D_MODEL = 1024
HEAD_DIM = 64
DIL_PATTERNS = ((128, 1), (512, 4), (2048, 16))
SWA_WINDOW = 128
SWA_KV_HEADS = 2
HGRN_HEADS = 4
HGRN_DK = 128
HGRN_CHUNK = 32
D_FF = 4 * D_MODEL
D_IN = 8192
BAND = 128
GROUP_W = 256
EPS = 1e-6
NEG = -1e30

COL_AQ, COL_AK, COL_AV = 0, 768, 1536
COL_BQ, COL_BK, COL_BV = 2304, 2816, 2944
COL_CQ, COL_CF, COL_CI, COL_CG = 3072, 3584, 4096, 4608
COL_GATES = 5120

VMEM_LIMIT = 56 * 1024 * 1024

_f32 = jnp.float32
_bf16 = jnp.bfloat16


def _dot(a, b):
    return jnp.dot(a, b, preferred_element_type=_f32)


def _dot_nt(a, b):
    return lax.dot_general(a, b, (((1,), (1,)), ((), ())), preferred_element_type=_f32)


def _dot_tn(a, b):
    return lax.dot_general(a, b, (((0,), (0,)), ((), ())), preferred_element_type=_f32)


def _rms(x, g):
    return x * lax.rsqrt(jnp.mean(x * x, axis=-1, keepdims=True) + EPS) * g


def _sigmoid(x):
    return 1.0 / (1.0 + jnp.exp(-x))


def _params(*sem):
    return pltpu.CompilerParams(dimension_semantics=sem, vmem_limit_bytes=VMEM_LIMIT)


def _in_proj_kernel(x_ref, g_ref, w_ref, o_ref, hn_ref):
    @pl.when(pl.program_id(1) == 0)
    def _():
        hn_ref[...] = _rms(x_ref[...], g_ref[...]).astype(_bf16)

    o_ref[...] = _dot(hn_ref[...], w_ref[...]).astype(_bf16)


def _in_proj(x, g, w, *, tm=1024, tn=1024):
    t, d = x.shape
    n = w.shape[1]
    return pl.pallas_call(
        _in_proj_kernel,
        out_shape=jax.ShapeDtypeStruct((t, n), _bf16),
        grid=(t // tm, n // tn),
        in_specs=[
            pl.BlockSpec((tm, d), lambda i, j: (i, 0)),
            pl.BlockSpec((1, d), lambda i, j: (0, 0)),
            pl.BlockSpec((d, tn), lambda i, j: (0, j)),
        ],
        out_specs=pl.BlockSpec((tm, tn), lambda i, j: (i, j)),
        scratch_shapes=[pltpu.VMEM((tm, d), _bf16)],
        compiler_params=_params("arbitrary", "arbitrary"),
        name="in_proj",
    )(x, g, w)


def _band_attention(q_ref, k_ref, v_ref, o_ref, lse_ref, *, nb, max_dist, sink_of_head):
    lane_head = lax.broadcasted_iota(jnp.int32, (BAND, GROUP_W), 1) // HEAD_DIM
    qi = lax.broadcasted_iota(jnp.int32, (BAND, BAND), 0)
    ki = lax.broadcasted_iota(jnp.int32, (BAND, BAND), 1)
    mask_cur = ki <= qi
    mask_prev = (BAND + qi - ki) <= max_dist
    scale = HEAD_DIM ** -0.5

    def block(i, has_prev):
        r0 = pl.multiple_of(i * BAND, BAND) if has_prev else 0
        q = q_ref[pl.ds(r0, BAND), :] * jnp.asarray(scale, _bf16)
        kc = k_ref[pl.ds(r0, BAND), :]
        vc = v_ref[pl.ds(r0, BAND), :]
        if has_prev:
            rp = pl.multiple_of(jnp.maximum(i - 1, 0) * BAND, BAND)
            kp = k_ref[pl.ds(rp, BAND), :]
            vp = v_ref[pl.ds(rp, BAND), :]
            mp = jnp.logical_and(mask_prev, i > 0)
        o_acc = jnp.zeros((BAND, GROUP_W), _f32)
        lse_acc = jnp.zeros((BAND, GROUP_W), _f32)
        for h in range(GROUP_W // HEAD_DIM):
            sel = lane_head == h
            qm = jnp.where(sel, q, jnp.zeros_like(q))
            sc = jnp.where(mask_cur, _dot_nt(qm, kc), NEG)
            m = jnp.max(sc, axis=-1, keepdims=True)
            if has_prev:
                sp = jnp.where(mp, _dot_nt(qm, kp), NEG)
                m = jnp.maximum(m, jnp.max(sp, axis=-1, keepdims=True))
            if sink_of_head is not None:
                sink = sink_of_head(h)
                m = jnp.maximum(m, sink)
            pc = jnp.exp(sc - m)
            den = jnp.sum(pc, axis=-1, keepdims=True)
            pv = _dot(pc.astype(_bf16), vc)
            if has_prev:
                pp = jnp.exp(sp - m)
                den = den + jnp.sum(pp, axis=-1, keepdims=True)
                pv = pv + _dot(pp.astype(_bf16), vp)
            if sink_of_head is not None:
                den = den + jnp.exp(sink - m)
            o_acc = jnp.where(sel, pv / den, o_acc)
            if lse_ref is not None:
                lse_acc = jnp.where(sel, m + jnp.log(den), lse_acc)
        o_ref[pl.ds(r0, BAND), :] = o_acc.astype(o_ref.dtype)
        if lse_ref is not None:
            lse_ref[pl.ds(r0, BAND), :] = lse_acc

    if nb == 1:
        block(0, False)
    else:
        def body(i, c):
            block(i, True)
            return c
        lax.fori_loop(0, nb, body, 0)


def _dil_kernel(q_ref, k_ref, v_ref, o_ref, lse_ref, *, nb, max_dist):
    _band_attention(q_ref, k_ref, v_ref, o_ref, lse_ref, nb=nb, max_dist=max_dist, sink_of_head=None)


def _dilated_group(proj, gi, *, bsz, seq):
    window, d = DIL_PATTERNS[gi]
    ln = seq // d
    nb = ln // BAND
    cpb = D_IN // GROUP_W
    folded = proj.reshape(bsz, ln, d * D_IN)
    qb, kb, vb = (COL_AQ // GROUP_W + gi, COL_AK // GROUP_W + gi, COL_AV // GROUP_W + gi)
    spec = lambda cb: pl.BlockSpec((None, ln, GROUP_W), lambda b, r: (b, 0, r * cpb + cb))
    out_spec = pl.BlockSpec((None, ln, GROUP_W), lambda b, r: (b, 0, r))
    o, lse = pl.pallas_call(
        functools.partial(_dil_kernel, nb=nb, max_dist=window // d),
        out_shape=(jax.ShapeDtypeStruct((bsz, ln, d * GROUP_W), _f32),
                   jax.ShapeDtypeStruct((bsz, ln, d * GROUP_W), _f32)),
        grid=(bsz, d),
        in_specs=[spec(qb), spec(kb), spec(vb)],
        out_specs=(out_spec, out_spec),
        compiler_params=_params("arbitrary", "arbitrary"),
        name=f"dil_attn_{gi}",
    )(folded, folded, folded)
    return o.reshape(bsz * seq, GROUP_W), lse.reshape(bsz * seq, GROUP_W)


def _swa_kernel(sink_ref, q_ref, k_ref, v_ref, o_ref, kx_ref, vx_ref, *, nb):
    kvh = pl.program_id(1)
    lane = lax.broadcasted_iota(jnp.int32, (BAND, 2 * HEAD_DIM), 1)
    keep = (lane < HEAD_DIM) == (kvh == 0)

    def expand(i, c):
        r0 = pl.multiple_of(i * BAND, BAND)
        for src, dst in ((k_ref, kx_ref), (v_ref, vx_ref)):
            x = src[pl.ds(r0, BAND), :].astype(_f32)
            x2 = jnp.where(keep, x, pltpu.roll(x, HEAD_DIM, axis=1)).astype(_bf16)
            dst[pl.ds(r0, BAND), :] = jnp.concatenate([x2, x2], axis=1)
        return c

    lax.fori_loop(0, nb, expand, 0)
    heads_per_kv = GROUP_W // HEAD_DIM
    _band_attention(q_ref, kx_ref, vx_ref, o_ref, None, nb=nb, max_dist=SWA_WINDOW - 1,
                    sink_of_head=lambda h: sink_ref[kvh * heads_per_kv + h])


def _swa(proj, sinks, *, bsz, seq):
    nb = seq // BAND
    p3 = proj.reshape(bsz, seq, D_IN)
    kvw = SWA_KV_HEADS * HEAD_DIM
    return pl.pallas_call(
        functools.partial(_swa_kernel, nb=nb),
        out_shape=jax.ShapeDtypeStruct((bsz, seq, SWA_KV_HEADS * GROUP_W), _bf16),
        grid=(bsz, SWA_KV_HEADS),
        in_specs=[
            pl.BlockSpec(memory_space=pltpu.SMEM),
            pl.BlockSpec((None, seq, GROUP_W), lambda b, g: (b, 0, COL_BQ // GROUP_W + g)),
            pl.BlockSpec((None, seq, kvw), lambda b, g: (b, 0, COL_BK // kvw)),
            pl.BlockSpec((None, seq, kvw), lambda b, g: (b, 0, COL_BV // kvw)),
        ],
        out_specs=pl.BlockSpec((None, seq, GROUP_W), lambda b, g: (b, 0, g)),
        scratch_shapes=[pltpu.VMEM((seq, GROUP_W), _bf16), pltpu.VMEM((seq, GROUP_W), _bf16)],
        compiler_params=_params("arbitrary", "arbitrary"),
        name="swa_attn",
    )(sinks, p3, p3, p3).reshape(bsz * seq, SWA_KV_HEADS * GROUP_W)


def _split3(x):
    hi = x.astype(_bf16)
    r = x - hi.astype(_f32)
    mid = r.astype(_bf16)
    lo = (r - mid.astype(_f32)).astype(_bf16)
    return hi, mid, lo


def _hgrn_kernel(lbl_ref, gain_ref, q_ref, f_ref, i_ref, g_ref, o_ref, st_ref, *, layer, tc):
    @pl.when(pl.program_id(1) == 0)
    def _():
        st_ref[...] = jnp.zeros_like(st_ref)

    logits = lbl_ref[...]
    e = jnp.exp(logits - jnp.max(logits, axis=0, keepdims=True))
    sm = e / jnp.sum(e, axis=0, keepdims=True)
    lb = jnp.zeros((1, sm.shape[1]), _f32)
    for j in range(1, layer + 1):
        lb = lb + sm[j:j + 1, :]
    log_lb = jnp.log(lb)
    log_1mlb = jnp.log1p(-lb)

    ti = lax.broadcasted_iota(jnp.int32, (tc, tc), 0)
    si = lax.broadcasted_iota(jnp.int32, (tc, tc), 1)
    same = (ti // HGRN_CHUNK) == (si // HGRN_CHUNK)
    causal = jnp.logical_and(same, si <= ti)
    sum_mat = jnp.concatenate([causal, same], axis=0).astype(_f32).astype(_bf16)
    nchunk = tc // HGRN_CHUNK

    for h in range(HGRN_HEADS):
        sl = slice(h * HGRN_DK, (h + 1) * HGRN_DK)
        qraw = q_ref[:, sl].astype(_f32)
        q = qraw * _sigmoid(qraw)
        fz = f_ref[:, sl].astype(_f32)
        v = i_ref[:, sl]
        log_sig = jnp.minimum(fz, 0.0) - jnp.log1p(jnp.exp(-jnp.abs(fz)))
        a = log_lb[:, sl]
        bterm = log_1mlb[:, sl] + log_sig
        logf = jnp.maximum(a, bterm) + jnp.log1p(jnp.exp(-jnp.abs(a - bterm)))
        k = (1.0 - lb[:, sl]) * _sigmoid(-fz)

        sums = _dot(sum_mat, jnp.concatenate(_split3(logf), axis=1))
        sums = sums[:, 0:HGRN_DK] + sums[:, HGRN_DK:2 * HGRN_DK] + sums[:, 2 * HGRN_DK:]
        b = sums[0:tc]
        tot = sums[tc:]
        qe = (q * jnp.exp(b)).astype(_bf16)
        ke = (k * jnp.exp(-b)).astype(_bf16)
        kd = (k * jnp.exp(tot - b)).astype(_bf16)

        att = jnp.where(causal, _dot_nt(qe, ke), 0.0)
        o = _dot(att.astype(_bf16), v)

        st = st_ref[h]
        inter = []
        for n in range(nchunk):
            rs = slice(n * HGRN_CHUNK, (n + 1) * HGRN_CHUNK)
            inter.append(_dot_nt(qe[rs], st.astype(_bf16)))
            decay = jnp.exp(tot[n * HGRN_CHUNK:n * HGRN_CHUNK + 1, :])
            st = st * decay + _dot_tn(v[rs], kd[rs])
        st_ref[h] = st
        o = o + jnp.concatenate(inter, axis=0)

        y = _rms(o, gain_ref[...])
        graw = g_ref[:, sl].astype(_f32)
        o_ref[:, sl] = (y * (graw * _sigmoid(graw))).astype(o_ref.dtype)


def _hgrn(proj, lb_logits, gain, *, layer, bsz, seq, tc=256):
    p3 = proj.reshape(bsz, seq, D_IN)
    w = HGRN_HEADS * HGRN_DK
    spec = lambda col: pl.BlockSpec((None, tc, w), lambda b, t: (b, t, col // w))
    return pl.pallas_call(
        functools.partial(_hgrn_kernel, layer=layer, tc=tc),
        out_shape=jax.ShapeDtypeStruct((bsz, seq, w), _bf16),
        grid=(bsz, seq // tc),
        in_specs=[
            pl.BlockSpec(lb_logits.shape, lambda b, t: (0, 0)),
            pl.BlockSpec((1, HGRN_DK), lambda b, t: (0, 0)),
            spec(COL_CQ), spec(COL_CF), spec(COL_CI), spec(COL_CG),
        ],
        out_specs=pl.BlockSpec((None, tc, w), lambda b, t: (b, t, 0)),
        scratch_shapes=[pltpu.VMEM((HGRN_HEADS, HGRN_DK, HGRN_DK), _f32)],
        compiler_params=_params("arbitrary", "arbitrary"),
        name="hgrn2",
    )(lb_logits, gain, p3, p3, p3, p3).reshape(bsz * seq, w)


def _merge_kernel(o0_ref, l0_ref, o1_ref, l1_ref, o2_ref, l2_ref, ob_ref, oc_ref,
                  g0_ref, g1_ref, g2_ref, h_ref, wa_ref, wb_ref, wc_ref, wo_ref, gp_ref, out_ref):
    l0, l1, l2 = l0_ref[...], l1_ref[...], l2_ref[...]
    m = jnp.maximum(jnp.maximum(l0, l1), l2)
    e0, e1, e2 = jnp.exp(l0 - m), jnp.exp(l1 - m), jnp.exp(l2 - m)
    oa = (e0 * o0_ref[...] + e1 * o1_ref[...] + e2 * o2_ref[...]) / (e0 + e1 + e2)
    ya = _dot(oa.astype(_bf16), wa_ref[...])
    yb = _dot(ob_ref[...], wb_ref[...])
    yc = _dot(oc_ref[...], wc_ref[...])
    mix = (_sigmoid(g0_ref[...].astype(_f32)) * ya + _sigmoid(g1_ref[...].astype(_f32)) * yb
           + _sigmoid(g2_ref[...].astype(_f32)) * yc)
    out = _dot(mix.astype(_bf16), wo_ref[...])
    out_ref[...] = h_ref[...] + _rms(out, gp_ref[...])


def _merge(oas, ob, oc, proj, h, wa, wb, wc, wo, gp, *, tm=512):
    t, d = h.shape
    row = lambda w: pl.BlockSpec((tm, w), lambda i: (i, 0))
    full = lambda a: pl.BlockSpec(a.shape, lambda i: (0, 0))
    gate = lambda j: pl.BlockSpec((tm, d), lambda i: (i, COL_GATES // d + j))
    flat = [a for pair in oas for a in pair]
    return pl.pallas_call(
        _merge_kernel,
        out_shape=jax.ShapeDtypeStruct((t, d), _f32),
        grid=(t // tm,),
        in_specs=[row(GROUP_W)] * 6 + [row(ob.shape[1]), row(oc.shape[1]), gate(0), gate(1), gate(2), row(d),
                                       full(wa), full(wb), full(wc), full(wo), full(gp)],
        out_specs=row(d),
        compiler_params=_params("arbitrary"),
        name="merge_out",
    )(*flat, ob, oc, proj, proj, proj, h, wa, wb, wc, wo, gp)


def _shift_rows(u, carry, s):
    r = pltpu.roll(u, s, axis=0)
    c = pltpu.roll(carry, s, axis=0)
    row = lax.broadcasted_iota(jnp.int32, carry.shape, 0)
    head = jnp.where(row < s, c, r[0:8])
    return jnp.concatenate([head, r[8:]], axis=0)


def _ffn_kernel(x_ref, gpre_ref, wa_ref, wb_ref, cwa_ref, cwb_ref, cba_ref, cbb_ref, wd_ref, gpost_ref,
                out_ref, hn_ref, acc_ref, ca_ref, cb_ref, *, tiles_per_seq):
    i, j = pl.program_id(0), pl.program_id(1)
    tm = x_ref.shape[0]

    @pl.when(j == 0)
    def _():
        hn_ref[...] = _rms(x_ref[...], gpre_ref[...]).astype(_bf16)
        acc_ref[...] = jnp.zeros_like(acc_ref)

    seq_start = (i % tiles_per_seq) == 0

    @pl.when(i == 0)
    def _():
        ca_ref[j] = jnp.zeros(ca_ref.shape[1:], _f32)
        cb_ref[j] = jnp.zeros(cb_ref.shape[1:], _f32)

    def conv(w_ref, cw_ref, cbias_ref, carry_ref):
        u = _dot(hn_ref[...], w_ref[...])
        carry = jnp.where(seq_start, 0.0, carry_ref[j])
        carry_ref[j] = u[tm - 8:tm]
        cw = cw_ref[...]
        return (cw[2:3] * u + cw[1:2] * _shift_rows(u, carry, 1) + cw[0:1] * _shift_rows(u, carry, 2)
                + cbias_ref[...])

    a = conv(wa_ref, cwa_ref, cba_ref, ca_ref)
    b = conv(wb_ref, cwb_ref, cbb_ref, cb_ref)
    gelu = 0.5 * a * (1.0 + jnp.tanh(0.7978845608028654 * (a + 0.044715 * (a * a * a))))
    acc_ref[...] += _dot((gelu * b).astype(_bf16), wd_ref[...])

    @pl.when(j == pl.num_programs(1) - 1)
    def _():
        out_ref[...] = x_ref[...] + _rms(acc_ref[...], gpost_ref[...])


def _ffn(h, gpre, w_up, conv_w, conv_b, w_down, gpost, *, seq, tm=512, tf=512):
    t, d = h.shape
    nf = D_FF // tf
    cb2 = conv_b.reshape(1, 2 * D_FF)
    return pl.pallas_call(
        functools.partial(_ffn_kernel, tiles_per_seq=seq // tm),
        out_shape=jax.ShapeDtypeStruct((t, d), _f32),
        grid=(t // tm, nf),
        in_specs=[
            pl.BlockSpec((tm, d), lambda i, j: (i, 0)),
            pl.BlockSpec((1, d), lambda i, j: (0, 0)),
            pl.BlockSpec((d, tf), lambda i, j: (0, j)),
            pl.BlockSpec((d, tf), lambda i, j: (0, nf + j)),
            pl.BlockSpec((3, tf), lambda i, j: (0, j)),
            pl.BlockSpec((3, tf), lambda i, j: (0, nf + j)),
            pl.BlockSpec((1, tf), lambda i, j: (0, j)),
            pl.BlockSpec((1, tf), lambda i, j: (0, nf + j)),
            pl.BlockSpec((tf, d), lambda i, j: (j, 0)),
            pl.BlockSpec((1, d), lambda i, j: (0, 0)),
        ],
        out_specs=pl.BlockSpec((tm, d), lambda i, j: (i, 0)),
        scratch_shapes=[
            pltpu.VMEM((tm, d), _bf16),
            pltpu.VMEM((tm, d), _f32),
            pltpu.VMEM((nf, 8, tf), _f32),
            pltpu.VMEM((nf, 8, tf), _f32),
        ],
        compiler_params=_params("arbitrary", "arbitrary"),
        name="conv_ffn",
    )(h, gpre, w_up, w_up, conv_w, conv_w, cb2, cb2, w_down, gpost)


def kernel(x, norm_pre_mix, norm_post_mix, norm_pre_ffn, norm_post_ffn, w_in, attn_sinks, hgrn_lb_logits,
           hgrn_out_norm, w_branch_a, w_branch_b, w_branch_c, w_out, w_ffn_up, ffn_conv_w, ffn_conv_b,
           w_ffn_down):
    bsz, seq, d = x.shape
    depth = w_in.shape[0]
    h = x.reshape(bsz * seq, d)
    row = lambda a: a.reshape(1, -1)
    for l in range(depth):
        proj = _in_proj(h, row(norm_pre_mix[l]), w_in[l].astype(_bf16))
        oas = [_dilated_group(proj, gi, bsz=bsz, seq=seq) for gi in range(len(DIL_PATTERNS))]
        ob = _swa(proj, attn_sinks[l], bsz=bsz, seq=seq)
        oc = _hgrn(proj, hgrn_lb_logits, row(hgrn_out_norm[l]), layer=l, bsz=bsz, seq=seq)
        h = _merge(oas, ob, oc, proj, h, w_branch_a[l].astype(_bf16), w_branch_b[l].astype(_bf16),
                   w_branch_c[l].astype(_bf16), w_out[l].astype(_bf16), row(norm_post_mix[l]))
        h = _ffn(h, row(norm_pre_ffn[l]), w_ffn_up[l].astype(_bf16), ffn_conv_w[l], ffn_conv_b[l],
                 w_ffn_down[l].astype(_bf16), row(norm_post_ffn[l]), seq=seq)
    return h.reshape(bsz, seq, d)
```

```python
import functools

import jax
import jax.numpy as jnp
from jax import lax
from jax.experimental import pallas as pl
from jax.experimental.pallas import tpu as pltpu

D_MODEL = 1024
HEAD_DIM = 64
DIL_PATTERNS = ((128, 1), (512, 4), (2048, 16))
SWA_WINDOW = 128
SWA_KV_HEADS = 2
HGRN_HEADS = 4
HGRN_DK = 128
HGRN_CHUNK = 32
D_FF = 4 * D_MODEL
BAND = 128
GROUP_W = 256
QKV_W = 3 * GROUP_W
EPS = 1e-6
LANES = 128
NEG = -1e30

SRC_AQ, SRC_AK, SRC_AV = 0, 768, 1536
SRC_B, SRC_C, SRC_GATES = 2304, 3072, 5120
NAT_GATES = 0
NAT_A0 = 3072
NAT_BQ, NAT_BK, NAT_BV = 3840, 4352, 4480
NAT_CQ, NAT_CF, NAT_CI, NAT_CG = 4608, 5120, 5632, 6144
NAT_W = 6656
D_IN = NAT_W + 2 * QKV_W

VMEM_LIMIT = 56 * 1024 * 1024

_f32 = jnp.float32
_bf16 = jnp.bfloat16


def _dot(a, b):
    return jnp.dot(a, b, preferred_element_type=_f32)


def _dot_nt(a, b):
    return lax.dot_general(a, b, (((1,), (1,)), ((), ())), preferred_element_type=_f32)


def _dot_tn(a, b):
    return lax.dot_general(a, b, (((0,), (0,)), ((), ())), preferred_element_type=_f32)


def _rms(x, g):
    return x * lax.rsqrt(jnp.mean(x * x, axis=-1, keepdims=True) + EPS) * g


def _sigmoid(x):
    return 1.0 / (1.0 + jnp.exp(-x))


def _params(*sem):
    return pltpu.CompilerParams(dimension_semantics=sem, vmem_limit_bytes=VMEM_LIMIT)


def _permute_w_in(w):
    cols = lambda start, width: w[:, start:start + width]
    grp = lambda gi: [cols(SRC_AQ + gi * GROUP_W, GROUP_W), cols(SRC_AK + gi * GROUP_W, GROUP_W),
                      cols(SRC_AV + gi * GROUP_W, GROUP_W)]
    parts = [cols(SRC_GATES, 3 * D_MODEL)] + grp(0) + [cols(SRC_B, SRC_C - SRC_B), cols(SRC_C, SRC_GATES - SRC_C)]
    parts += grp(1) + grp(2)
    return jnp.concatenate(parts, axis=1).astype(_bf16)


def _in_proj_kernel(x_ref, g_ref, w_ref, nat_ref, g1_ref, g2_ref, hn_ref, *, tn):
    tm = x_ref.shape[0]
    hn = _rms(x_ref[...], g_ref[...])
    nlc = hn_ref.shape[0]
    for c in range(nlc):
        hn_ref[c] = hn[:, c * LANES:(c + 1) * LANES]
    hb = hn.astype(_bf16)
    for c in range(NAT_W // tn):
        nat_ref[:, c * tn:(c + 1) * tn] = _dot(hb, w_ref[:, c * tn:(c + 1) * tn]).astype(_bf16)
    for out_ref, col0 in ((g1_ref, NAT_W), (g2_ref, NAT_W + QKV_W)):
        d = out_ref.shape[0]
        rows = tm // d
        lhs = jnp.concatenate(
            [jnp.concatenate([hn_ref[c, pl.ds(r, rows, stride=d), :] for c in range(nlc)], axis=1)
             for r in range(d)], axis=0).astype(_bf16)
        y = _dot(lhs, w_ref[:, col0:col0 + QKV_W]).astype(_bf16)
        for r in range(d):
            out_ref[r] = y[r * rows:(r + 1) * rows]


def _in_proj(x, g, w, *, bsz, seq, tm=512, tn=512):
    t, d = x.shape
    tps = seq // tm
    d1, d2 = DIL_PATTERNS[1][1], DIL_PATTERNS[2][1]
    return pl.pallas_call(
        functools.partial(_in_proj_kernel, tn=tn),
        out_shape=(jax.ShapeDtypeStruct((t, NAT_W), _bf16),
                   jax.ShapeDtypeStruct((bsz, d1, seq // d1, QKV_W), _bf16),
                   jax.ShapeDtypeStruct((bsz, d2, seq // d2, QKV_W), _bf16)),
        grid=(t // tm,),
        in_specs=[
            pl.BlockSpec((tm, d), lambda i: (i, 0)),
            pl.BlockSpec((1, d), lambda i: (0, 0)),
            pl.BlockSpec((d, D_IN), lambda i: (0, 0), pipeline_mode=pl.Buffered(1)),
        ],
        out_specs=(
            pl.BlockSpec((tm, NAT_W), lambda i: (i, 0)),
            pl.BlockSpec((None, d1, tm // d1, QKV_W), lambda i: (i // tps, 0, i % tps, 0)),
            pl.BlockSpec((None, d2, tm // d2, QKV_W), lambda i: (i // tps, 0, i % tps, 0)),
        ),
        scratch_shapes=[pltpu.VMEM((d // LANES, tm, LANES), _f32)],
        compiler_params=_params("arbitrary"),
        name="in_proj",
    )(x, g, w)


def _band_block(q, kc, vc, kp, vp, prev_valid, *, max_dist, sink_of_head, want_lse):
    lane_head = lax.broadcasted_iota(jnp.int32, (BAND, GROUP_W), 1) // HEAD_DIM
    qi = lax.broadcasted_iota(jnp.int32, (BAND, BAND), 0)
    ki = lax.broadcasted_iota(jnp.int32, (BAND, BAND), 1)
    mask_cur = ki <= qi
    has_prev = kp is not None
    if has_prev:
        mp = jnp.logical_and((BAND + qi - ki) <= max_dist, prev_valid)
    q = q * jnp.asarray(HEAD_DIM ** -0.5, _bf16)
    o_acc = jnp.zeros((BAND, GROUP_W), _f32)
    lse_acc = jnp.zeros((BAND, GROUP_W), _f32)
    for h in range(GROUP_W // HEAD_DIM):
        sel = lane_head == h
        qm = jnp.where(sel, q, jnp.zeros_like(q))
        sc = jnp.where(mask_cur, _dot_nt(qm, kc), NEG)
        m = jnp.max(sc, axis=-1, keepdims=True)
        if has_prev:
            sp = jnp.where(mp, _dot_nt(qm, kp), NEG)
            m = jnp.maximum(m, jnp.max(sp, axis=-1, keepdims=True))
        if sink_of_head is not None:
            sink = sink_of_head(h)
            m = jnp.maximum(m, sink)
        pc = jnp.exp(sc - m)
        den = jnp.sum(pc, axis=-1, keepdims=True)
        pv = _dot(pc.astype(_bf16), vc)
        if has_prev:
            pp = jnp.exp(sp - m)
            den = den + jnp.sum(pp, axis=-1, keepdims=True)
            pv = pv + _dot(pp.astype(_bf16), vp)
        if sink_of_head is not None:
            den = den + jnp.exp(sink - m)
        o_acc = jnp.where(sel, pv / den, o_acc)
        if want_lse:
            lse_acc = jnp.where(sel, m + jnp.log(den), lse_acc)
    return o_acc, (lse_acc if want_lse else None)


def _qkv_cols(ref, rows, lead=()):
    idx = lambda c: lead + (rows, slice(c * GROUP_W, (c + 1) * GROUP_W))
    return ref[idx(0)], ref[idx(1)], ref[idx(2)]


def _dil_kernel(a0_ref, g1_ref, g2_ref, oa_ref, o_s, lse_s):
    seq = a0_ref.shape[0]
    blk = functools.partial(_band_block, sink_of_head=None, want_lse=True)
    halves = [slice(hf * LANES, (hf + 1) * LANES) for hf in range(GROUP_W // LANES)]

    def put(g, rows, o, lse):
        for hf, ls in enumerate(halves):
            o_s[g, hf, rows, :] = o[:, ls]
            lse_s[g, hf, rows, :] = lse[:, ls]

    def g0_body(i, c):
        r0 = pl.multiple_of(i * BAND, BAND)
        rp = pl.multiple_of(jnp.maximum(i - 1, 0) * BAND, BAND)
        q, kc, vc = _qkv_cols(a0_ref, pl.ds(r0, BAND))
        _, kp, vp = _qkv_cols(a0_ref, pl.ds(rp, BAND))
        o, lse = blk(q, kc, vc, kp, vp, i > 0, max_dist=DIL_PATTERNS[0][0])
        put(0, pl.ds(r0, BAND), o, lse)
        return c

    lax.fori_loop(0, seq // BAND, g0_body, 0)

    win1, d1 = DIL_PATTERNS[1]
    nb1 = seq // d1 // BAND

    def g1_body(n, c):
        r, i = n // nb1, n % nb1
        r0 = pl.multiple_of(i * BAND, BAND)
        rp = pl.multiple_of(jnp.maximum(i - 1, 0) * BAND, BAND)
        q, kc, vc = _qkv_cols(g1_ref, pl.ds(r0, BAND), (r,))
        _, kp, vp = _qkv_cols(g1_ref, pl.ds(rp, BAND), (r,))
        o, lse = blk(q, kc, vc, kp, vp, i > 0, max_dist=win1 // d1)
        put(1, pl.ds(i * (BAND * d1) + r, BAND, stride=d1), o, lse)
        return c

    lax.fori_loop(0, d1 * nb1, g1_body, 0)

    win2, d2 = DIL_PATTERNS[2]
    assert seq // d2 == BAND and win2 // d2 >= BAND - 1

    def g2_body(r, c):
        q, kc, vc = _qkv_cols(g2_ref, slice(None), (r,))
        o, lse = blk(q, kc, vc, None, None, None, max_dist=win2 // d2)
        put(2, pl.ds(r, BAND, stride=d2), o, lse)
        return c

    lax.fori_loop(0, d2, g2_body, 0)

    def mix_body(i, c):
        rows = pl.ds(pl.multiple_of(i * BAND, BAND), BAND)
        for hf, ls in enumerate(halves):
            l0, l1, l2 = lse_s[0, hf, rows, :], lse_s[1, hf, rows, :], lse_s[2, hf, rows, :]
            m = jnp.maximum(jnp.maximum(l0, l1), l2)
            e0, e1, e2 = jnp.exp(l0 - m), jnp.exp(l1 - m), jnp.exp(l2 - m)
            oa = (e0 * o_s[0, hf, rows, :] + e1 * o_s[1, hf, rows, :] + e2 * o_s[2, hf, rows, :]) / (e0 + e1 + e2)
            oa_ref[rows, ls] = oa.astype(oa_ref.dtype)
        return c

    lax.fori_loop(0, seq // BAND, mix_body, 0)


def _dilated(nat3, g1, g2):
    bsz, seq, _ = nat3.shape
    n_grp = len(DIL_PATTERNS)
    return pl.pallas_call(
        _dil_kernel,
        out_shape=jax.ShapeDtypeStruct((bsz, seq, GROUP_W), _bf16),
        grid=(bsz,),
        in_specs=[
            pl.BlockSpec((None, seq, QKV_W), lambda b: (b, 0, NAT_A0 // QKV_W)),
            pl.BlockSpec((None,) + g1.shape[1:], lambda b: (b, 0, 0, 0)),
            pl.BlockSpec((None,) + g2.shape[1:], lambda b: (b, 0, 0, 0)),
        ],
        out_specs=pl.BlockSpec((None, seq, GROUP_W), lambda b: (b, 0, 0)),
        scratch_shapes=[pltpu.VMEM((n_grp, GROUP_W // LANES, seq, LANES), _f32)] * 2,
        compiler_params=_params("arbitrary"),
        name="dil_attn",
    )(nat3, g1, g2).reshape(bsz * seq, GROUP_W)


def _swa_kernel(sink_ref, q_ref, k_ref, v_ref, o_ref, kx_ref, vx_ref):
    seq = q_ref.shape[0]
    nb = seq // BAND
    kvh = pl.program_id(1)
    lane = lax.broadcasted_iota(jnp.int32, (BAND, 2 * HEAD_DIM), 1)
    keep = (lane < HEAD_DIM) == (kvh == 0)

    def expand(i, c):
        r0 = pl.multiple_of(i * BAND, BAND)
        for src, dst in ((k_ref, kx_ref), (v_ref, vx_ref)):
            x = src[pl.ds(r0, BAND), :].astype(_f32)
            x2 = jnp.where(keep, x, pltpu.roll(x, HEAD_DIM, axis=1)).astype(_bf16)
            dst[pl.ds(r0, BAND), :] = jnp.concatenate([x2, x2], axis=1)
        return c

    lax.fori_loop(0, nb, expand, 0)
    heads_per_kv = GROUP_W // HEAD_DIM

    def body(i, c):
        r0 = pl.multiple_of(i * BAND, BAND)
        rp = pl.multiple_of(jnp.maximum(i - 1, 0) * BAND, BAND)
        o, _ = _band_block(q_ref[pl.ds(r0, BAND), :], kx_ref[pl.ds(r0, BAND), :], vx_ref[pl.ds(r0, BAND), :],
                           kx_ref[pl.ds(rp, BAND), :], vx_ref[pl.ds(rp, BAND), :], i > 0,
                           max_dist=SWA_WINDOW - 1, want_lse=False,
                           sink_of_head=lambda h: sink_ref[kvh * heads_per_kv + h])
        o_ref[pl.ds(r0, BAND), :] = o.astype(o_ref.dtype)
        return c

    lax.fori_loop(0, nb, body, 0)


def _swa(nat3, sinks):
    bsz, seq, _ = nat3.shape
    kvw = SWA_KV_HEADS * HEAD_DIM
    return pl.pallas_call(
        _swa_kernel,
        out_shape=jax.ShapeDtypeStruct((bsz, seq, SWA_KV_HEADS * GROUP_W), _bf16),
        grid=(bsz, SWA_KV_HEADS),
        in_specs=[
            pl.BlockSpec(memory_space=pltpu.SMEM),
            pl.BlockSpec((None, seq, GROUP_W), lambda b, g: (b, 0, NAT_BQ // GROUP_W + g)),
            pl.BlockSpec((None, seq, kvw), lambda b, g: (b, 0, NAT_BK // kvw)),
            pl.BlockSpec((None, seq, kvw), lambda b, g: (b, 0, NAT_BV // kvw)),
        ],
        out_specs=pl.BlockSpec((None, seq, GROUP_W), lambda b, g: (b, 0, g)),
        scratch_shapes=[pltpu.VMEM((seq, GROUP_W), _bf16), pltpu.VMEM((seq, GROUP_W), _bf16)],
        compiler_params=_params("arbitrary", "arbitrary"),
        name="swa_attn",
    )(sinks, nat3, nat3, nat3).reshape(bsz * seq, SWA_KV_HEADS * GROUP_W)


def _split3(x):
    hi = x.astype(_bf16)
    r = x - hi.astype(_f32)
    mid = r.astype(_bf16)
    lo = (r - mid.astype(_f32)).astype(_bf16)
    return hi, mid, lo


def _hgrn_kernel(lbl_ref, gain_ref, q_ref, f_ref, i_ref, g_ref, o_ref, st_ref, *, layer, tc):
    @pl.when(pl.program_id(1) == 0)
    def _():
        st_ref[...] = jnp.zeros_like(st_ref)

    logits = lbl_ref[...]
    e = jnp.exp(logits - jnp.max(logits, axis=0, keepdims=True))
    sm = e / jnp.sum(e, axis=0, keepdims=True)
    lb = jnp.zeros((1, sm.shape[1]), _f32)
    for j in range(1, layer + 1):
        lb = lb + sm[j:j + 1, :]
    log_lb = jnp.log(lb)
    log_1mlb = jnp.log1p(-lb)

    ti = lax.broadcasted_iota(jnp.int32, (tc, tc), 0)
    si = lax.broadcasted_iota(jnp.int32, (tc, tc), 1)
    same = (ti // HGRN_CHUNK) == (si // HGRN_CHUNK)
    causal = jnp.logical_and(same, si <= ti)
    sum_mat = jnp.concatenate([causal, same], axis=0).astype(_f32).astype(_bf16)
    nchunk = tc // HGRN_CHUNK

    for h in range(HGRN_HEADS):
        sl = slice(h * HGRN_DK, (h + 1) * HGRN_DK)
        qraw = q_ref[:, sl].astype(_f32)
        q = qraw * _sigmoid(qraw)
        fz = f_ref[:, sl].astype(_f32)
        v = i_ref[:, sl]
        log_sig = jnp.minimum(fz, 0.0) - jnp.log1p(jnp.exp(-jnp.abs(fz)))
        a = log_lb[:, sl]
        bterm = log_1mlb[:, sl] + log_sig
        logf = jnp.maximum(a, bterm) + jnp.log1p(jnp.exp(-jnp.abs(a - bterm)))
        k = (1.0 - lb[:, sl]) * _sigmoid(-fz)

        sums = _dot(sum_mat, jnp.concatenate(_split3(logf), axis=1))
        sums = sums[:, 0:HGRN_DK] + sums[:, HGRN_DK:2 * HGRN_DK] + sums[:, 2 * HGRN_DK:]
        b = sums[0:tc]
        tot = sums[tc:]
        qe = (q * jnp.exp(b)).astype(_bf16)
        ke = (k * jnp.exp(-b)).astype(_bf16)
        kd = (k * jnp.exp(tot - b)).astype(_bf16)

        att = jnp.where(causal, _dot_nt(qe, ke), 0.0)
        o = _dot(att.astype(_bf16), v)

        st = st_ref[h]
        inter = []
        for n in range(nchunk):
            rs = slice(n * HGRN_CHUNK, (n + 1) * HGRN_CHUNK)
            inter.append(_dot_nt(qe[rs], st.astype(_bf16)))
            decay = jnp.exp(tot[n * HGRN_CHUNK:n * HGRN_CHUNK + 1, :])
            st = st * decay + _dot_tn(v[rs], kd[rs])
        st_ref[h] = st
        o = o + jnp.concatenate(inter, axis=0)

        y = _rms(o, gain_ref[...])
        graw = g_ref[:, sl].astype(_f32)
        o_ref[:, sl] = (y * (graw * _sigmoid(graw))).astype(o_ref.dtype)


def _hgrn(nat3, lb_logits, gain, *, layer, tc=256):
    bsz, seq, _ = nat3.shape
    w = HGRN_HEADS * HGRN_DK
    spec = lambda col: pl.BlockSpec((None, tc, w), lambda b, t: (b, t, col // w))
    return pl.pallas_call(
        functools.partial(_hgrn_kernel, layer=layer, tc=tc),
        out_shape=jax.ShapeDtypeStruct((bsz, seq, w), _bf16),
        grid=(bsz, seq // tc),
        in_specs=[
            pl.BlockSpec(lb_logits.shape, lambda b, t: (0, 0)),
            pl.BlockSpec((1, HGRN_DK), lambda b, t: (0, 0)),
            spec(NAT_CQ), spec(NAT_CF), spec(NAT_CI), spec(NAT_CG),
        ],
        out_specs=pl.BlockSpec((None, tc, w), lambda b, t: (b, t, 0)),
        scratch_shapes=[pltpu.VMEM((HGRN_HEADS, HGRN_DK, HGRN_DK), _f32)],
        compiler_params=_params("arbitrary", "arbitrary"),
        name="hgrn2",
    )(lb_logits, gain, nat3, nat3, nat3, nat3).reshape(bsz * seq, w)


def _merge_kernel(oa_ref, ob_ref, oc_ref, g0_ref, g1_ref, g2_ref, h_ref, wa_ref, wb_ref, wc_ref, wo_ref, gp_ref,
                  out_ref):
    ya = _dot(oa_ref[...], wa_ref[...])
    yb = _dot(ob_ref[...], wb_ref[...])
    yc = _dot(oc_ref[...], wc_ref[...])
    mix = (_sigmoid(g0_ref[...].astype(_f32)) * ya + _sigmoid(g1_ref[...].astype(_f32)) * yb
           + _sigmoid(g2_ref[...].astype(_f32)) * yc)
    out = _dot(mix.astype(_bf16), wo_ref[...])
    out_ref[...] = h_ref[...] + _rms(out, gp_ref[...])


def _merge(oa, ob, oc, nat, h, wa, wb, wc, wo, gp, *, tm=512):
    t, d = h.shape
    row = lambda a: pl.BlockSpec((tm, a.shape[1]), lambda i: (i, 0))
    full = lambda a: pl.BlockSpec(a.shape, lambda i: (0, 0))
    gate = lambda j: pl.BlockSpec((tm, d), lambda i: (i, NAT_GATES // d + j))
    return pl.pallas_call(
        _merge_kernel,
        out_shape=jax.ShapeDtypeStruct((t, d), _f32),
        grid=(t // tm,),
        in_specs=[row(oa), row(ob), row(oc), gate(0), gate(1), gate(2), row(h),
                  full(wa), full(wb), full(wc), full(wo), full(gp)],
        out_specs=row(h),
        compiler_params=_params("arbitrary"),
        name="merge_out",
    )(oa, ob, oc, nat, nat, nat, h, wa, wb, wc, wo, gp)


def _shift_rows(u, carry, s):
    r = pltpu.roll(u, s, axis=0)
    c = pltpu.roll(carry, s, axis=0)
    row = lax.broadcasted_iota(jnp.int32, carry.shape, 0)
    head = jnp.where(row < s, c, r[0:8])
    return jnp.concatenate([head, r[8:]], axis=0)


def _ffn_kernel(x_ref, gpre_ref, wa_ref, wb_ref, cwa_ref, cwb_ref, cba_ref, cbb_ref, wd_ref, gpost_ref,
                out_ref, hn_ref, acc_ref, ca_ref, cb_ref, *, tiles_per_seq):
    i, j = pl.program_id(0), pl.program_id(1)
    tm = x_ref.shape[0]

    @pl.when(j == 0)
    def _():
        hn_ref[...] = _rms(x_ref[...], gpre_ref[...]).astype(_bf16)
        acc_ref[...] = jnp.zeros_like(acc_ref)

    seq_start = (i % tiles_per_seq) == 0

    @pl.when(i == 0)
    def _():
        ca_ref[j] = jnp.zeros(ca_ref.shape[1:], _f32)
        cb_ref[j] = jnp.zeros(cb_ref.shape[1:], _f32)

    def conv(w_ref, cw_ref, cbias_ref, carry_ref):
        u = _dot(hn_ref[...], w_ref[...])
        carry = jnp.where(seq_start, 0.0, carry_ref[j])
        carry_ref[j] = u[tm - 8:tm]
        cw = cw_ref[...]
        return (cw[2:3] * u + cw[1:2] * _shift_rows(u, carry, 1) + cw[0:1] * _shift_rows(u, carry, 2)
                + cbias_ref[...])

    a = conv(wa_ref, cwa_ref, cba_ref, ca_ref)
    b = conv(wb_ref, cwb_ref, cbb_ref, cb_ref)
    gelu = 0.5 * a * (1.0 + jnp.tanh(0.7978845608028654 * (a + 0.044715 * (a * a * a))))
    acc_ref[...] += _dot((gelu * b).astype(_bf16), wd_ref[...])

    @pl.when(j == pl.num_programs(1) - 1)
    def _():
        out_ref[...] = x_ref[...] + _rms(acc_ref[...], gpost_ref[...])


def _ffn(h, gpre, w_up, conv_w, conv_b, w_down, gpost, *, seq, tm=512, tf=512):
    t, d = h.shape
    nf = D_FF // tf
    cb2 = conv_b.reshape(1, 2 * D_FF)
    return pl.pallas_call(
        functools.partial(_ffn_kernel, tiles_per_seq=seq // tm),
        out_shape=jax.ShapeDtypeStruct((t, d), _f32),
        grid=(t // tm, nf),
        in_specs=[
            pl.BlockSpec((tm, d), lambda i, j: (i, 0)),
            pl.BlockSpec((1, d), lambda i, j: (0, 0)),
            pl.BlockSpec((d, tf), lambda i, j: (0, j)),
            pl.BlockSpec((d, tf), lambda i, j: (0, nf + j)),
            pl.BlockSpec((3, tf), lambda i, j: (0, j)),
            pl.BlockSpec((3, tf), lambda i, j: (0, nf + j)),
            pl.BlockSpec((1, tf), lambda i, j: (0, j)),
            pl.BlockSpec((1, tf), lambda i, j: (0, nf + j)),
            pl.BlockSpec((tf, d), lambda i, j: (j, 0)),
            pl.BlockSpec((1, d), lambda i, j: (0, 0)),
        ],
        out_specs=pl.BlockSpec((tm, d), lambda i, j: (i, 0)),
        scratch_shapes=[
            pltpu.VMEM((tm, d), _bf16),
            pltpu.VMEM((tm, d), _f32),
            pltpu.VMEM((nf, 8, tf), _f32),
            pltpu.VMEM((nf, 8, tf), _f32),
        ],
        compiler_params=_params("arbitrary", "arbitrary"),
        name="conv_ffn",
    )(h, gpre, w_up, w_up, conv_w, conv_w, cb2, cb2, w_down, gpost)


def kernel(x, norm_pre_mix, norm_post_mix, norm_pre_ffn, norm_post_ffn, w_in, attn_sinks, hgrn_lb_logits,
           hgrn_out_norm, w_branch_a, w_branch_b, w_branch_c, w_out, w_ffn_up, ffn_conv_w, ffn_conv_b,
           w_ffn_down):
    bsz, seq, d = x.shape
    depth = w_in.shape[0]
    h = x.reshape(bsz * seq, d)
    row = lambda a: a.reshape(1, -1)
    for l in range(depth):
        nat, g1, g2 = _in_proj(h, row(norm_pre_mix[l]), _permute_w_in(w_in[l]), bsz=bsz, seq=seq)
        nat3 = nat.reshape(bsz, seq, NAT_W)
        oa = _dilated(nat3, g1, g2)
        ob = _swa(nat3, attn_sinks[l])
        oc = _hgrn(nat3, hgrn_lb_logits, row(hgrn_out_norm[l]), layer=l)
        h = _merge(oa, ob, oc, nat, h, w_branch_a[l].astype(_bf16), w_branch_b[l].astype(_bf16),
                   w_branch_c[l].astype(_bf16), w_out[l].astype(_bf16), row(norm_post_mix[l]))
        h = _ffn(h, row(norm_pre_ffn[l]), w_ffn_up[l].astype(_bf16), ffn_conv_w[l], ffn_conv_b[l],
                 w_ffn_down[l].astype(_bf16), row(norm_post_ffn[l]), seq=seq)
    return h.reshape(bsz, seq, d)
```

```python
import functools

import jax
import jax.numpy as jnp
from jax import lax
from jax.experimental import pallas as pl
from jax.experimental.pallas import tpu as pltpu

D_MODEL = 1024
HEAD_DIM = 64
DIL_PATTERNS = ((128, 1), (512, 4), (2048, 16))
SWA_WINDOW = 128
SWA_KV_HEADS = 2
HGRN_HEADS = 4
HGRN_DK = 128
HGRN_CHUNK = 32
D_FF = 4 * D_MODEL
BAND = 128
GROUP_W = 256
QKV_W = 3 * GROUP_W
EPS = 1e-6
LANES = 128
NEG = -1e30
ATTN_UNROLL = 4

SRC_AQ, SRC_AK, SRC_AV = 0, 768, 1536
SRC_B, SRC_C, SRC_GATES = 2304, 3072, 5120
NAT_GATES = 0
NAT_A0 = 3072
NAT_BQ, NAT_BK, NAT_BV = 3840, 4352, 4480
NAT_CQ, NAT_CF, NAT_CI, NAT_CG = 4608, 5120, 5632, 6144
NAT_W = 6656
D_IN = NAT_W + 2 * QKV_W

VMEM_LIMIT = 56 * 1024 * 1024

_f32 = jnp.float32
_bf16 = jnp.bfloat16


def _dot(a, b):
    return jnp.dot(a, b, preferred_element_type=_f32)


def _dot_nt(a, b):
    return lax.dot_general(a, b, (((1,), (1,)), ((), ())), preferred_element_type=_f32)


def _dot_tn(a, b):
    return lax.dot_general(a, b, (((0,), (0,)), ((), ())), preferred_element_type=_f32)


def _rms(x, g):
    return x * lax.rsqrt(jnp.mean(x * x, axis=-1, keepdims=True) + EPS) * g


def _sigmoid(x):
    return 1.0 / (1.0 + jnp.exp(-x))


def _params(*sem):
    return pltpu.CompilerParams(dimension_semantics=sem, vmem_limit_bytes=VMEM_LIMIT)


def _permute_w_in(w):
    cols = lambda start, width: w[:, start:start + width]
    grp = lambda gi: [cols(SRC_AQ + gi * GROUP_W, GROUP_W), cols(SRC_AK + gi * GROUP_W, GROUP_W),
                      cols(SRC_AV + gi * GROUP_W, GROUP_W)]
    parts = [cols(SRC_GATES, 3 * D_MODEL)] + grp(0) + [cols(SRC_B, SRC_C - SRC_B), cols(SRC_C, SRC_GATES - SRC_C)]
    parts += grp(1) + grp(2)
    return jnp.concatenate(parts, axis=1).astype(_bf16)


def _in_proj_kernel(x_ref, g_ref, w_ref, nat_ref, g1_ref, g2_ref, hn_ref, *, tn):
    tm = x_ref.shape[0]
    hn = _rms(x_ref[...], g_ref[...])
    nlc = hn_ref.shape[0]
    for c in range(nlc):
        hn_ref[c] = hn[:, c * LANES:(c + 1) * LANES]
    hb = hn.astype(_bf16)
    for c in range(NAT_W // tn):
        nat_ref[:, c * tn:(c + 1) * tn] = _dot(hb, w_ref[:, c * tn:(c + 1) * tn]).astype(_bf16)
    for out_ref, col0 in ((g1_ref, NAT_W), (g2_ref, NAT_W + QKV_W)):
        d = out_ref.shape[0]
        rows = tm // d
        lhs = jnp.concatenate(
            [jnp.concatenate([hn_ref[c, pl.ds(r, rows, stride=d), :] for c in range(nlc)], axis=1)
             for r in range(d)], axis=0).astype(_bf16)
        y = _dot(lhs, w_ref[:, col0:col0 + QKV_W]).astype(_bf16)
        for r in range(d):
            out_ref[r] = y[r * rows:(r + 1) * rows]


def _in_proj(x, g, w, *, bsz, seq, tm=512, tn=512):
    t, d = x.shape
    tps = seq // tm
    d1, d2 = DIL_PATTERNS[1][1], DIL_PATTERNS[2][1]
    return pl.pallas_call(
        functools.partial(_in_proj_kernel, tn=tn),
        out_shape=(jax.ShapeDtypeStruct((t, NAT_W), _bf16),
                   jax.ShapeDtypeStruct((bsz, d1, seq // d1, QKV_W), _bf16),
                   jax.ShapeDtypeStruct((bsz, d2, seq // d2, QKV_W), _bf16)),
        grid=(t // tm,),
        in_specs=[
            pl.BlockSpec((tm, d), lambda i: (i, 0)),
            pl.BlockSpec((1, d), lambda i: (0, 0)),
            pl.BlockSpec((d, D_IN), lambda i: (0, 0), pipeline_mode=pl.Buffered(1)),
        ],
        out_specs=(
            pl.BlockSpec((tm, NAT_W), lambda i: (i, 0)),
            pl.BlockSpec((None, d1, tm // d1, QKV_W), lambda i: (i // tps, 0, i % tps, 0)),
            pl.BlockSpec((None, d2, tm // d2, QKV_W), lambda i: (i // tps, 0, i % tps, 0)),
        ),
        scratch_shapes=[pltpu.VMEM((d // LANES, tm, LANES), _f32)],
        compiler_params=_params("arbitrary"),
        name="in_proj",
    )(x, g, w)


def _band_bias(max_dist, nk, first):
    qi = lax.broadcasted_iota(jnp.int32, (BAND, nk), 0)
    ki = lax.broadcasted_iota(jnp.int32, (BAND, nk), 1)
    dist = qi + (nk - BAND) - ki
    ok = (dist >= 0) & (dist <= max_dist)
    if first:
        ok = ok & (ki >= nk - BAND)
    return jnp.where(ok, 0.0, NEG).astype(_f32)


def _band_block(q, kcat, vcat, bias, *, sink_of_head, want_lse):
    nh = GROUP_W // HEAD_DIM
    nk = kcat.shape[0]
    has_prev = nk == 2 * BAND
    lane_head = lax.broadcasted_iota(jnp.int32, (BAND, GROUP_W), 1) // HEAD_DIM
    q = q * jnp.asarray(HEAD_DIM ** -0.5, _bf16)
    qs = jnp.concatenate([jnp.where(lane_head == h, q, jnp.zeros_like(q)) for h in range(nh)], axis=0)
    s = _dot_nt(qs, kcat).reshape(nh, BAND, nk) + bias[None]
    fold = (lambda x, op: op(x[..., :BAND], x[..., BAND:])) if has_prev else (lambda x, op: x)
    m = jnp.max(fold(s, jnp.maximum), axis=-1, keepdims=True)
    if sink_of_head is not None:
        head = lax.broadcasted_iota(jnp.int32, (nh, BAND, 1), 0)
        sink = jnp.zeros((nh, BAND, 1), _f32)
        for h in range(nh):
            sink = jnp.where(head == h, sink_of_head(h), sink)
        m = jnp.maximum(m, sink)
    p = jnp.exp(s - m)
    den = jnp.sum(fold(p, jnp.add), axis=-1, keepdims=True)
    if sink_of_head is not None:
        den = den + jnp.exp(sink - m)
    pv = _dot(p.reshape(nh * BAND, nk).astype(_bf16), vcat).reshape(nh, BAND, GROUP_W)
    pv = pv * (1.0 / den)
    o = pv[0]
    for h in range(1, nh):
        o = jnp.where(lane_head == h, pv[h], o)
    if not want_lse:
        return o, None
    lse = m + jnp.log(den)
    lse_b = jnp.broadcast_to(lse[0], (BAND, GROUP_W))
    for h in range(1, nh):
        lse_b = jnp.where(lane_head == h, lse[h], lse_b)
    return o, lse_b


def _qkv_cols(ref, rows, lead=()):
    idx = lambda c: lead + (rows, slice(c * GROUP_W, (c + 1) * GROUP_W))
    return ref[idx(0)], ref[idx(1)], ref[idx(2)]


def _dil_kernel(a0_ref, g1_ref, g2_ref, oa_ref, o_s, lse_s, bias_s):
    seq = a0_ref.shape[0]
    blk = functools.partial(_band_block, sink_of_head=None, want_lse=True)
    halves = [slice(hf * LANES, (hf + 1) * LANES) for hf in range(GROUP_W // LANES)]
    win0, win1, win2 = (w // d for w, d in DIL_PATTERNS)
    d1, d2 = DIL_PATTERNS[1][1], DIL_PATTERNS[2][1]
    assert win0 == win1 and seq // d2 == BAND and win2 >= BAND - 1
    bias_s[0] = _band_bias(win0, 2 * BAND, True)
    bias_s[1] = _band_bias(win0, 2 * BAND, False)
    bias_s[2, :, 0:BAND] = _band_bias(win2, BAND, True)

    def put(g, rows, o, lse):
        for hf, ls in enumerate(halves):
            o_s[g, hf, rows, :] = o[:, ls]
            lse_s[g, hf, rows, :] = lse[:, ls]

    def band(ref, lead, i):
        cur = pl.ds(pl.multiple_of(i * BAND, BAND), BAND)
        prev = pl.ds(pl.multiple_of(jnp.maximum(i - 1, 0) * BAND, BAND), BAND)
        q, kc, vc = _qkv_cols(ref, cur, lead)
        _, kp, vp = _qkv_cols(ref, prev, lead)
        return blk(q, jnp.concatenate([kp, kc], axis=0), jnp.concatenate([vp, vc], axis=0),
                   bias_s[jnp.minimum(i, 1)])

    def g0_body(i, c):
        o, lse = band(a0_ref, (), i)
        put(0, pl.ds(pl.multiple_of(i * BAND, BAND), BAND), o, lse)
        return c

    lax.fori_loop(0, seq // BAND, g0_body, 0, unroll=ATTN_UNROLL)

    nb1 = seq // d1 // BAND

    def g1_body(n, c):
        r, i = n // nb1, n % nb1
        o, lse = band(g1_ref, (r,), i)
        put(1, pl.ds(i * (BAND * d1) + r, BAND, stride=d1), o, lse)
        return c

    lax.fori_loop(0, d1 * nb1, g1_body, 0, unroll=ATTN_UNROLL)

    def g2_body(r, c):
        q, kc, vc = _qkv_cols(g2_ref, slice(None), (r,))
        o, lse = blk(q, kc, vc, bias_s[2, :, 0:BAND])
        put(2, pl.ds(r, BAND, stride=d2), o, lse)
        return c

    lax.fori_loop(0, d2, g2_body, 0, unroll=ATTN_UNROLL)

    def mix_body(i, c):
        rows = pl.ds(pl.multiple_of(i * BAND, BAND), BAND)
        for hf, ls in enumerate(halves):
            l0, l1, l2 = lse_s[0, hf, rows, :], lse_s[1, hf, rows, :], lse_s[2, hf, rows, :]
            m = jnp.maximum(jnp.maximum(l0, l1), l2)
            e0, e1, e2 = jnp.exp(l0 - m), jnp.exp(l1 - m), jnp.exp(l2 - m)
            oa = (e0 * o_s[0, hf, rows, :] + e1 * o_s[1, hf, rows, :] + e2 * o_s[2, hf, rows, :]) / (e0 + e1 + e2)
            oa_ref[rows, ls] = oa.astype(oa_ref.dtype)
        return c

    lax.fori_loop(0, seq // BAND, mix_body, 0)


def _dilated(nat3, g1, g2):
    bsz, seq, _ = nat3.shape
    n_grp = len(DIL_PATTERNS)
    return pl.pallas_call(
        _dil_kernel,
        out_shape=jax.ShapeDtypeStruct((bsz, seq, GROUP_W), _bf16),
        grid=(bsz,),
        in_specs=[
            pl.BlockSpec((None, seq, QKV_W), lambda b: (b, 0, NAT_A0 // QKV_W)),
            pl.BlockSpec((None,) + g1.shape[1:], lambda b: (b, 0, 0, 0)),
            pl.BlockSpec((None,) + g2.shape[1:], lambda b: (b, 0, 0, 0)),
        ],
        out_specs=pl.BlockSpec((None, seq, GROUP_W), lambda b: (b, 0, 0)),
        scratch_shapes=[pltpu.VMEM((n_grp, GROUP_W // LANES, seq, LANES), _f32)] * 2
        + [pltpu.VMEM((3, BAND, 2 * BAND), _f32)],
        compiler_params=_params("arbitrary"),
        name="dil_attn",
    )(nat3, g1, g2).reshape(bsz * seq, GROUP_W)


def _swa_kernel(sink_ref, q_ref, k_ref, v_ref, o_ref, kx_ref, vx_ref, bias_s):
    seq = q_ref.shape[0]
    nb = seq // BAND
    kvh = pl.program_id(1)
    bias_s[0] = _band_bias(SWA_WINDOW - 1, 2 * BAND, True)
    bias_s[1] = _band_bias(SWA_WINDOW - 1, 2 * BAND, False)
    lane = lax.broadcasted_iota(jnp.int32, (BAND, 2 * HEAD_DIM), 1)
    keep = (lane < HEAD_DIM) == (kvh == 0)

    def expand(i, c):
        r0 = pl.multiple_of(i * BAND, BAND)
        for src, dst in ((k_ref, kx_ref), (v_ref, vx_ref)):
            x = src[pl.ds(r0, BAND), :].astype(_f32)
            x2 = jnp.where(keep, x, pltpu.roll(x, HEAD_DIM, axis=1)).astype(_bf16)
            dst[pl.ds(r0, BAND), :] = jnp.concatenate([x2, x2], axis=1)
        return c

    lax.fori_loop(0, nb, expand, 0)
    heads_per_kv = GROUP_W // HEAD_DIM

    def body(i, c):
        cur = pl.ds(pl.multiple_of(i * BAND, BAND), BAND)
        prev = pl.ds(pl.multiple_of(jnp.maximum(i - 1, 0) * BAND, BAND), BAND)
        o, _ = _band_block(q_ref[cur, :], jnp.concatenate([kx_ref[prev, :], kx_ref[cur, :]], axis=0),
                           jnp.concatenate([vx_ref[prev, :], vx_ref[cur, :]], axis=0),
                           bias_s[jnp.minimum(i, 1)], want_lse=False,
                           sink_of_head=lambda h: sink_ref[kvh * heads_per_kv + h])
        o_ref[cur, :] = o.astype(o_ref.dtype)
        return c

    lax.fori_loop(0, nb, body, 0, unroll=ATTN_UNROLL)


def _swa(nat3, sinks):
    bsz, seq, _ = nat3.shape
    kvw = SWA_KV_HEADS * HEAD_DIM
    return pl.pallas_call(
        _swa_kernel,
        out_shape=jax.ShapeDtypeStruct((bsz, seq, SWA_KV_HEADS * GROUP_W), _bf16),
        grid=(bsz, SWA_KV_HEADS),
        in_specs=[
            pl.BlockSpec(memory_space=pltpu.SMEM),
            pl.BlockSpec((None, seq, GROUP_W), lambda b, g: (b, 0, NAT_BQ // GROUP_W + g)),
            pl.BlockSpec((None, seq, kvw), lambda b, g: (b, 0, NAT_BK // kvw)),
            pl.BlockSpec((None, seq, kvw), lambda b, g: (b, 0, NAT_BV // kvw)),
        ],
        out_specs=pl.BlockSpec((None, seq, GROUP_W), lambda b, g: (b, 0, g)),
        scratch_shapes=[pltpu.VMEM((seq, GROUP_W), _bf16), pltpu.VMEM((seq, GROUP_W), _bf16),
                        pltpu.VMEM((2, BAND, 2 * BAND), _f32)],
        compiler_params=_params("arbitrary", "arbitrary"),
        name="swa_attn",
    )(sinks, nat3, nat3, nat3).reshape(bsz * seq, SWA_KV_HEADS * GROUP_W)


def _split3(x):
    hi = x.astype(_bf16)
    r = x - hi.astype(_f32)
    mid = r.astype(_bf16)
    lo = (r - mid.astype(_f32)).astype(_bf16)
    return hi, mid, lo


def _hgrn_kernel(lbl_ref, gain_ref, q_ref, f_ref, i_ref, g_ref, o_ref, st_ref, *, layer, tc):
    @pl.when(pl.program_id(1) == 0)
    def _():
        st_ref[...] = jnp.zeros_like(st_ref)

    logits = lbl_ref[...]
    e = jnp.exp(logits - jnp.max(logits, axis=0, keepdims=True))
    sm = e / jnp.sum(e, axis=0, keepdims=True)
    lb = jnp.zeros((1, sm.shape[1]), _f32)
    for j in range(1, layer + 1):
        lb = lb + sm[j:j + 1, :]
    log_lb = jnp.log(lb)
    log_1mlb = jnp.log1p(-lb)

    ti = lax.broadcasted_iota(jnp.int32, (tc, tc), 0)
    si = lax.broadcasted_iota(jnp.int32, (tc, tc), 1)
    same = (ti // HGRN_CHUNK) == (si // HGRN_CHUNK)
    causal = jnp.logical_and(same, si <= ti)
    sum_mat = jnp.concatenate([causal, same], axis=0).astype(_f32).astype(_bf16)
    nchunk = tc // HGRN_CHUNK

    for h in range(HGRN_HEADS):
        sl = slice(h * HGRN_DK, (h + 1) * HGRN_DK)
        qraw = q_ref[:, sl].astype(_f32)
        q = qraw * _sigmoid(qraw)
        fz = f_ref[:, sl].astype(_f32)
        v = i_ref[:, sl]
        log_sig = jnp.minimum(fz, 0.0) - jnp.log1p(jnp.exp(-jnp.abs(fz)))
        a = log_lb[:, sl]
        bterm = log_1mlb[:, sl] + log_sig
        logf = jnp.maximum(a, bterm) + jnp.log1p(jnp.exp(-jnp.abs(a - bterm)))
        k = (1.0 - lb[:, sl]) * _sigmoid(-fz)

        sums = _dot(sum_mat, jnp.concatenate(_split3(logf), axis=1))
        sums = sums[:, 0:HGRN_DK] + sums[:, HGRN_DK:2 * HGRN_DK] + sums[:, 2 * HGRN_DK:]
        b = sums[0:tc]
        tot = sums[tc:]
        qe = (q * jnp.exp(b)).astype(_bf16)
        ke = (k * jnp.exp(-b)).astype(_bf16)
        kd = (k * jnp.exp(tot - b)).astype(_bf16)

        att = jnp.where(causal, _dot_nt(qe, ke), 0.0)
        o = _dot(att.astype(_bf16), v)

        st = st_ref[h]
        inter = []
        for n in range(nchunk):
            rs = slice(n * HGRN_CHUNK, (n + 1) * HGRN_CHUNK)
            inter.append(_dot_nt(qe[rs], st.astype(_bf16)))
            decay = jnp.exp(tot[n * HGRN_CHUNK:n * HGRN_CHUNK + 1, :])
            st = st * decay + _dot_tn(v[rs], kd[rs])
        st_ref[h] = st
        o = o + jnp.concatenate(inter, axis=0)

        y = _rms(o, gain_ref[...])
        graw = g_ref[:, sl].astype(_f32)
        o_ref[:, sl] = (y * (graw * _sigmoid(graw))).astype(o_ref.dtype)


def _hgrn(nat3, lb_logits, gain, *, layer, tc=256):
    bsz, seq, _ = nat3.shape
    w = HGRN_HEADS * HGRN_DK
    spec = lambda col: pl.BlockSpec((None, tc, w), lambda b, t: (b, t, col // w))
    return pl.pallas_call(
        functools.partial(_hgrn_kernel, layer=layer, tc=tc),
        out_shape=jax.ShapeDtypeStruct((bsz, seq, w), _bf16),
        grid=(bsz, seq // tc),
        in_specs=[
            pl.BlockSpec(lb_logits.shape, lambda b, t: (0, 0)),
            pl.BlockSpec((1, HGRN_DK), lambda b, t: (0, 0)),
            spec(NAT_CQ), spec(NAT_CF), spec(NAT_CI), spec(NAT_CG),
        ],
        out_specs=pl.BlockSpec((None, tc, w), lambda b, t: (b, t, 0)),
        scratch_shapes=[pltpu.VMEM((HGRN_HEADS, HGRN_DK, HGRN_DK), _f32)],
        compiler_params=_params("arbitrary", "arbitrary"),
        name="hgrn2",
    )(lb_logits, gain, nat3, nat3, nat3, nat3).reshape(bsz * seq, w)


def _merge_kernel(oa_ref, ob_ref, oc_ref, g0_ref, g1_ref, g2_ref, h_ref, wa_ref, wb_ref, wc_ref, wo_ref, gp_ref,
                  out_ref):
    ya = _dot(oa_ref[...], wa_ref[...])
    yb = _dot(ob_ref[...], wb_ref[...])
    yc = _dot(oc_ref[...], wc_ref[...])
    mix = (_sigmoid(g0_ref[...].astype(_f32)) * ya + _sigmoid(g1_ref[...].astype(_f32)) * yb
           + _sigmoid(g2_ref[...].astype(_f32)) * yc)
    out = _dot(mix.astype(_bf16), wo_ref[...])
    out_ref[...] = h_ref[...] + _rms(out, gp_ref[...])


def _merge(oa, ob, oc, nat, h, wa, wb, wc, wo, gp, *, tm=512):
    t, d = h.shape
    row = lambda a: pl.BlockSpec((tm, a.shape[1]), lambda i: (i, 0))
    full = lambda a: pl.BlockSpec(a.shape, lambda i: (0, 0))
    gate = lambda j: pl.BlockSpec((tm, d), lambda i: (i, NAT_GATES // d + j))
    return pl.pallas_call(
        _merge_kernel,
        out_shape=jax.ShapeDtypeStruct((t, d), _f32),
        grid=(t // tm,),
        in_specs=[row(oa), row(ob), row(oc), gate(0), gate(1), gate(2), row(h),
                  full(wa), full(wb), full(wc), full(wo), full(gp)],
        out_specs=row(h),
        compiler_params=_params("arbitrary"),
        name="merge_out",
    )(oa, ob, oc, nat, nat, nat, h, wa, wb, wc, wo, gp)


HALO = 8
FFN_ROWS = 64


def _ffn_kernel(xc_ref, xp_ref, gpre_ref, wa_ref, wb_ref,
                cwa_p, cwb_p, cba_p, cbb_p, wd_p, cwa_c, cwb_c, cba_c, cbb_c, wd_c, gpost_ref,
                out_ref, hn_ref, acc_ref, carry_ref, u0_ref, u1_ref, t_ref, *, n_chunks, tiles_per_seq, total_chunks):
    s = pl.program_id(0)
    tm = xc_ref.shape[0]
    tf = u0_ref.shape[1] // 2
    c0 = 2 * s
    new_tile = (c0 % n_chunks) == 0

    @pl.when(s == 0)
    def _():
        u1_ref[...] = jnp.zeros_like(u1_ref)
        acc_ref[...] = jnp.zeros_like(acc_ref)
        carry_ref[...] = jnp.zeros_like(carry_ref)

    @pl.when(new_tile)
    def _():
        hn_ref[...] = _rms(xc_ref[...], gpre_ref[...]).astype(_bf16)

    def up(u_ref, half):
        cols = slice(half * tf, (half + 1) * tf)
        u_ref[HALO:, 0:tf] = _dot(hn_ref[...], wa_ref[:, cols])
        u_ref[HALO:, tf:] = _dot(hn_ref[...], wb_ref[:, cols])

    def gate_down(u_ref, chunk, cwa_ref, cwb_ref, cba_ref, cbb_ref, wd_ref):
        j = chunk % n_chunks
        seq_start = ((chunk // n_chunks) % tiles_per_seq) == 0
        u_ref[0:HALO, :] = jnp.where(seq_start, 0.0, carry_ref[j])
        carry_ref[j] = u_ref[tm:tm + HALO, :]

        def conv(rows, cols, cw_ref, cbias_ref, lanes):
            x = u_ref[rows, cols]
            cw = cw_ref[:, lanes]
            return (cw[2:3] * x[HALO:] + cw[1:2] * pltpu.roll(x, 1, axis=0)[HALO:]
                    + cw[0:1] * pltpu.roll(x, 2, axis=0)[HALO:] + cbias_ref[:, lanes])

        for r0 in range(0, tm, FFN_ROWS):
            rows = slice(r0, r0 + FFN_ROWS + HALO)
            for l0 in range(0, tf, LANES):
                lanes = slice(l0, l0 + LANES)
                a = conv(rows, lanes, cwa_ref, cba_ref, lanes)
                b = conv(rows, slice(tf + l0, tf + l0 + LANES), cwb_ref, cbb_ref, lanes)
                gelu = 0.5 * a * (1.0 + jnp.tanh(0.7978845608028654 * (a + 0.044715 * (a * a * a))))
                t_ref[r0:r0 + FFN_ROWS, lanes] = (gelu * b).astype(_bf16)
        acc_ref[...] += _dot(t_ref[...], wd_ref[...])

    up(u0_ref, 0)
    gate_down(u1_ref, jnp.maximum(c0 - 1, 0), cwa_p, cwb_p, cba_p, cbb_p, wd_p)

    @pl.when(new_tile)
    def _():
        out_ref[...] = xp_ref[...] + _rms(acc_ref[...], gpost_ref[...])
        acc_ref[...] = jnp.zeros_like(acc_ref)

    up(u1_ref, 1)
    gate_down(u0_ref, jnp.minimum(c0, total_chunks - 1), cwa_c, cwb_c, cba_c, cbb_c, wd_c)


def _ffn(h, gpre, w_up, conv_w, conv_b, w_down, gpost, *, seq, tm=512, tf=512):
    t, d = h.shape
    n_chunks = D_FF // tf
    n_tiles = t // tm
    total = n_tiles * n_chunks
    steps = total // 2 + 1
    cb2 = conv_b.reshape(1, 2 * D_FF)
    tile_up = lambda s: jnp.minimum(2 * s // n_chunks, n_tiles - 1)
    prev = lambda s: jnp.maximum(2 * s - 1, 0)
    cur = lambda s: jnp.minimum(2 * s, total - 1)
    pair = lambda s: jnp.minimum(s, steps - 2) % (n_chunks // 2)

    def chunk_specs(chunk):
        j = lambda s: chunk(s) % n_chunks
        return [pl.BlockSpec((3, tf), lambda s: (0, j(s))), pl.BlockSpec((3, tf), lambda s: (0, n_chunks + j(s))),
                pl.BlockSpec((1, tf), lambda s: (0, j(s))), pl.BlockSpec((1, tf), lambda s: (0, n_chunks + j(s))),
                pl.BlockSpec((tf, d), lambda s: (j(s), 0))]

    chunk_args = [conv_w, conv_w, cb2, cb2, w_down]
    return pl.pallas_call(
        functools.partial(_ffn_kernel, n_chunks=n_chunks, tiles_per_seq=seq // tm, total_chunks=total),
        out_shape=jax.ShapeDtypeStruct((t, d), _f32),
        grid=(steps,),
        in_specs=[
            pl.BlockSpec((tm, d), lambda s: (tile_up(s), 0)),
            pl.BlockSpec((tm, d), lambda s: (prev(s) // n_chunks, 0)),
            pl.BlockSpec((1, d), lambda s: (0, 0)),
            pl.BlockSpec((d, 2 * tf), lambda s: (0, pair(s))),
            pl.BlockSpec((d, 2 * tf), lambda s: (0, n_chunks // 2 + pair(s))),
        ] + chunk_specs(prev) + chunk_specs(cur) + [pl.BlockSpec((1, d), lambda s: (0, 0))],
        out_specs=pl.BlockSpec((tm, d), lambda s: (prev(s) // n_chunks, 0)),
        scratch_shapes=[
            pltpu.VMEM((tm, d), _bf16),
            pltpu.VMEM((tm, d), _f32),
            pltpu.VMEM((n_chunks, HALO, 2 * tf), _f32),
            pltpu.VMEM((HALO + tm, 2 * tf), _f32),
            pltpu.VMEM((HALO + tm, 2 * tf), _f32),
            pltpu.VMEM((tm, tf), _bf16),
        ],
        compiler_params=_params("arbitrary"),
        name="conv_ffn",
    )(h, h, gpre, w_up, w_up, *chunk_args, *chunk_args, gpost)


def kernel(x, norm_pre_mix, norm_post_mix, norm_pre_ffn, norm_post_ffn, w_in, attn_sinks, hgrn_lb_logits,
           hgrn_out_norm, w_branch_a, w_branch_b, w_branch_c, w_out, w_ffn_up, ffn_conv_w, ffn_conv_b,
           w_ffn_down):
    bsz, seq, d = x.shape
    depth = w_in.shape[0]
    h = x.reshape(bsz * seq, d)
    row = lambda a: a.reshape(1, -1)
    for l in range(depth):
        nat, g1, g2 = _in_proj(h, row(norm_pre_mix[l]), _permute_w_in(w_in[l]), bsz=bsz, seq=seq)
        nat3 = nat.reshape(bsz, seq, NAT_W)
        oa = _dilated(nat3, g1, g2)
        ob = _swa(nat3, attn_sinks[l])
        oc = _hgrn(nat3, hgrn_lb_logits, row(hgrn_out_norm[l]), layer=l)
        h = _merge(oa, ob, oc, nat, h, w_branch_a[l].astype(_bf16), w_branch_b[l].astype(_bf16),
                   w_branch_c[l].astype(_bf16), w_out[l].astype(_bf16), row(norm_post_mix[l]))
        h = _ffn(h, row(norm_pre_ffn[l]), w_ffn_up[l].astype(_bf16), ffn_conv_w[l], ffn_conv_b[l],
                 w_ffn_down[l].astype(_bf16), row(norm_post_ffn[l]), seq=seq)
    return h.reshape(bsz, seq, d)
```

```python
import functools

import jax
import jax.numpy as jnp
from jax import lax
from jax.experimental import pallas as pl
from jax.experimental.pallas import tpu as pltpu

D_MODEL = 1024
HEAD_DIM = 64
DIL_PATTERNS = ((128, 1), (512, 4), (2048, 16))
SWA_WINDOW = 128
SWA_KV_HEADS = 2
HGRN_HEADS = 4
HGRN_DK = 128
HGRN_CHUNK = 32
HGRN_ATT_BLOCK = 128
D_FF = 4 * D_MODEL
BAND = 128
GROUP_W = 256
QKV_W = 3 * GROUP_W
EPS = 1e-6
LANES = 128
NEG = -1e30
ATTN_UNROLL = 4

SRC_AQ, SRC_AK, SRC_AV = 0, 768, 1536
SRC_B, SRC_C, SRC_GATES = 2304, 3072, 5120
NAT_GATES = 0
NAT_A0 = 3072
NAT_BQ, NAT_BK, NAT_BV = 3840, 4352, 4480
NAT_CQ, NAT_CF, NAT_CI, NAT_CG = 4608, 5120, 5632, 6144
NAT_W = 6656
D_IN = NAT_W + 2 * QKV_W

VMEM_LIMIT = 56 * 1024 * 1024

_f32 = jnp.float32
_bf16 = jnp.bfloat16


def _dot(a, b):
    return jnp.dot(a, b, preferred_element_type=_f32)


def _dot_nt(a, b):
    return lax.dot_general(a, b, (((1,), (1,)), ((), ())), preferred_element_type=_f32)


def _dot_tn(a, b):
    return lax.dot_general(a, b, (((0,), (0,)), ((), ())), preferred_element_type=_f32)


def _rms(x, g):
    return x * lax.rsqrt(jnp.mean(x * x, axis=-1, keepdims=True) + EPS) * g


def _sigmoid(x):
    return 1.0 / (1.0 + jnp.exp(-x))


def _params(*sem):
    return pltpu.CompilerParams(dimension_semantics=sem, vmem_limit_bytes=VMEM_LIMIT)


def _permute_w_in(w):
    cols = lambda start, width: w[:, start:start + width]
    grp = lambda gi: [cols(SRC_AQ + gi * GROUP_W, GROUP_W), cols(SRC_AK + gi * GROUP_W, GROUP_W),
                      cols(SRC_AV + gi * GROUP_W, GROUP_W)]
    parts = [cols(SRC_GATES, 3 * D_MODEL)] + grp(0) + [cols(SRC_B, SRC_C - SRC_B), cols(SRC_C, SRC_GATES - SRC_C)]
    parts += grp(1) + grp(2)
    return jnp.concatenate(parts, axis=1).astype(_bf16)


def _in_proj_kernel(x_ref, g_ref, w_ref, nat_ref, g1_ref, g2_ref, hn_ref, *, tn):
    tm = x_ref.shape[0]
    hn = _rms(x_ref[...], g_ref[...])
    nlc = hn_ref.shape[0]
    for c in range(nlc):
        hn_ref[c] = hn[:, c * LANES:(c + 1) * LANES]
    hb = hn.astype(_bf16)
    for c in range(NAT_W // tn):
        nat_ref[:, c * tn:(c + 1) * tn] = _dot(hb, w_ref[:, c * tn:(c + 1) * tn]).astype(_bf16)
    for out_ref, col0 in ((g1_ref, NAT_W), (g2_ref, NAT_W + QKV_W)):
        d = out_ref.shape[0]
        rows = tm // d
        lhs = jnp.concatenate(
            [jnp.concatenate([hn_ref[c, pl.ds(r, rows, stride=d), :] for c in range(nlc)], axis=1)
             for r in range(d)], axis=0).astype(_bf16)
        y = _dot(lhs, w_ref[:, col0:col0 + QKV_W]).astype(_bf16)
        for r in range(d):
            out_ref[r] = y[r * rows:(r + 1) * rows]


def _in_proj(x, g, w, *, bsz, seq, tm=512, tn=512):
    t, d = x.shape
    tps = seq // tm
    d1, d2 = DIL_PATTERNS[1][1], DIL_PATTERNS[2][1]
    return pl.pallas_call(
        functools.partial(_in_proj_kernel, tn=tn),
        out_shape=(jax.ShapeDtypeStruct((t, NAT_W), _bf16),
                   jax.ShapeDtypeStruct((bsz, d1, seq // d1, QKV_W), _bf16),
                   jax.ShapeDtypeStruct((bsz, d2, seq // d2, QKV_W), _bf16)),
        grid=(t // tm,),
        in_specs=[
            pl.BlockSpec((tm, d), lambda i: (i, 0)),
            pl.BlockSpec((1, d), lambda i: (0, 0)),
            pl.BlockSpec((d, D_IN), lambda i: (0, 0), pipeline_mode=pl.Buffered(1)),
        ],
        out_specs=(
            pl.BlockSpec((tm, NAT_W), lambda i: (i, 0)),
            pl.BlockSpec((None, d1, tm // d1, QKV_W), lambda i: (i // tps, 0, i % tps, 0)),
            pl.BlockSpec((None, d2, tm // d2, QKV_W), lambda i: (i // tps, 0, i % tps, 0)),
        ),
        scratch_shapes=[pltpu.VMEM((d // LANES, tm, LANES), _f32)],
        compiler_params=_params("arbitrary"),
        name="in_proj",
    )(x, g, w)


def _band_bias(max_dist, nk, first):
    qi = lax.broadcasted_iota(jnp.int32, (BAND, nk), 0)
    ki = lax.broadcasted_iota(jnp.int32, (BAND, nk), 1)
    dist = qi + (nk - BAND) - ki
    ok = (dist >= 0) & (dist <= max_dist)
    if first:
        ok = ok & (ki >= nk - BAND)
    return jnp.where(ok, 0.0, NEG).astype(_f32)


def _band_blocks(blocks, *, sink_of_head, want_lse):
    nh = GROUP_W // HEAD_DIM
    nk = blocks[0][1].shape[0]
    has_prev = nk == 2 * BAND
    lane_head = lax.broadcasted_iota(jnp.int32, (BAND, GROUP_W), 1) // HEAD_DIM
    fold = (lambda x, op: op(x[..., :BAND], x[..., BAND:])) if has_prev else (lambda x, op: x)
    scale = jnp.asarray(HEAD_DIM ** -0.5, _bf16)

    def stack_heads(q):
        q = q * scale
        return jnp.concatenate([jnp.where(lane_head == h, q, jnp.zeros_like(q)) for h in range(nh)], axis=0)

    def pick_heads(x):
        out = jnp.broadcast_to(x[0], (BAND, GROUP_W))
        for h in range(1, nh):
            out = jnp.where(lane_head == h, x[h], out)
        return out

    qs = [stack_heads(q) for q, _, _, _ in blocks]
    s = [_dot_nt(qh, kcat).reshape(nh, BAND, nk) + bias[None] for qh, (_, kcat, _, bias) in zip(qs, blocks)]
    m = [jnp.max(fold(x, jnp.maximum), axis=-1, keepdims=True) for x in s]
    if sink_of_head is not None:
        head = lax.broadcasted_iota(jnp.int32, (nh, BAND, 1), 0)
        sink = jnp.zeros((nh, BAND, 1), _f32)
        for h in range(nh):
            sink = jnp.where(head == h, sink_of_head(h), sink)
        m = [jnp.maximum(x, sink) for x in m]
    p = [jnp.exp(x - mx) for x, mx in zip(s, m)]
    den = [jnp.sum(fold(x, jnp.add), axis=-1, keepdims=True) for x in p]
    if sink_of_head is not None:
        den = [d + jnp.exp(sink - mx) for d, mx in zip(den, m)]
    pv = [_dot(x.reshape(nh * BAND, nk).astype(_bf16), vcat).reshape(nh, BAND, GROUP_W)
          for x, (_, _, vcat, _) in zip(p, blocks)]
    o = [pick_heads(x * (1.0 / d)) for x, d in zip(pv, den)]
    if not want_lse:
        return [(x, None) for x in o]
    return [(x, pick_heads(mx + jnp.log(d))) for x, mx, d in zip(o, m, den)]


def _qkv_cols(ref, rows, lead=()):
    idx = lambda c: lead + (rows, slice(c * GROUP_W, (c + 1) * GROUP_W))
    return ref[idx(0)], ref[idx(1)], ref[idx(2)]


def _dil_kernel(a0_ref, g1_ref, g2_ref, oa_ref, o_s, lse_s, bias_s):
    seq = a0_ref.shape[0]
    blks = functools.partial(_band_blocks, sink_of_head=None, want_lse=True)
    halves = [slice(hf * LANES, (hf + 1) * LANES) for hf in range(GROUP_W // LANES)]
    nu = ATTN_UNROLL
    win0, win1, win2 = (w // d for w, d in DIL_PATTERNS)
    d1, d2 = DIL_PATTERNS[1][1], DIL_PATTERNS[2][1]
    assert win0 == win1 and seq // d2 == BAND and win2 >= BAND - 1
    bias_s[0] = _band_bias(win0, 2 * BAND, True)
    bias_s[1] = _band_bias(win0, 2 * BAND, False)
    bias_s[2, :, 0:BAND] = _band_bias(win2, BAND, True)

    def put(g, rows, o, lse):
        for hf, ls in enumerate(halves):
            o_s[g, hf, rows, :] = o[:, ls]
            lse_s[g, hf, rows, :] = lse[:, ls]

    def band(ref, lead, i):
        cur = pl.ds(pl.multiple_of(i * BAND, BAND), BAND)
        prev = pl.ds(pl.multiple_of(jnp.maximum(i - 1, 0) * BAND, BAND), BAND)
        q, kc, vc = _qkv_cols(ref, cur, lead)
        _, kp, vp = _qkv_cols(ref, prev, lead)
        return (q, jnp.concatenate([kp, kc], axis=0), jnp.concatenate([vp, vc], axis=0),
                bias_s[jnp.minimum(i, 1)])

    def g0_body(n, c):
        ids = [n * nu + u for u in range(nu)]
        for i, (o, lse) in zip(ids, blks([band(a0_ref, (), i) for i in ids])):
            put(0, pl.ds(pl.multiple_of(i * BAND, BAND), BAND), o, lse)
        return c

    lax.fori_loop(0, seq // BAND // nu, g0_body, 0)

    nb1 = seq // d1 // BAND

    def g1_body(n, c):
        ids = [((n * nu + u) // nb1, (n * nu + u) % nb1) for u in range(nu)]
        for (r, i), (o, lse) in zip(ids, blks([band(g1_ref, (r,), i) for r, i in ids])):
            put(1, pl.ds(i * (BAND * d1) + r, BAND, stride=d1), o, lse)
        return c

    lax.fori_loop(0, d1 * nb1 // nu, g1_body, 0)

    def g2_body(n, c):
        ids = [n * nu + u for u in range(nu)]
        outs = blks([_qkv_cols(g2_ref, slice(None), (r,)) + (bias_s[2, :, 0:BAND],) for r in ids])
        for r, (o, lse) in zip(ids, outs):
            put(2, pl.ds(r, BAND, stride=d2), o, lse)
        return c

    lax.fori_loop(0, d2 // nu, g2_body, 0)

    def mix_body(i, c):
        rows = pl.ds(pl.multiple_of(i * BAND, BAND), BAND)
        for hf, ls in enumerate(halves):
            l0, l1, l2 = lse_s[0, hf, rows, :], lse_s[1, hf, rows, :], lse_s[2, hf, rows, :]
            m = jnp.maximum(jnp.maximum(l0, l1), l2)
            e0, e1, e2 = jnp.exp(l0 - m), jnp.exp(l1 - m), jnp.exp(l2 - m)
            oa = (e0 * o_s[0, hf, rows, :] + e1 * o_s[1, hf, rows, :] + e2 * o_s[2, hf, rows, :]) / (e0 + e1 + e2)
            oa_ref[rows, ls] = oa.astype(oa_ref.dtype)
        return c

    lax.fori_loop(0, seq // BAND, mix_body, 0)


def _dilated(nat3, g1, g2):
    bsz, seq, _ = nat3.shape
    n_grp = len(DIL_PATTERNS)
    return pl.pallas_call(
        _dil_kernel,
        out_shape=jax.ShapeDtypeStruct((bsz, seq, GROUP_W), _bf16),
        grid=(bsz,),
        in_specs=[
            pl.BlockSpec((None, seq, QKV_W), lambda b: (b, 0, NAT_A0 // QKV_W)),
            pl.BlockSpec((None,) + g1.shape[1:], lambda b: (b, 0, 0, 0)),
            pl.BlockSpec((None,) + g2.shape[1:], lambda b: (b, 0, 0, 0)),
        ],
        out_specs=pl.BlockSpec((None, seq, GROUP_W), lambda b: (b, 0, 0)),
        scratch_shapes=[pltpu.VMEM((n_grp, GROUP_W // LANES, seq, LANES), _f32)] * 2
        + [pltpu.VMEM((3, BAND, 2 * BAND), _f32)],
        compiler_params=_params("arbitrary"),
        name="dil_attn",
    )(nat3, g1, g2).reshape(bsz * seq, GROUP_W)


def _swa_kernel(sink_ref, q_ref, k_ref, v_ref, o_ref, kx_ref, vx_ref, bias_s):
    seq = q_ref.shape[0]
    nb = seq // BAND
    kvh = pl.program_id(1)
    bias_s[0] = _band_bias(SWA_WINDOW - 1, 2 * BAND, True)
    bias_s[1] = _band_bias(SWA_WINDOW - 1, 2 * BAND, False)
    lane = lax.broadcasted_iota(jnp.int32, (BAND, 2 * HEAD_DIM), 1)
    keep = (lane < HEAD_DIM) == (kvh == 0)

    def expand(i, c):
        r0 = pl.multiple_of(i * BAND, BAND)
        for src, dst in ((k_ref, kx_ref), (v_ref, vx_ref)):
            x = src[pl.ds(r0, BAND), :].astype(_f32)
            x2 = jnp.where(keep, x, pltpu.roll(x, HEAD_DIM, axis=1)).astype(_bf16)
            dst[pl.ds(r0, BAND), :] = jnp.concatenate([x2, x2], axis=1)
        return c

    lax.fori_loop(0, nb, expand, 0)
    heads_per_kv = GROUP_W // HEAD_DIM

    def band(i):
        cur = pl.ds(pl.multiple_of(i * BAND, BAND), BAND)
        prev = pl.ds(pl.multiple_of(jnp.maximum(i - 1, 0) * BAND, BAND), BAND)
        return (q_ref[cur, :], jnp.concatenate([kx_ref[prev, :], kx_ref[cur, :]], axis=0),
                jnp.concatenate([vx_ref[prev, :], vx_ref[cur, :]], axis=0), bias_s[jnp.minimum(i, 1)])

    def body(n, c):
        ids = [n * ATTN_UNROLL + u for u in range(ATTN_UNROLL)]
        outs = _band_blocks([band(i) for i in ids], want_lse=False,
                            sink_of_head=lambda h: sink_ref[kvh * heads_per_kv + h])
        for i, (o, _) in zip(ids, outs):
            o_ref[pl.ds(pl.multiple_of(i * BAND, BAND), BAND), :] = o.astype(o_ref.dtype)
        return c

    lax.fori_loop(0, nb // ATTN_UNROLL, body, 0)


def _swa(nat3, sinks):
    bsz, seq, _ = nat3.shape
    kvw = SWA_KV_HEADS * HEAD_DIM
    return pl.pallas_call(
        _swa_kernel,
        out_shape=jax.ShapeDtypeStruct((bsz, seq, SWA_KV_HEADS * GROUP_W), _bf16),
        grid=(bsz, SWA_KV_HEADS),
        in_specs=[
            pl.BlockSpec(memory_space=pltpu.SMEM),
            pl.BlockSpec((None, seq, GROUP_W), lambda b, g: (b, 0, NAT_BQ // GROUP_W + g)),
            pl.BlockSpec((None, seq, kvw), lambda b, g: (b, 0, NAT_BK // kvw)),
            pl.BlockSpec((None, seq, kvw), lambda b, g: (b, 0, NAT_BV // kvw)),
        ],
        out_specs=pl.BlockSpec((None, seq, GROUP_W), lambda b, g: (b, 0, g)),
        scratch_shapes=[pltpu.VMEM((seq, GROUP_W), _bf16), pltpu.VMEM((seq, GROUP_W), _bf16),
                        pltpu.VMEM((2, BAND, 2 * BAND), _f32)],
        compiler_params=_params("arbitrary", "arbitrary"),
        name="swa_attn",
    )(sinks, nat3, nat3, nat3).reshape(bsz * seq, SWA_KV_HEADS * GROUP_W)


def _split3(x):
    hi = x.astype(_bf16)
    r = x - hi.astype(_f32)
    mid = r.astype(_bf16)
    lo = (r - mid.astype(_f32)).astype(_bf16)
    return hi, mid, lo


def _hgrn_kernel(lbl_ref, gain_ref, q_ref, f_ref, i_ref, g_ref, o_ref, st_ref, *, layer, tc):
    @pl.when(pl.program_id(1) == 0)
    def _():
        st_ref[...] = jnp.zeros_like(st_ref)

    logits = lbl_ref[...]
    e = jnp.exp(logits - jnp.max(logits, axis=0, keepdims=True))
    sm = e / jnp.sum(e, axis=0, keepdims=True)
    lb = jnp.zeros((1, sm.shape[1]), _f32)
    for j in range(1, layer + 1):
        lb = lb + sm[j:j + 1, :]
    one_m_lb = 1.0 - lb

    ti = lax.broadcasted_iota(jnp.int32, (tc, tc), 0)
    si = lax.broadcasted_iota(jnp.int32, (tc, tc), 1)
    same = (ti // HGRN_CHUNK) == (si // HGRN_CHUNK)
    causal = jnp.logical_and(same, si <= ti)
    sum_mat = jnp.concatenate([causal, same], axis=0).astype(_f32).astype(_bf16)
    nchunk = tc // HGRN_CHUNK
    blk = HGRN_ATT_BLOCK
    tb = lax.broadcasted_iota(jnp.int32, (blk, blk), 0)
    sb = lax.broadcasted_iota(jnp.int32, (blk, blk), 1)
    causal_blk = jnp.logical_and((tb // HGRN_CHUNK) == (sb // HGRN_CHUNK), sb <= tb)

    heads = [slice(h * HGRN_DK, (h + 1) * HGRN_DK) for h in range(HGRN_HEADS)]
    def gates(sl):
        qraw = q_ref[:, sl].astype(_f32)
        fz = f_ref[:, sl].astype(_f32)
        t = jnp.exp(-jnp.abs(fz))
        r = 1.0 / (1.0 + t)
        tr = t * r
        pos = fz >= 0.0
        logf = jnp.log(lb[:, sl] + one_m_lb[:, sl] * jnp.where(pos, r, tr))
        k = one_m_lb[:, sl] * jnp.where(pos, tr, r)
        return qraw * _sigmoid(qraw), k, logf

    def decayed(q, k, sums):
        sums = sums[:, 0:HGRN_DK] + sums[:, HGRN_DK:2 * HGRN_DK] + sums[:, 2 * HGRN_DK:]
        eb = jnp.exp(sums[0:tc])
        dec = jnp.exp(sums[tc:])
        kinv = k * (1.0 / eb)
        return (q * eb).astype(_bf16), kinv.astype(_bf16), (kinv * dec).astype(_bf16), dec

    qkl = [gates(sl) for sl in heads]
    sums = [_dot(sum_mat, jnp.concatenate(_split3(logf), axis=1)) for _, _, logf in qkl]
    qkd = [decayed(q, k, s) for (q, k, _), s in zip(qkl, sums)]
    vs = [i_ref[:, sl] for sl in heads]
    att = [[jnp.where(causal_blk, _dot_nt(qe[r0:r0 + blk], ke[r0:r0 + blk]), 0.0).astype(_bf16)
            for r0 in range(0, tc, blk)] for qe, ke, _, _ in qkd]
    intra = [jnp.concatenate([_dot(a, v[i * blk:(i + 1) * blk]) for i, a in enumerate(att_h)], axis=0)
             for att_h, v in zip(att, vs)]
    per_head = [(qe, kd, v, dec, o) for (qe, _, kd, dec), v, o in zip(qkd, vs, intra)]

    states = [st_ref[h] for h in range(HGRN_HEADS)]
    inter = [[] for _ in heads]
    for n in range(nchunk):
        rs = slice(n * HGRN_CHUNK, (n + 1) * HGRN_CHUNK)
        for h, (qe, kd, v, dec, _) in enumerate(per_head):
            inter[h].append(_dot_nt(qe[rs], states[h].astype(_bf16)))
            states[h] = states[h] * dec[n * HGRN_CHUNK:n * HGRN_CHUNK + 1, :] + _dot_tn(v[rs], kd[rs])

    for h, sl in enumerate(heads):
        st_ref[h] = states[h]
        o = per_head[h][4] + jnp.concatenate(inter[h], axis=0)
        y = _rms(o, gain_ref[...])
        graw = g_ref[:, sl].astype(_f32)
        o_ref[:, sl] = (y * (graw * _sigmoid(graw))).astype(o_ref.dtype)


def _hgrn(nat3, lb_logits, gain, *, layer, tc=256):
    bsz, seq, _ = nat3.shape
    w = HGRN_HEADS * HGRN_DK
    spec = lambda col: pl.BlockSpec((None, tc, w), lambda b, t: (b, t, col // w))
    return pl.pallas_call(
        functools.partial(_hgrn_kernel, layer=layer, tc=tc),
        out_shape=jax.ShapeDtypeStruct((bsz, seq, w), _bf16),
        grid=(bsz, seq // tc),
        in_specs=[
            pl.BlockSpec(lb_logits.shape, lambda b, t: (0, 0)),
            pl.BlockSpec((1, HGRN_DK), lambda b, t: (0, 0)),
            spec(NAT_CQ), spec(NAT_CF), spec(NAT_CI), spec(NAT_CG),
        ],
        out_specs=pl.BlockSpec((None, tc, w), lambda b, t: (b, t, 0)),
        scratch_shapes=[pltpu.VMEM((HGRN_HEADS, HGRN_DK, HGRN_DK), _f32)],
        compiler_params=_params("arbitrary", "arbitrary"),
        name="hgrn2",
    )(lb_logits, gain, nat3, nat3, nat3, nat3).reshape(bsz * seq, w)


MERGE_PIECE = 256


def _merge_kernel(oa_ref, ob_ref, oc_ref, g0_ref, g1_ref, g2_ref, h_ref, wa_ref, wb_ref, wc_ref, wo_ref, gp_ref,
                  out_ref):
    d = wo_ref.shape[0]
    pieces = [slice(c, c + MERGE_PIECE) for c in range(0, d, MERGE_PIECE)]

    def branches(cols):
        return (_dot(oa_ref[...], wa_ref[:, cols]), _dot(ob_ref[...], wb_ref[:, cols]),
                _dot(oc_ref[...], wc_ref[:, cols]))

    def gated(cols, ya, yb, yc):
        sig = lambda ref: _sigmoid(ref[:, cols].astype(_f32))
        return (sig(g0_ref) * ya + sig(g1_ref) * yb + sig(g2_ref) * yc).astype(_bf16)

    y = branches(pieces[0])
    mix = []
    for c, cols in enumerate(pieces):
        y_next = branches(pieces[c + 1]) if c + 1 < len(pieces) else None
        mix.append(gated(cols, *y))
        y = y_next
    out = _dot(jnp.concatenate(mix, axis=1), wo_ref[...])
    out_ref[...] = h_ref[...] + _rms(out, gp_ref[...])


def _merge(oa, ob, oc, nat, h, wa, wb, wc, wo, gp, *, tm=512):
    t, d = h.shape
    row = lambda a: pl.BlockSpec((tm, a.shape[1]), lambda i: (i, 0))
    full = lambda a: pl.BlockSpec(a.shape, lambda i: (0, 0))
    gate = lambda j: pl.BlockSpec((tm, d), lambda i: (i, NAT_GATES // d + j))
    return pl.pallas_call(
        _merge_kernel,
        out_shape=jax.ShapeDtypeStruct((t, d), _f32),
        grid=(t // tm,),
        in_specs=[row(oa), row(ob), row(oc), gate(0), gate(1), gate(2), row(h),
                  full(wa), full(wb), full(wc), full(wo), full(gp)],
        out_specs=row(h),
        compiler_params=_params("arbitrary"),
        name="merge_out",
    )(oa, ob, oc, nat, nat, nat, h, wa, wb, wc, wo, gp)


HALO = 8
FFN_PIECE = 256

def _shift_rows(u, carry, s):
    r = pltpu.roll(u, s, axis=0)
    c = pltpu.roll(carry, s, axis=0)
    row = lax.broadcasted_iota(jnp.int32, carry.shape, 0)
    head = jnp.where(row < s, c, r[0:HALO])
    return jnp.concatenate([head, r[HALO:]], axis=0)


def _ffn_kernel(x_ref, gpre_ref, wa_ref, wb_ref, cwa_ref, cwb_ref, cba_ref, cbb_ref, wd_ref, gpost_ref,
                out_ref, hn_ref, acc_ref, ca_ref, cb_ref, *, tiles_per_seq):
    i, j = pl.program_id(0), pl.program_id(1)
    tm = x_ref.shape[0]
    tf = wa_ref.shape[1]

    @pl.when(j == 0)
    def _():
        hn_ref[...] = _rms(x_ref[...], gpre_ref[...]).astype(_bf16)
        acc_ref[...] = jnp.zeros_like(acc_ref)

    @pl.when(i == 0)
    def _():
        ca_ref[j] = jnp.zeros(ca_ref.shape[1:], _f32)
        cb_ref[j] = jnp.zeros(cb_ref.shape[1:], _f32)

    seq_start = (i % tiles_per_seq) == 0

    def up(cols):
        return _dot(hn_ref[...], wa_ref[:, cols]), _dot(hn_ref[...], wb_ref[:, cols])

    def conv(u, cols, cw_ref, cbias_ref, carry_ref):
        carry = jnp.where(seq_start, 0.0, carry_ref[j, :, cols])
        carry_ref[j, :, cols] = u[tm - HALO:tm]
        cw = cw_ref[:, cols]
        return (cw[2:3] * u + cw[1:2] * _shift_rows(u, carry, 1) + cw[0:1] * _shift_rows(u, carry, 2)
                + cbias_ref[:, cols])

    def gate(cols, ua, ub):
        a = conv(ua, cols, cwa_ref, cba_ref, ca_ref)
        b = conv(ub, cols, cwb_ref, cbb_ref, cb_ref)
        gelu = 0.5 * a * (1.0 + jnp.tanh(0.7978845608028654 * (a + 0.044715 * (a * a * a))))
        return (gelu * b).astype(_bf16)

    pieces = [slice(c, c + FFN_PIECE) for c in range(0, tf, FFN_PIECE)]
    u = up(pieces[0])
    ts = []
    for c, cols in enumerate(pieces):
        u_next = up(pieces[c + 1]) if c + 1 < len(pieces) else None
        ts.append(gate(cols, *u))
        u = u_next
    acc_ref[...] += _dot(jnp.concatenate(ts, axis=1), wd_ref[...])

    @pl.when(j == pl.num_programs(1) - 1)
    def _():
        out_ref[...] = x_ref[...] + _rms(acc_ref[...], gpost_ref[...])


def _ffn(h, gpre, w_up, conv_w, conv_b, w_down, gpost, *, seq, tm=512, tf=1024):
    t, d = h.shape
    nf = D_FF // tf
    cb2 = conv_b.reshape(1, 2 * D_FF)
    return pl.pallas_call(
        functools.partial(_ffn_kernel, tiles_per_seq=seq // tm),
        out_shape=jax.ShapeDtypeStruct((t, d), _f32),
        grid=(t // tm, nf),
        in_specs=[
            pl.BlockSpec((tm, d), lambda i, j: (i, 0)),
            pl.BlockSpec((1, d), lambda i, j: (0, 0)),
            pl.BlockSpec((d, tf), lambda i, j: (0, j)),
            pl.BlockSpec((d, tf), lambda i, j: (0, nf + j)),
            pl.BlockSpec((3, tf), lambda i, j: (0, j)),
            pl.BlockSpec((3, tf), lambda i, j: (0, nf + j)),
            pl.BlockSpec((1, tf), lambda i, j: (0, j)),
            pl.BlockSpec((1, tf), lambda i, j: (0, nf + j)),
            pl.BlockSpec((tf, d), lambda i, j: (j, 0)),
            pl.BlockSpec((1, d), lambda i, j: (0, 0)),
        ],
        out_specs=pl.BlockSpec((tm, d), lambda i, j: (i, 0)),
        scratch_shapes=[
            pltpu.VMEM((tm, d), _bf16),
            pltpu.VMEM((tm, d), _f32),
            pltpu.VMEM((nf, HALO, tf), _f32),
            pltpu.VMEM((nf, HALO, tf), _f32),
        ],
        compiler_params=_params("arbitrary", "arbitrary"),
        name="conv_ffn",
    )(h, gpre, w_up, w_up, conv_w, conv_w, cb2, cb2, w_down, gpost)


def kernel(x, norm_pre_mix, norm_post_mix, norm_pre_ffn, norm_post_ffn, w_in, attn_sinks, hgrn_lb_logits,
           hgrn_out_norm, w_branch_a, w_branch_b, w_branch_c, w_out, w_ffn_up, ffn_conv_w, ffn_conv_b,
           w_ffn_down):
    bsz, seq, d = x.shape
    depth = w_in.shape[0]
    h = x.reshape(bsz * seq, d)
    row = lambda a: a.reshape(1, -1)
    for l in range(depth):
        nat, g1, g2 = _in_proj(h, row(norm_pre_mix[l]), _permute_w_in(w_in[l]), bsz=bsz, seq=seq)
        nat3 = nat.reshape(bsz, seq, NAT_W)
        oa = _dilated(nat3, g1, g2)
        ob = _swa(nat3, attn_sinks[l])
        oc = _hgrn(nat3, hgrn_lb_logits, row(hgrn_out_norm[l]), layer=l)
        h = _merge(oa, ob, oc, nat, h, w_branch_a[l].astype(_bf16), w_branch_b[l].astype(_bf16),
                   w_branch_c[l].astype(_bf16), w_out[l].astype(_bf16), row(norm_post_mix[l]))
        h = _ffn(h, row(norm_pre_ffn[l]), w_ffn_up[l].astype(_bf16), ffn_conv_w[l], ffn_conv_b[l],
                 w_ffn_down[l].astype(_bf16), row(norm_post_ffn[l]), seq=seq)
    return h.reshape(bsz, seq, d)
```

```python
import functools

import jax
import jax.numpy as jnp
from jax import lax
from jax.experimental import pallas as pl
from jax.experimental.pallas import tpu as pltpu

D_MODEL = 1024
HEAD_DIM = 64
DIL_PATTERNS = ((128, 1), (512, 4), (2048, 16))
SWA_WINDOW = 128
SWA_KV_HEADS = 2
HGRN_HEADS = 4
HGRN_DK = 128
HGRN_CHUNK = 32
HGRN_ATT_BLOCK = 128
D_FF = 4 * D_MODEL
BAND = 128
GROUP_W = 256
QKV_W = 3 * GROUP_W
EPS = 1e-6
LANES = 128
NEG = -1e30
ATTN_UNROLL = 4

SRC_AQ, SRC_AK, SRC_AV = 0, 768, 1536
SRC_B, SRC_C, SRC_GATES = 2304, 3072, 5120
NAT_GATES = 0
NAT_A0 = 3072
NAT_BQ, NAT_BK, NAT_BV = 3840, 4352, 4480
NAT_CQ, NAT_CF, NAT_CI, NAT_CG = 4608, 5120, 5632, 6144
NAT_W = 6656
D_IN = NAT_W + 2 * QKV_W

VMEM_LIMIT = 56 * 1024 * 1024

_f32 = jnp.float32
_bf16 = jnp.bfloat16


def _dot(a, b):
    return jnp.dot(a, b, preferred_element_type=_f32)


def _dot_nt(a, b):
    return lax.dot_general(a, b, (((1,), (1,)), ((), ())), preferred_element_type=_f32)


def _dot_tn(a, b):
    return lax.dot_general(a, b, (((0,), (0,)), ((), ())), preferred_element_type=_f32)


def _rms(x, g):
    return x * lax.rsqrt(jnp.mean(x * x, axis=-1, keepdims=True) + EPS) * g


def _sigmoid(x):
    return 1.0 / (1.0 + jnp.exp(-x))


def _params(*sem):
    return pltpu.CompilerParams(dimension_semantics=sem, vmem_limit_bytes=VMEM_LIMIT)


def _permute_w_in(w):
    cols = lambda start, width: w[:, start:start + width]
    grp = lambda gi: [cols(SRC_AQ + gi * GROUP_W, GROUP_W), cols(SRC_AK + gi * GROUP_W, GROUP_W),
                      cols(SRC_AV + gi * GROUP_W, GROUP_W)]
    parts = [cols(SRC_GATES, 3 * D_MODEL)] + grp(0) + [cols(SRC_B, SRC_C - SRC_B), cols(SRC_C, SRC_GATES - SRC_C)]
    parts += grp(1) + grp(2)
    return jnp.concatenate(parts, axis=1).astype(_bf16)


def _in_proj_kernel(x_ref, g_ref, w_ref, nat_ref, g1_ref, g2_ref, hn_ref, *, tn):
    tm = x_ref.shape[0]
    hn = _rms(x_ref[...], g_ref[...])
    nlc = hn_ref.shape[0]
    for c in range(nlc):
        hn_ref[c] = hn[:, c * LANES:(c + 1) * LANES]
    hb = hn.astype(_bf16)
    for c in range(NAT_W // tn):
        nat_ref[:, c * tn:(c + 1) * tn] = _dot(hb, w_ref[:, c * tn:(c + 1) * tn]).astype(_bf16)
    for out_ref, col0 in ((g1_ref, NAT_W), (g2_ref, NAT_W + QKV_W)):
        d = out_ref.shape[0]
        rows = tm // d
        lhs = jnp.concatenate(
            [jnp.concatenate([hn_ref[c, pl.ds(r, rows, stride=d), :] for c in range(nlc)], axis=1)
             for r in range(d)], axis=0).astype(_bf16)
        y = _dot(lhs, w_ref[:, col0:col0 + QKV_W]).astype(_bf16)
        for r in range(d):
            out_ref[r] = y[r * rows:(r + 1) * rows]


def _in_proj(x, g, w, *, bsz, seq, tm=512, tn=512):
    t, d = x.shape
    tps = seq // tm
    d1, d2 = DIL_PATTERNS[1][1], DIL_PATTERNS[2][1]
    return pl.pallas_call(
        functools.partial(_in_proj_kernel, tn=tn),
        out_shape=(jax.ShapeDtypeStruct((t, NAT_W), _bf16),
                   jax.ShapeDtypeStruct((bsz, d1, seq // d1, QKV_W), _bf16),
                   jax.ShapeDtypeStruct((bsz, d2, seq // d2, QKV_W), _bf16)),
        grid=(t // tm,),
        in_specs=[
            pl.BlockSpec((tm, d), lambda i: (i, 0)),
            pl.BlockSpec((1, d), lambda i: (0, 0)),
            pl.BlockSpec((d, D_IN), lambda i: (0, 0), pipeline_mode=pl.Buffered(1)),
        ],
        out_specs=(
            pl.BlockSpec((tm, NAT_W), lambda i: (i, 0)),
            pl.BlockSpec((None, d1, tm // d1, QKV_W), lambda i: (i // tps, 0, i % tps, 0)),
            pl.BlockSpec((None, d2, tm // d2, QKV_W), lambda i: (i // tps, 0, i % tps, 0)),
        ),
        scratch_shapes=[pltpu.VMEM((d // LANES, tm, LANES), _f32)],
        compiler_params=_params("arbitrary"),
        name="in_proj",
    )(x, g, w)


def _band_bias(max_dist, nk, first):
    qi = lax.broadcasted_iota(jnp.int32, (BAND, nk), 0)
    ki = lax.broadcasted_iota(jnp.int32, (BAND, nk), 1)
    dist = qi + (nk - BAND) - ki
    ok = (dist >= 0) & (dist <= max_dist)
    if first:
        ok = ok & (ki >= nk - BAND)
    return jnp.where(ok, 0.0, NEG).astype(_f32)


def _band_blocks(blocks, *, sink_of_head, want_lse):
    nh = GROUP_W // HEAD_DIM
    nk = blocks[0][1].shape[0]
    has_prev = nk == 2 * BAND
    lane_head = lax.broadcasted_iota(jnp.int32, (BAND, GROUP_W), 1) // HEAD_DIM
    fold = (lambda x, op: op(x[..., :BAND], x[..., BAND:])) if has_prev else (lambda x, op: x)
    scale = jnp.asarray(HEAD_DIM ** -0.5, _bf16)

    def stack_heads(q):
        q = q * scale
        return jnp.concatenate([jnp.where(lane_head == h, q, jnp.zeros_like(q)) for h in range(nh)], axis=0)

    def pick_heads(x):
        out = jnp.broadcast_to(x[0], (BAND, GROUP_W))
        for h in range(1, nh):
            out = jnp.where(lane_head == h, x[h], out)
        return out

    qs = [stack_heads(q) for q, _, _, _ in blocks]
    s = [_dot_nt(qh, kcat).reshape(nh, BAND, nk) + bias[None] for qh, (_, kcat, _, bias) in zip(qs, blocks)]
    m = [jnp.max(fold(x, jnp.maximum), axis=-1, keepdims=True) for x in s]
    if sink_of_head is not None:
        head = lax.broadcasted_iota(jnp.int32, (nh, BAND, 1), 0)
        sink = jnp.zeros((nh, BAND, 1), _f32)
        for h in range(nh):
            sink = jnp.where(head == h, sink_of_head(h), sink)
        m = [jnp.maximum(x, sink) for x in m]
    p = [jnp.exp(x - mx) for x, mx in zip(s, m)]
    den = [jnp.sum(fold(x, jnp.add), axis=-1, keepdims=True) for x in p]
    if sink_of_head is not None:
        den = [d + jnp.exp(sink - mx) for d, mx in zip(den, m)]
    pv = [_dot(x.reshape(nh * BAND, nk).astype(_bf16), vcat).reshape(nh, BAND, GROUP_W)
          for x, (_, _, vcat, _) in zip(p, blocks)]
    o = [pick_heads(x * (1.0 / d)) for x, d in zip(pv, den)]
    if not want_lse:
        return [(x, None) for x in o]
    return [(x, pick_heads(mx + jnp.log(d))) for x, mx, d in zip(o, m, den)]


def _qkv_cols(ref, rows, lead=()):
    idx = lambda c: lead + (rows, slice(c * GROUP_W, (c + 1) * GROUP_W))
    return ref[idx(0)], ref[idx(1)], ref[idx(2)]


def _dil_kernel(a0_ref, g1_ref, g2_ref, oa_ref, o_s, lse_s, bias_s):
    seq = a0_ref.shape[0]
    blks = functools.partial(_band_blocks, sink_of_head=None, want_lse=True)
    halves = [slice(hf * LANES, (hf + 1) * LANES) for hf in range(GROUP_W // LANES)]
    nu = ATTN_UNROLL
    win0, win1, win2 = (w // d for w, d in DIL_PATTERNS)
    d1, d2 = DIL_PATTERNS[1][1], DIL_PATTERNS[2][1]
    assert win0 == win1 and seq // d2 == BAND and win2 >= BAND - 1
    bias_s[0] = _band_bias(win0, 2 * BAND, True)
    bias_s[1] = _band_bias(win0, 2 * BAND, False)
    bias_s[2, :, 0:BAND] = _band_bias(win2, BAND, True)

    def put(g, rows, o, lse):
        for hf, ls in enumerate(halves):
            o_s[g, hf, rows, :] = o[:, ls]
            lse_s[g, hf, rows, :] = lse[:, ls]

    def band(ref, lead, i):
        cur = pl.ds(pl.multiple_of(i * BAND, BAND), BAND)
        prev = pl.ds(pl.multiple_of(jnp.maximum(i - 1, 0) * BAND, BAND), BAND)
        q, kc, vc = _qkv_cols(ref, cur, lead)
        _, kp, vp = _qkv_cols(ref, prev, lead)
        return (q, jnp.concatenate([kp, kc], axis=0), jnp.concatenate([vp, vc], axis=0),
                bias_s[jnp.minimum(i, 1)])

    def g0_body(n, c):
        ids = [n * nu + u for u in range(nu)]
        for i, (o, lse) in zip(ids, blks([band(a0_ref, (), i) for i in ids])):
            put(0, pl.ds(pl.multiple_of(i * BAND, BAND), BAND), o, lse)
        return c

    lax.fori_loop(0, seq // BAND // nu, g0_body, 0)

    nb1 = seq // d1 // BAND

    def g1_body(n, c):
        ids = [((n * nu + u) // nb1, (n * nu + u) % nb1) for u in range(nu)]
        for (r, i), (o, lse) in zip(ids, blks([band(g1_ref, (r,), i) for r, i in ids])):
            put(1, pl.ds(i * (BAND * d1) + r, BAND, stride=d1), o, lse)
        return c

    lax.fori_loop(0, d1 * nb1 // nu, g1_body, 0)

    def g2_body(n, c):
        ids = [n * nu + u for u in range(nu)]
        outs = blks([_qkv_cols(g2_ref, slice(None), (r,)) + (bias_s[2, :, 0:BAND],) for r in ids])
        for r, (o, lse) in zip(ids, outs):
            put(2, pl.ds(r, BAND, stride=d2), o, lse)
        return c

    lax.fori_loop(0, d2 // nu, g2_body, 0)

    def mix_body(i, c):
        rows = pl.ds(pl.multiple_of(i * BAND, BAND), BAND)
        for hf, ls in enumerate(halves):
            l0, l1, l2 = lse_s[0, hf, rows, :], lse_s[1, hf, rows, :], lse_s[2, hf, rows, :]
            m = jnp.maximum(jnp.maximum(l0, l1), l2)
            e0, e1, e2 = jnp.exp(l0 - m), jnp.exp(l1 - m), jnp.exp(l2 - m)
            oa = (e0 * o_s[0, hf, rows, :] + e1 * o_s[1, hf, rows, :] + e2 * o_s[2, hf, rows, :]) / (e0 + e1 + e2)
            oa_ref[rows, ls] = oa.astype(oa_ref.dtype)
        return c

    lax.fori_loop(0, seq // BAND, mix_body, 0)


def _dilated(nat3, g1, g2):
    bsz, seq, _ = nat3.shape
    n_grp = len(DIL_PATTERNS)
    return pl.pallas_call(
        _dil_kernel,
        out_shape=jax.ShapeDtypeStruct((bsz, seq, GROUP_W), _bf16),
        grid=(bsz,),
        in_specs=[
            pl.BlockSpec((None, seq, QKV_W), lambda b: (b, 0, NAT_A0 // QKV_W)),
            pl.BlockSpec((None,) + g1.shape[1:], lambda b: (b, 0, 0, 0)),
            pl.BlockSpec((None,) + g2.shape[1:], lambda b: (b, 0, 0, 0)),
        ],
        out_specs=pl.BlockSpec((None, seq, GROUP_W), lambda b: (b, 0, 0)),
        scratch_shapes=[pltpu.VMEM((n_grp, GROUP_W // LANES, seq, LANES), _f32)] * 2
        + [pltpu.VMEM((3, BAND, 2 * BAND), _f32)],
        compiler_params=_params("arbitrary"),
        name="dil_attn",
    )(nat3, g1, g2).reshape(bsz * seq, GROUP_W)


def _swa_kernel(sink_ref, q_ref, k_ref, v_ref, o_ref, kx_ref, vx_ref, bias_s):
    seq = q_ref.shape[0]
    nb = seq // BAND
    kvh = pl.program_id(1)
    bias_s[0] = _band_bias(SWA_WINDOW - 1, 2 * BAND, True)
    bias_s[1] = _band_bias(SWA_WINDOW - 1, 2 * BAND, False)
    lane = lax.broadcasted_iota(jnp.int32, (BAND, 2 * HEAD_DIM), 1)
    keep = (lane < HEAD_DIM) == (kvh == 0)

    def expand(i, c):
        r0 = pl.multiple_of(i * BAND, BAND)
        for src, dst in ((k_ref, kx_ref), (v_ref, vx_ref)):
            x = src[pl.ds(r0, BAND), :].astype(_f32)
            x2 = jnp.where(keep, x, pltpu.roll(x, HEAD_DIM, axis=1)).astype(_bf16)
            dst[pl.ds(r0, BAND), :] = jnp.concatenate([x2, x2], axis=1)
        return c

    lax.fori_loop(0, nb, expand, 0)
    heads_per_kv = GROUP_W // HEAD_DIM

    def band(i):
        cur = pl.ds(pl.multiple_of(i * BAND, BAND), BAND)
        prev = pl.ds(pl.multiple_of(jnp.maximum(i - 1, 0) * BAND, BAND), BAND)
        return (q_ref[cur, :], jnp.concatenate([kx_ref[prev, :], kx_ref[cur, :]], axis=0),
                jnp.concatenate([vx_ref[prev, :], vx_ref[cur, :]], axis=0), bias_s[jnp.minimum(i, 1)])

    def body(n, c):
        ids = [n * ATTN_UNROLL + u for u in range(ATTN_UNROLL)]
        outs = _band_blocks([band(i) for i in ids], want_lse=False,
                            sink_of_head=lambda h: sink_ref[kvh * heads_per_kv + h])
        for i, (o, _) in zip(ids, outs):
            o_ref[pl.ds(pl.multiple_of(i * BAND, BAND), BAND), :] = o.astype(o_ref.dtype)
        return c

    lax.fori_loop(0, nb // ATTN_UNROLL, body, 0)


def _swa(nat3, sinks):
    bsz, seq, _ = nat3.shape
    kvw = SWA_KV_HEADS * HEAD_DIM
    return pl.pallas_call(
        _swa_kernel,
        out_shape=jax.ShapeDtypeStruct((bsz, seq, SWA_KV_HEADS * GROUP_W), _bf16),
        grid=(bsz, SWA_KV_HEADS),
        in_specs=[
            pl.BlockSpec(memory_space=pltpu.SMEM),
            pl.BlockSpec((None, seq, GROUP_W), lambda b, g: (b, 0, NAT_BQ // GROUP_W + g)),
            pl.BlockSpec((None, seq, kvw), lambda b, g: (b, 0, NAT_BK // kvw)),
            pl.BlockSpec((None, seq, kvw), lambda b, g: (b, 0, NAT_BV // kvw)),
        ],
        out_specs=pl.BlockSpec((None, seq, GROUP_W), lambda b, g: (b, 0, g)),
        scratch_shapes=[pltpu.VMEM((seq, GROUP_W), _bf16), pltpu.VMEM((seq, GROUP_W), _bf16),
                        pltpu.VMEM((2, BAND, 2 * BAND), _f32)],
        compiler_params=_params("arbitrary", "arbitrary"),
        name="swa_attn",
    )(sinks, nat3, nat3, nat3).reshape(bsz * seq, SWA_KV_HEADS * GROUP_W)


def _split3(x):
    hi = x.astype(_bf16)
    r = x - hi.astype(_f32)
    mid = r.astype(_bf16)
    lo = (r - mid.astype(_f32)).astype(_bf16)
    return hi, mid, lo


def _hgrn_kernel(lbl_ref, gain_ref, q_ref, f_ref, i_ref, g_ref, o_ref, st_ref, *, layer, tc):
    @pl.when(pl.program_id(1) == 0)
    def _():
        st_ref[...] = jnp.zeros_like(st_ref)

    logits = lbl_ref[...]
    e = jnp.exp(logits - jnp.max(logits, axis=0, keepdims=True))
    sm = e / jnp.sum(e, axis=0, keepdims=True)
    lb = jnp.zeros((1, sm.shape[1]), _f32)
    for j in range(1, layer + 1):
        lb = lb + sm[j:j + 1, :]
    one_m_lb = 1.0 - lb

    ti = lax.broadcasted_iota(jnp.int32, (tc, tc), 0)
    si = lax.broadcasted_iota(jnp.int32, (tc, tc), 1)
    same = (ti // HGRN_CHUNK) == (si // HGRN_CHUNK)
    causal = jnp.logical_and(same, si <= ti)
    sum_mat = jnp.concatenate([causal, same], axis=0).astype(_f32).astype(_bf16)
    nchunk = tc // HGRN_CHUNK
    blk = HGRN_ATT_BLOCK
    tb = lax.broadcasted_iota(jnp.int32, (blk, blk), 0)
    sb = lax.broadcasted_iota(jnp.int32, (blk, blk), 1)
    causal_blk = jnp.logical_and((tb // HGRN_CHUNK) == (sb // HGRN_CHUNK), sb <= tb)

    heads = [slice(h * HGRN_DK, (h + 1) * HGRN_DK) for h in range(HGRN_HEADS)]
    def gates(sl):
        qraw = q_ref[:, sl].astype(_f32)
        fz = f_ref[:, sl].astype(_f32)
        t = jnp.exp(-jnp.abs(fz))
        r = 1.0 / (1.0 + t)
        tr = t * r
        pos = fz >= 0.0
        logf = jnp.log(lb[:, sl] + one_m_lb[:, sl] * jnp.where(pos, r, tr))
        k = one_m_lb[:, sl] * jnp.where(pos, tr, r)
        return qraw * _sigmoid(qraw), k, logf

    def decayed(q, k, sums):
        sums = sums[:, 0:HGRN_DK] + sums[:, HGRN_DK:2 * HGRN_DK] + sums[:, 2 * HGRN_DK:]
        eb = jnp.exp(sums[0:tc])
        dec = jnp.exp(sums[tc:])
        kinv = k * (1.0 / eb)
        return (q * eb).astype(_bf16), kinv.astype(_bf16), (kinv * dec).astype(_bf16), dec

    qkl = [gates(sl) for sl in heads]
    sums = [_dot(sum_mat, jnp.concatenate(_split3(logf), axis=1)) for _, _, logf in qkl]
    qkd = [decayed(q, k, s) for (q, k, _), s in zip(qkl, sums)]
    vs = [i_ref[:, sl] for sl in heads]
    att = [[jnp.where(causal_blk, _dot_nt(qe[r0:r0 + blk], ke[r0:r0 + blk]), 0.0).astype(_bf16)
            for r0 in range(0, tc, blk)] for qe, ke, _, _ in qkd]
    intra = [jnp.concatenate([_dot(a, v[i * blk:(i + 1) * blk]) for i, a in enumerate(att_h)], axis=0)
             for att_h, v in zip(att, vs)]
    per_head = [(qe, kd, v, dec, o) for (qe, _, kd, dec), v, o in zip(qkd, vs, intra)]

    states = [st_ref[h] for h in range(HGRN_HEADS)]
    inter = [[] for _ in heads]
    for n in range(nchunk):
        rs = slice(n * HGRN_CHUNK, (n + 1) * HGRN_CHUNK)
        for h, (qe, kd, v, dec, _) in enumerate(per_head):
            inter[h].append(_dot_nt(qe[rs], states[h].astype(_bf16)))
            states[h] = states[h] * dec[n * HGRN_CHUNK:n * HGRN_CHUNK + 1, :] + _dot_tn(v[rs], kd[rs])

    for h, sl in enumerate(heads):
        st_ref[h] = states[h]
        o = per_head[h][4] + jnp.concatenate(inter[h], axis=0)
        y = _rms(o, gain_ref[...])
        graw = g_ref[:, sl].astype(_f32)
        o_ref[:, sl] = (y * (graw * _sigmoid(graw))).astype(o_ref.dtype)


def _hgrn(nat3, lb_logits, gain, *, layer, tc=256):
    bsz, seq, _ = nat3.shape
    w = HGRN_HEADS * HGRN_DK
    spec = lambda col: pl.BlockSpec((None, tc, w), lambda b, t: (b, t, col // w))
    return pl.pallas_call(
        functools.partial(_hgrn_kernel, layer=layer, tc=tc),
        out_shape=jax.ShapeDtypeStruct((bsz, seq, w), _bf16),
        grid=(bsz, seq // tc),
        in_specs=[
            pl.BlockSpec(lb_logits.shape, lambda b, t: (0, 0)),
            pl.BlockSpec((1, HGRN_DK), lambda b, t: (0, 0)),
            spec(NAT_CQ), spec(NAT_CF), spec(NAT_CI), spec(NAT_CG),
        ],
        out_specs=pl.BlockSpec((None, tc, w), lambda b, t: (b, t, 0)),
        scratch_shapes=[pltpu.VMEM((HGRN_HEADS, HGRN_DK, HGRN_DK), _f32)],
        compiler_params=_params("arbitrary", "arbitrary"),
        name="hgrn2",
    )(lb_logits, gain, nat3, nat3, nat3, nat3).reshape(bsz * seq, w)


MERGE_PIECE = 256


def _merge_kernel(oa_ref, ob_ref, oc_ref, g0_ref, g1_ref, g2_ref, h_ref, wa_ref, wb_ref, wc_ref, wo_ref, gp_ref,
                  out_ref):
    d = wo_ref.shape[0]
    pieces = [slice(c, c + MERGE_PIECE) for c in range(0, d, MERGE_PIECE)]

    def branches(cols):
        return (_dot(oa_ref[...], wa_ref[:, cols]), _dot(ob_ref[...], wb_ref[:, cols]),
                _dot(oc_ref[...], wc_ref[:, cols]))

    def gated(cols, ya, yb, yc):
        sig = lambda ref: _sigmoid(ref[:, cols].astype(_f32))
        return (sig(g0_ref) * ya + sig(g1_ref) * yb + sig(g2_ref) * yc).astype(_bf16)

    y = branches(pieces[0])
    mix = []
    for c, cols in enumerate(pieces):
        y_next = branches(pieces[c + 1]) if c + 1 < len(pieces) else None
        mix.append(gated(cols, *y))
        y = y_next
    out = _dot(jnp.concatenate(mix, axis=1), wo_ref[...])
    out_ref[...] = h_ref[...] + _rms(out, gp_ref[...])


def _merge(oa, ob, oc, nat, h, wa, wb, wc, wo, gp, *, tm=1024):
    t, d = h.shape
    row = lambda a: pl.BlockSpec((tm, a.shape[1]), lambda i: (i, 0))
    full = lambda a: pl.BlockSpec(a.shape, lambda i: (0, 0))
    gate = lambda j: pl.BlockSpec((tm, d), lambda i: (i, NAT_GATES // d + j))
    return pl.pallas_call(
        _merge_kernel,
        out_shape=jax.ShapeDtypeStruct((t, d), _f32),
        grid=(t // tm,),
        in_specs=[row(oa), row(ob), row(oc), gate(0), gate(1), gate(2), row(h),
                  full(wa), full(wb), full(wc), full(wo), full(gp)],
        out_specs=row(h),
        compiler_params=_params("arbitrary"),
        name="merge_out",
    )(oa, ob, oc, nat, nat, nat, h, wa, wb, wc, wo, gp)


HALO = 8
FFN_PIECE = 256

def _shift_rows(u, carry, s):
    r = pltpu.roll(u, s, axis=0)
    c = pltpu.roll(carry, s, axis=0)
    row = lax.broadcasted_iota(jnp.int32, carry.shape, 0)
    head = jnp.where(row < s, c, r[0:HALO])
    return jnp.concatenate([head, r[HALO:]], axis=0)


def _ffn_kernel(x_ref, gpre_ref, wa_ref, wb_ref, cwa_ref, cwb_ref, cba_ref, cbb_ref, wd_ref, gpost_ref,
                out_ref, hn_ref, acc_ref, ca_ref, cb_ref, *, tiles_per_seq):
    i, j = pl.program_id(0), pl.program_id(1)
    tm = x_ref.shape[0]
    tf = wa_ref.shape[1]

    @pl.when(j == 0)
    def _():
        hn_ref[...] = _rms(x_ref[...], gpre_ref[...]).astype(_bf16)
        acc_ref[...] = jnp.zeros_like(acc_ref)

    @pl.when(i == 0)
    def _():
        ca_ref[j] = jnp.zeros(ca_ref.shape[1:], _f32)
        cb_ref[j] = jnp.zeros(cb_ref.shape[1:], _f32)

    seq_start = (i % tiles_per_seq) == 0

    def up(cols):
        return _dot(hn_ref[...], wa_ref[:, cols]), _dot(hn_ref[...], wb_ref[:, cols])

    def conv(u, cols, cw_ref, cbias_ref, carry_ref):
        carry = jnp.where(seq_start, 0.0, carry_ref[j, :, cols])
        carry_ref[j, :, cols] = u[tm - HALO:tm]
        cw = cw_ref[:, cols]
        return (cw[2:3] * u + cw[1:2] * _shift_rows(u, carry, 1) + cw[0:1] * _shift_rows(u, carry, 2)
                + cbias_ref[:, cols])

    def gate(cols, ua, ub):
        a = conv(ua, cols, cwa_ref, cba_ref, ca_ref)
        b = conv(ub, cols, cwb_ref, cbb_ref, cb_ref)
        gelu = 0.5 * a * (1.0 + jnp.tanh(0.7978845608028654 * (a + 0.044715 * (a * a * a))))
        return (gelu * b).astype(_bf16)

    pieces = [slice(c, c + FFN_PIECE) for c in range(0, tf, FFN_PIECE)]
    u = up(pieces[0])
    ts = []
    for c, cols in enumerate(pieces):
        u_next = up(pieces[c + 1]) if c + 1 < len(pieces) else None
        ts.append(gate(cols, *u))
        u = u_next
    acc_ref[...] += _dot(jnp.concatenate(ts, axis=1), wd_ref[...])

    @pl.when(j == pl.num_programs(1) - 1)
    def _():
        out_ref[...] = x_ref[...] + _rms(acc_ref[...], gpost_ref[...])


def _ffn(h, gpre, w_up, conv_w, conv_b, w_down, gpost, *, seq, tm=1024, tf=1024):
    t, d = h.shape
    nf = D_FF // tf
    cb2 = conv_b.reshape(1, 2 * D_FF)
    return pl.pallas_call(
        functools.partial(_ffn_kernel, tiles_per_seq=seq // tm),
        out_shape=jax.ShapeDtypeStruct((t, d), _f32),
        grid=(t // tm, nf),
        in_specs=[
            pl.BlockSpec((tm, d), lambda i, j: (i, 0)),
            pl.BlockSpec((1, d), lambda i, j: (0, 0)),
            pl.BlockSpec((d, tf), lambda i, j: (0, j)),
            pl.BlockSpec((d, tf), lambda i, j: (0, nf + j)),
            pl.BlockSpec((3, tf), lambda i, j: (0, j)),
            pl.BlockSpec((3, tf), lambda i, j: (0, nf + j)),
            pl.BlockSpec((1, tf), lambda i, j: (0, j)),
            pl.BlockSpec((1, tf), lambda i, j: (0, nf + j)),
            pl.BlockSpec((tf, d), lambda i, j: (j, 0)),
            pl.BlockSpec((1, d), lambda i, j: (0, 0)),
        ],
        out_specs=pl.BlockSpec((tm, d), lambda i, j: (i, 0)),
        scratch_shapes=[
            pltpu.VMEM((tm, d), _bf16),
            pltpu.VMEM((tm, d), _f32),
            pltpu.VMEM((nf, HALO, tf), _f32),
            pltpu.VMEM((nf, HALO, tf), _f32),
        ],
        compiler_params=_params("arbitrary", "arbitrary"),
        name="conv_ffn",
    )(h, gpre, w_up, w_up, conv_w, conv_w, cb2, cb2, w_down, gpost)


def kernel(x, norm_pre_mix, norm_post_mix, norm_pre_ffn, norm_post_ffn, w_in, attn_sinks, hgrn_lb_logits,
           hgrn_out_norm, w_branch_a, w_branch_b, w_branch_c, w_out, w_ffn_up, ffn_conv_w, ffn_conv_b,
           w_ffn_down):
    bsz, seq, d = x.shape
    depth = w_in.shape[0]
    h = x.reshape(bsz * seq, d)
    row = lambda a: a.reshape(1, -1)
    for l in range(depth):
        nat, g1, g2 = _in_proj(h, row(norm_pre_mix[l]), _permute_w_in(w_in[l]), bsz=bsz, seq=seq)
        nat3 = nat.reshape(bsz, seq, NAT_W)
        oa = _dilated(nat3, g1, g2)
        ob = _swa(nat3, attn_sinks[l])
        oc = _hgrn(nat3, hgrn_lb_logits, row(hgrn_out_norm[l]), layer=l)
        h = _merge(oa, ob, oc, nat, h, w_branch_a[l].astype(_bf16), w_branch_b[l].astype(_bf16),
                   w_branch_c[l].astype(_bf16), w_out[l].astype(_bf16), row(norm_post_mix[l]))
        h = _ffn(h, row(norm_pre_ffn[l]), w_ffn_up[l].astype(_bf16), ffn_conv_w[l], ffn_conv_b[l],
                 w_ffn_down[l].astype(_bf16), row(norm_post_ffn[l]), seq=seq)
    return h.reshape(bsz, seq, d)
```

```python
import functools

import jax
import jax.numpy as jnp
from jax import lax
from jax.experimental import pallas as pl
from jax.experimental.pallas import tpu as pltpu

D_MODEL = 1024
HEAD_DIM = 64
DIL_PATTERNS = ((128, 1), (512, 4), (2048, 16))
SWA_WINDOW = 128
SWA_KV_HEADS = 2
HGRN_HEADS = 4
HGRN_DK = 128
HGRN_CHUNK = 32
HGRN_ATT_BLOCK = 128
D_FF = 4 * D_MODEL
BAND = 128
GROUP_W = 256
QKV_W = 3 * GROUP_W
EPS = 1e-6
LANES = 128
NEG = -1e30
ATTN_UNROLL = 4

SRC_AQ, SRC_AK, SRC_AV = 0, 768, 1536
SRC_B, SRC_C, SRC_GATES = 2304, 3072, 5120
NAT_GATES = 0
NAT_A0 = 3072
NAT_BQ, NAT_BK, NAT_BV = 3840, 4352, 4480
NAT_W = 4608
HGRN_W = HGRN_HEADS * HGRN_DK
COL_C = NAT_W
COL_G1 = COL_C + 4 * HGRN_W
COL_G2 = COL_G1 + QKV_W
D_IN = COL_G2 + QKV_W

VMEM_LIMIT = 56 * 1024 * 1024

_f32 = jnp.float32
_bf16 = jnp.bfloat16


def _dot(a, b):
    return jnp.dot(a, b, preferred_element_type=_f32)


def _dot_nt(a, b):
    return lax.dot_general(a, b, (((1,), (1,)), ((), ())), preferred_element_type=_f32)


def _dot_tn(a, b):
    return lax.dot_general(a, b, (((0,), (0,)), ((), ())), preferred_element_type=_f32)


def _rms(x, g):
    return x * lax.rsqrt(jnp.mean(x * x, axis=-1, keepdims=True) + EPS) * g


def _sigmoid(x):
    return 1.0 / (1.0 + jnp.exp(-x))


def _params(*sem):
    return pltpu.CompilerParams(dimension_semantics=sem, vmem_limit_bytes=VMEM_LIMIT)


def _permute_w_in(w):
    cols = lambda start, width: w[:, start:start + width]
    grp = lambda gi: [cols(SRC_AQ + gi * GROUP_W, GROUP_W), cols(SRC_AK + gi * GROUP_W, GROUP_W),
                      cols(SRC_AV + gi * GROUP_W, GROUP_W)]
    parts = [cols(SRC_GATES, 3 * D_MODEL)] + grp(0) + [cols(SRC_B, SRC_C - SRC_B), cols(SRC_C, SRC_GATES - SRC_C)]
    parts += grp(1) + grp(2)
    return jnp.concatenate(parts, axis=1).astype(_bf16)


def _in_proj_kernel(x_ref, g_ref, w_ref, lbl_ref, gain_ref, nat_ref, oc_ref, g1_ref, g2_ref,
                    hn_ref, c_ref, st_ref, *, tn, layer, tiles_per_seq):
    tm = x_ref.shape[0]

    @pl.when(pl.program_id(0) % tiles_per_seq == 0)
    def _():
        st_ref[...] = jnp.zeros_like(st_ref)

    hn = _rms(x_ref[...], g_ref[...])
    nlc = hn_ref.shape[0]
    for c in range(nlc):
        hn_ref[c] = hn[:, c * LANES:(c + 1) * LANES]
    hb = hn.astype(_bf16)
    def hgrn_chunk(c):
        c_ref[:, c:c + tn] = _dot(hb, w_ref[:, COL_C + c:COL_C + c + tn]).astype(_bf16)

    def nat_chunk(c):
        nat_ref[:, c:c + tn] = _dot(hb, w_ref[:, c:c + tn]).astype(_bf16)

    def dilated(out_ref, col0):
        d = out_ref.shape[0]
        rows = tm // d
        lhs = jnp.concatenate(
            [jnp.concatenate([hn_ref[c, pl.ds(r, rows, stride=d), :] for c in range(nlc)], axis=1)
             for r in range(d)], axis=0).astype(_bf16)
        y = _dot(lhs, w_ref[:, col0:col0 + QKV_W]).astype(_bf16)
        for r in range(d):
            out_ref[r] = y[r * rows:(r + 1) * rows]

    for c in range(0, 2 * HGRN_W, tn):
        hgrn_chunk(c)
    matmuls = [functools.partial(hgrn_chunk, c) for c in range(2 * HGRN_W, 4 * HGRN_W, tn)]
    matmuls += [functools.partial(nat_chunk, c) for c in range(0, NAT_W, tn)]
    matmuls += [functools.partial(dilated, g1_ref, COL_G1), functools.partial(dilated, g2_ref, COL_G2)]
    hgrn = _hgrn_stages(c_ref, lbl_ref, gain_ref, st_ref, oc_ref, layer=layer)
    per_matmul = -(-_hgrn_stage_count(tm) // len(matmuls))
    for mm in matmuls:
        mm()
        for _ in range(per_matmul):
            next(hgrn, None)
    for _ in hgrn:
        pass


def _in_proj(x, g, w, lb_logits, gain, *, layer, bsz, seq, tm=512, tn=512):
    t, d = x.shape
    tps = seq // tm
    d1, d2 = DIL_PATTERNS[1][1], DIL_PATTERNS[2][1]
    return pl.pallas_call(
        functools.partial(_in_proj_kernel, tn=tn, layer=layer, tiles_per_seq=tps),
        out_shape=(jax.ShapeDtypeStruct((t, NAT_W), _bf16),
                   jax.ShapeDtypeStruct((t, HGRN_W), _bf16),
                   jax.ShapeDtypeStruct((bsz, d1, seq // d1, QKV_W), _bf16),
                   jax.ShapeDtypeStruct((bsz, d2, seq // d2, QKV_W), _bf16)),
        grid=(t // tm,),
        in_specs=[
            pl.BlockSpec((tm, d), lambda i: (i, 0)),
            pl.BlockSpec((1, d), lambda i: (0, 0)),
            pl.BlockSpec((d, D_IN), lambda i: (0, 0), pipeline_mode=pl.Buffered(1)),
            pl.BlockSpec(lb_logits.shape, lambda i: (0, 0)),
            pl.BlockSpec((1, HGRN_DK), lambda i: (0, 0)),
        ],
        out_specs=(
            pl.BlockSpec((tm, NAT_W), lambda i: (i, 0)),
            pl.BlockSpec((tm, HGRN_W), lambda i: (i, 0)),
            pl.BlockSpec((None, d1, tm // d1, QKV_W), lambda i: (i // tps, 0, i % tps, 0)),
            pl.BlockSpec((None, d2, tm // d2, QKV_W), lambda i: (i // tps, 0, i % tps, 0)),
        ),
        scratch_shapes=[pltpu.VMEM((d // LANES, tm, LANES), _f32),
                        pltpu.VMEM((tm, 4 * HGRN_W), _bf16),
                        pltpu.VMEM((HGRN_HEADS, HGRN_DK, HGRN_DK), _f32)],
        compiler_params=_params("arbitrary"),
        name="in_proj",
    )(x, g, w, lb_logits, gain)


def _band_bias(max_dist, nk, first):
    qi = lax.broadcasted_iota(jnp.int32, (BAND, nk), 0)
    ki = lax.broadcasted_iota(jnp.int32, (BAND, nk), 1)
    dist = qi + (nk - BAND) - ki
    ok = (dist >= 0) & (dist <= max_dist)
    if first:
        ok = ok & (ki >= nk - BAND)
    return jnp.where(ok, 0.0, NEG).astype(_f32)


def _band_blocks(blocks, *, sink_of_head, want_lse):
    nh = GROUP_W // HEAD_DIM
    nk = blocks[0][1].shape[0]
    has_prev = nk == 2 * BAND
    lane_head = lax.broadcasted_iota(jnp.int32, (BAND, GROUP_W), 1) // HEAD_DIM
    fold = (lambda x, op: op(x[..., :BAND], x[..., BAND:])) if has_prev else (lambda x, op: x)
    scale = jnp.asarray(HEAD_DIM ** -0.5, _bf16)

    def stack_heads(q):
        q = q * scale
        return jnp.concatenate([jnp.where(lane_head == h, q, jnp.zeros_like(q)) for h in range(nh)], axis=0)

    def pick_heads(x):
        out = jnp.broadcast_to(x[0], (BAND, GROUP_W))
        for h in range(1, nh):
            out = jnp.where(lane_head == h, x[h], out)
        return out

    if sink_of_head is not None:
        head = lax.broadcasted_iota(jnp.int32, (nh, BAND, 1), 0)
        sink = jnp.zeros((nh, BAND, 1), _f32)
        for h in range(nh):
            sink = jnp.where(head == h, sink_of_head(h), sink)

    def scores(qh, kcat, bias):
        x = _dot_nt(qh, kcat).reshape(nh, BAND, nk) + bias[None]
        mx = jnp.max(fold(x, jnp.maximum), axis=-1, keepdims=True)
        return x, (mx if sink_of_head is None else jnp.maximum(mx, sink))

    def probs(x, mx):
        p = jnp.exp(x - mx)
        den = jnp.sum(fold(p, jnp.add), axis=-1, keepdims=True)
        if sink_of_head is not None:
            den = den + jnp.exp(sink - mx)
        return p.reshape(nh * BAND, nk).astype(_bf16), den

    qs = [stack_heads(q) for q, _, _, _ in blocks]
    sm = [scores(qh, kcat, bias) for qh, (_, kcat, _, bias) in zip(qs, blocks)]
    pd = [probs(x, mx) for x, mx in sm]
    pv = [_dot(p, vcat).reshape(nh, BAND, GROUP_W) for (p, _), (_, _, vcat, _) in zip(pd, blocks)]
    o = [pick_heads(x * (1.0 / d)) for x, (_, d) in zip(pv, pd)]
    if not want_lse:
        return [(x, None) for x in o]
    return [(x, pick_heads(mx + jnp.log(d))) for x, (_, mx), (_, d) in zip(o, sm, pd)]


def _qkv_cols(ref, rows, lead=()):
    idx = lambda c: lead + (rows, slice(c * GROUP_W, (c + 1) * GROUP_W))
    return ref[idx(0)], ref[idx(1)], ref[idx(2)]


def _dil_kernel(a0_ref, g1_ref, g2_ref, oa_ref, o_s, lse_s, bias_s):
    seq = a0_ref.shape[0]
    blks = functools.partial(_band_blocks, sink_of_head=None, want_lse=True)
    halves = [slice(hf * LANES, (hf + 1) * LANES) for hf in range(GROUP_W // LANES)]
    nu = ATTN_UNROLL
    win0, win1, win2 = (w // d for w, d in DIL_PATTERNS)
    d1, d2 = DIL_PATTERNS[1][1], DIL_PATTERNS[2][1]
    assert win0 == win1 and seq // d2 == BAND and win2 >= BAND - 1
    bias_s[0] = _band_bias(win0, 2 * BAND, True)
    bias_s[1] = _band_bias(win0, 2 * BAND, False)
    bias_s[2, :, 0:BAND] = _band_bias(win2, BAND, True)

    def put(g, rows, o, lse):
        for hf, ls in enumerate(halves):
            o_s[g, hf, rows, :] = o[:, ls]
            lse_s[g, hf, rows, :] = lse[:, ls]

    def band(ref, lead, i):
        cur = pl.ds(pl.multiple_of(i * BAND, BAND), BAND)
        prev = pl.ds(pl.multiple_of(jnp.maximum(i - 1, 0) * BAND, BAND), BAND)
        q, kc, vc = _qkv_cols(ref, cur, lead)
        _, kp, vp = _qkv_cols(ref, prev, lead)
        return (q, jnp.concatenate([kp, kc], axis=0), jnp.concatenate([vp, vc], axis=0),
                bias_s[jnp.minimum(i, 1)])

    def g0_body(n, c):
        ids = [n * nu + u for u in range(nu)]
        for i, (o, lse) in zip(ids, blks([band(a0_ref, (), i) for i in ids])):
            put(0, pl.ds(pl.multiple_of(i * BAND, BAND), BAND), o, lse)
        return c

    lax.fori_loop(0, seq // BAND // nu, g0_body, 0)

    nb1 = seq // d1 // BAND

    def g1_body(n, c):
        ids = [((n * nu + u) // nb1, (n * nu + u) % nb1) for u in range(nu)]
        for (r, i), (o, lse) in zip(ids, blks([band(g1_ref, (r,), i) for r, i in ids])):
            put(1, pl.ds(i * (BAND * d1) + r, BAND, stride=d1), o, lse)
        return c

    lax.fori_loop(0, d1 * nb1 // nu, g1_body, 0)

    def g2_body(n, c):
        ids = [n * nu + u for u in range(nu)]
        outs = blks([_qkv_cols(g2_ref, slice(None), (r,)) + (bias_s[2, :, 0:BAND],) for r in ids])
        for r, (o, lse) in zip(ids, outs):
            put(2, pl.ds(r, BAND, stride=d2), o, lse)
        return c

    lax.fori_loop(0, d2 // nu, g2_body, 0)

    def mix_body(i, c):
        rows = pl.ds(pl.multiple_of(i * BAND, BAND), BAND)
        for hf, ls in enumerate(halves):
            l0, l1, l2 = lse_s[0, hf, rows, :], lse_s[1, hf, rows, :], lse_s[2, hf, rows, :]
            m = jnp.maximum(jnp.maximum(l0, l1), l2)
            e0, e1, e2 = jnp.exp(l0 - m), jnp.exp(l1 - m), jnp.exp(l2 - m)
            oa = (e0 * o_s[0, hf, rows, :] + e1 * o_s[1, hf, rows, :] + e2 * o_s[2, hf, rows, :]) / (e0 + e1 + e2)
            oa_ref[rows, ls] = oa.astype(oa_ref.dtype)
        return c

    lax.fori_loop(0, seq // BAND, mix_body, 0)


def _dilated(nat3, g1, g2):
    bsz, seq, _ = nat3.shape
    n_grp = len(DIL_PATTERNS)
    return pl.pallas_call(
        _dil_kernel,
        out_shape=jax.ShapeDtypeStruct((bsz, seq, GROUP_W), _bf16),
        grid=(bsz,),
        in_specs=[
            pl.BlockSpec((None, seq, QKV_W), lambda b: (b, 0, NAT_A0 // QKV_W)),
            pl.BlockSpec((None,) + g1.shape[1:], lambda b: (b, 0, 0, 0)),
            pl.BlockSpec((None,) + g2.shape[1:], lambda b: (b, 0, 0, 0)),
        ],
        out_specs=pl.BlockSpec((None, seq, GROUP_W), lambda b: (b, 0, 0)),
        scratch_shapes=[pltpu.VMEM((n_grp, GROUP_W // LANES, seq, LANES), _f32)] * 2
        + [pltpu.VMEM((3, BAND, 2 * BAND), _f32)],
        compiler_params=_params("arbitrary"),
        name="dil_attn",
    )(nat3, g1, g2).reshape(bsz * seq, GROUP_W)


def _swa_kernel(sink_ref, q_ref, k_ref, v_ref, o_ref, kx_ref, vx_ref, bias_s):
    seq = q_ref.shape[0]
    nb = seq // BAND
    kvh = pl.program_id(1)
    bias_s[0] = _band_bias(SWA_WINDOW - 1, 2 * BAND, True)
    bias_s[1] = _band_bias(SWA_WINDOW - 1, 2 * BAND, False)
    lane = lax.broadcasted_iota(jnp.int32, (BAND, 2 * HEAD_DIM), 1)
    keep = (lane < HEAD_DIM) == (kvh == 0)

    def expand(i, c):
        r0 = pl.multiple_of(i * BAND, BAND)
        for src, dst in ((k_ref, kx_ref), (v_ref, vx_ref)):
            x = src[pl.ds(r0, BAND), :].astype(_f32)
            x2 = jnp.where(keep, x, pltpu.roll(x, HEAD_DIM, axis=1)).astype(_bf16)
            dst[pl.ds(r0, BAND), :] = jnp.concatenate([x2, x2], axis=1)
        return c

    lax.fori_loop(0, nb, expand, 0)
    heads_per_kv = GROUP_W // HEAD_DIM

    def band(i):
        cur = pl.ds(pl.multiple_of(i * BAND, BAND), BAND)
        prev = pl.ds(pl.multiple_of(jnp.maximum(i - 1, 0) * BAND, BAND), BAND)
        return (q_ref[cur, :], jnp.concatenate([kx_ref[prev, :], kx_ref[cur, :]], axis=0),
                jnp.concatenate([vx_ref[prev, :], vx_ref[cur, :]], axis=0), bias_s[jnp.minimum(i, 1)])

    def body(n, c):
        ids = [n * ATTN_UNROLL + u for u in range(ATTN_UNROLL)]
        outs = _band_blocks([band(i) for i in ids], want_lse=False,
                            sink_of_head=lambda h: sink_ref[kvh * heads_per_kv + h])
        for i, (o, _) in zip(ids, outs):
            o_ref[pl.ds(pl.multiple_of(i * BAND, BAND), BAND), :] = o.astype(o_ref.dtype)
        return c

    lax.fori_loop(0, nb // ATTN_UNROLL, body, 0)


def _swa(nat3, sinks):
    bsz, seq, _ = nat3.shape
    kvw = SWA_KV_HEADS * HEAD_DIM
    return pl.pallas_call(
        _swa_kernel,
        out_shape=jax.ShapeDtypeStruct((bsz, seq, SWA_KV_HEADS * GROUP_W), _bf16),
        grid=(bsz, SWA_KV_HEADS),
        in_specs=[
            pl.BlockSpec(memory_space=pltpu.SMEM),
            pl.BlockSpec((None, seq, GROUP_W), lambda b, g: (b, 0, NAT_BQ // GROUP_W + g)),
            pl.BlockSpec((None, seq, kvw), lambda b, g: (b, 0, NAT_BK // kvw)),
            pl.BlockSpec((None, seq, kvw), lambda b, g: (b, 0, NAT_BV // kvw)),
        ],
        out_specs=pl.BlockSpec((None, seq, GROUP_W), lambda b, g: (b, 0, g)),
        scratch_shapes=[pltpu.VMEM((seq, GROUP_W), _bf16), pltpu.VMEM((seq, GROUP_W), _bf16),
                        pltpu.VMEM((2, BAND, 2 * BAND), _f32)],
        compiler_params=_params("arbitrary", "arbitrary"),
        name="swa_attn",
    )(sinks, nat3, nat3, nat3).reshape(bsz * seq, SWA_KV_HEADS * GROUP_W)


def _split3(x):
    hi = x.astype(_bf16)
    r = x - hi.astype(_f32)
    mid = r.astype(_bf16)
    lo = (r - mid.astype(_f32)).astype(_bf16)
    return hi, mid, lo


HGRN_TILE = 256


def _hgrn_stage_count(tm):
    return 5 + (HGRN_TILE // HGRN_CHUNK + 1) * (tm // HGRN_TILE) + 1


def _hgrn_stages(c_ref, lbl_ref, gain_ref, st_ref, o_ref, *, layer):
    tm = c_ref.shape[0]
    tc = HGRN_TILE
    col = lambda k, sl: slice(k * HGRN_W + sl.start, k * HGRN_W + sl.stop)

    logits = lbl_ref[...]
    e = jnp.exp(logits - jnp.max(logits, axis=0, keepdims=True))
    sm = e / jnp.sum(e, axis=0, keepdims=True)
    lb = jnp.zeros((1, sm.shape[1]), _f32)
    for j in range(1, layer + 1):
        lb = lb + sm[j:j + 1, :]
    one_m_lb = 1.0 - lb

    ti = lax.broadcasted_iota(jnp.int32, (tc, tc), 0)
    si = lax.broadcasted_iota(jnp.int32, (tc, tc), 1)
    same = (ti // HGRN_CHUNK) == (si // HGRN_CHUNK)
    causal = jnp.logical_and(same, si <= ti)
    sum_mat = jnp.concatenate([causal, same], axis=0).astype(_f32).astype(_bf16)
    nchunk = tc // HGRN_CHUNK
    blk = HGRN_ATT_BLOCK
    tb = lax.broadcasted_iota(jnp.int32, (blk, blk), 0)
    sb = lax.broadcasted_iota(jnp.int32, (blk, blk), 1)
    causal_blk = jnp.logical_and((tb // HGRN_CHUNK) == (sb // HGRN_CHUNK), sb <= tb)

    heads = [slice(h * HGRN_DK, (h + 1) * HGRN_DK) for h in range(HGRN_HEADS)]
    subs = [slice(r0, r0 + tc) for r0 in range(0, tm, tc)]
    items = [(rows, sl) for rows in subs for sl in heads]
    def gates(rows, sl):
        qraw = c_ref[rows, col(0, sl)].astype(_f32)
        fz = c_ref[rows, col(1, sl)].astype(_f32)
        t = jnp.exp(-jnp.abs(fz))
        r = 1.0 / (1.0 + t)
        tr = t * r
        pos = fz >= 0.0
        logf = jnp.log(lb[:, sl] + one_m_lb[:, sl] * jnp.where(pos, r, tr))
        k = one_m_lb[:, sl] * jnp.where(pos, tr, r)
        return qraw * _sigmoid(qraw), k, logf

    def decayed(q, k, sums):
        sums = sums[:, 0:HGRN_DK] + sums[:, HGRN_DK:2 * HGRN_DK] + sums[:, 2 * HGRN_DK:]
        eb = jnp.exp(sums[0:tc])
        dec = jnp.exp(sums[tc:])
        kinv = k * (1.0 / eb)
        return (q * eb).astype(_bf16), kinv.astype(_bf16), (kinv * dec).astype(_bf16), dec

    qkl = [gates(rows, sl) for rows, sl in items]
    yield
    sums = [_dot(sum_mat, jnp.concatenate(_split3(logf), axis=1)) for _, _, logf in qkl]
    yield
    qkd = [decayed(q, k, s) for (q, k, _), s in zip(qkl, sums)]
    vs = [c_ref[rows, col(2, sl)] for rows, sl in items]
    yield
    att = [[jnp.where(causal_blk, _dot_nt(qe[r0:r0 + blk], ke[r0:r0 + blk]), 0.0).astype(_bf16)
            for r0 in range(0, tc, blk)] for qe, ke, _, _ in qkd]
    yield
    intra = [jnp.concatenate([_dot(a, v[i * blk:(i + 1) * blk]) for i, a in enumerate(att_h)], axis=0)
             for att_h, v in zip(att, vs)]
    yield

    states = [st_ref[h] for h in range(HGRN_HEADS)]
    for s, rows in enumerate(subs):
        per_head = [(qkd[i][0], qkd[i][2], vs[i], qkd[i][3], intra[i])
                    for i in range(s * HGRN_HEADS, (s + 1) * HGRN_HEADS)]
        inter = [[] for _ in heads]
        for n in range(nchunk):
            rs = slice(n * HGRN_CHUNK, (n + 1) * HGRN_CHUNK)
            for h, (qe, kd, v, dec, _) in enumerate(per_head):
                inter[h].append(_dot_nt(qe[rs], states[h].astype(_bf16)))
                states[h] = states[h] * dec[n * HGRN_CHUNK:n * HGRN_CHUNK + 1, :] + _dot_tn(v[rs], kd[rs])
            yield
        for h, sl in enumerate(heads):
            o = per_head[h][4] + jnp.concatenate(inter[h], axis=0)
            y = _rms(o, gain_ref[...])
            graw = c_ref[rows, col(3, sl)].astype(_f32)
            o_ref[rows, sl] = (y * (graw * _sigmoid(graw))).astype(o_ref.dtype)
        yield
    for h in range(HGRN_HEADS):
        st_ref[h] = states[h]
    yield


MERGE_PIECE = 256


def _merge_kernel(oa_ref, ob_ref, oc_ref, g0_ref, g1_ref, g2_ref, h_ref, wa_ref, wb_ref, wc_ref, wo_ref, gp_ref,
                  out_ref):
    d = wo_ref.shape[0]
    pieces = [slice(c, c + MERGE_PIECE) for c in range(0, d, MERGE_PIECE)]

    def branches(cols):
        return (_dot(oa_ref[...], wa_ref[:, cols]), _dot(ob_ref[...], wb_ref[:, cols]),
                _dot(oc_ref[...], wc_ref[:, cols]))

    def gated(cols, ya, yb, yc):
        sig = lambda ref: _sigmoid(ref[:, cols].astype(_f32))
        return (sig(g0_ref) * ya + sig(g1_ref) * yb + sig(g2_ref) * yc).astype(_bf16)

    y = branches(pieces[0])
    mix = []
    for c, cols in enumerate(pieces):
        y_next = branches(pieces[c + 1]) if c + 1 < len(pieces) else None
        mix.append(gated(cols, *y))
        y = y_next
    out = _dot(jnp.concatenate(mix, axis=1), wo_ref[...])
    out_ref[...] = h_ref[...] + _rms(out, gp_ref[...])


def _merge(oa, ob, oc, nat, h, wa, wb, wc, wo, gp, *, tm=1024):
    t, d = h.shape
    row = lambda a: pl.BlockSpec((tm, a.shape[1]), lambda i: (i, 0))
    full = lambda a: pl.BlockSpec(a.shape, lambda i: (0, 0))
    gate = lambda j: pl.BlockSpec((tm, d), lambda i: (i, NAT_GATES // d + j))
    return pl.pallas_call(
        _merge_kernel,
        out_shape=jax.ShapeDtypeStruct((t, d), _f32),
        grid=(t // tm,),
        in_specs=[row(oa), row(ob), row(oc), gate(0), gate(1), gate(2), row(h),
                  full(wa), full(wb), full(wc), full(wo), full(gp)],
        out_specs=row(h),
        compiler_params=_params("arbitrary"),
        name="merge_out",
    )(oa, ob, oc, nat, nat, nat, h, wa, wb, wc, wo, gp)


HALO = 8
FFN_PIECE = 256

def _shift_rows(u, carry, s):
    r = pltpu.roll(u, s, axis=0)
    c = pltpu.roll(carry, s, axis=0)
    row = lax.broadcasted_iota(jnp.int32, carry.shape, 0)
    head = jnp.where(row < s, c, r[0:HALO])
    return jnp.concatenate([head, r[HALO:]], axis=0)


def _ffn_kernel(x_ref, gpre_ref, wa_ref, wb_ref, cwa_ref, cwb_ref, cba_ref, cbb_ref, wd_ref, gpost_ref,
                out_ref, hn_ref, acc_ref, ca_ref, cb_ref, *, tiles_per_seq):
    i, j = pl.program_id(0), pl.program_id(1)
    tm = x_ref.shape[0]
    tf = wa_ref.shape[1]

    @pl.when(j == 0)
    def _():
        hn_ref[...] = _rms(x_ref[...], gpre_ref[...]).astype(_bf16)
        acc_ref[...] = jnp.zeros_like(acc_ref)

    @pl.when(i == 0)
    def _():
        ca_ref[j] = jnp.zeros(ca_ref.shape[1:], _f32)
        cb_ref[j] = jnp.zeros(cb_ref.shape[1:], _f32)

    seq_start = (i % tiles_per_seq) == 0

    def up(cols):
        return _dot(hn_ref[...], wa_ref[:, cols]), _dot(hn_ref[...], wb_ref[:, cols])

    def conv(u, cols, cw_ref, cbias_ref, carry_ref):
        carry = jnp.where(seq_start, 0.0, carry_ref[j, :, cols])
        carry_ref[j, :, cols] = u[tm - HALO:tm]
        cw = cw_ref[:, cols]
        return (cw[2:3] * u + cw[1:2] * _shift_rows(u, carry, 1) + cw[0:1] * _shift_rows(u, carry, 2)
                + cbias_ref[:, cols])

    def gate(cols, ua, ub):
        a = conv(ua, cols, cwa_ref, cba_ref, ca_ref)
        b = conv(ub, cols, cwb_ref, cbb_ref, cb_ref)
        gelu = 0.5 * a * (1.0 + jnp.tanh(0.7978845608028654 * (a + 0.044715 * (a * a * a))))
        return (gelu * b).astype(_bf16)

    pieces = [slice(c, c + FFN_PIECE) for c in range(0, tf, FFN_PIECE)]
    u = up(pieces[0])
    ts = []
    for c, cols in enumerate(pieces):
        u_next = up(pieces[c + 1]) if c + 1 < len(pieces) else None
        ts.append(gate(cols, *u))
        u = u_next
    acc_ref[...] += _dot(jnp.concatenate(ts, axis=1), wd_ref[...])

    @pl.when(j == pl.num_programs(1) - 1)
    def _():
        out_ref[...] = x_ref[...] + _rms(acc_ref[...], gpost_ref[...])


def _ffn(h, gpre, w_up, conv_w, conv_b, w_down, gpost, *, seq, tm=1024, tf=1024):
    t, d = h.shape
    nf = D_FF // tf
    cb2 = conv_b.reshape(1, 2 * D_FF)
    return pl.pallas_call(
        functools.partial(_ffn_kernel, tiles_per_seq=seq // tm),
        out_shape=jax.ShapeDtypeStruct((t, d), _f32),
        grid=(t // tm, nf),
        in_specs=[
            pl.BlockSpec((tm, d), lambda i, j: (i, 0)),
            pl.BlockSpec((1, d), lambda i, j: (0, 0)),
            pl.BlockSpec((d, tf), lambda i, j: (0, j)),
            pl.BlockSpec((d, tf), lambda i, j: (0, nf + j)),
            pl.BlockSpec((3, tf), lambda i, j: (0, j)),
            pl.BlockSpec((3, tf), lambda i, j: (0, nf + j)),
            pl.BlockSpec((1, tf), lambda i, j: (0, j)),
            pl.BlockSpec((1, tf), lambda i, j: (0, nf + j)),
            pl.BlockSpec((tf, d), lambda i, j: (j, 0)),
            pl.BlockSpec((1, d), lambda i, j: (0, 0)),
        ],
        out_specs=pl.BlockSpec((tm, d), lambda i, j: (i, 0)),
        scratch_shapes=[
            pltpu.VMEM((tm, d), _bf16),
            pltpu.VMEM((tm, d), _f32),
            pltpu.VMEM((nf, HALO, tf), _f32),
            pltpu.VMEM((nf, HALO, tf), _f32),
        ],
        compiler_params=_params("arbitrary", "arbitrary"),
        name="conv_ffn",
    )(h, gpre, w_up, w_up, conv_w, conv_w, cb2, cb2, w_down, gpost)


def kernel(x, norm_pre_mix, norm_post_mix, norm_pre_ffn, norm_post_ffn, w_in, attn_sinks, hgrn_lb_logits,
           hgrn_out_norm, w_branch_a, w_branch_b, w_branch_c, w_out, w_ffn_up, ffn_conv_w, ffn_conv_b,
           w_ffn_down):
    bsz, seq, d = x.shape
    depth = w_in.shape[0]
    h = x.reshape(bsz * seq, d)
    row = lambda a: a.reshape(1, -1)
    for l in range(depth):
        nat, oc, g1, g2 = _in_proj(h, row(norm_pre_mix[l]), _permute_w_in(w_in[l]), hgrn_lb_logits,
                                   row(hgrn_out_norm[l]), layer=l, bsz=bsz, seq=seq)
        nat3 = nat.reshape(bsz, seq, NAT_W)
        oa = _dilated(nat3, g1, g2)
        ob = _swa(nat3, attn_sinks[l])
        h = _merge(oa, ob, oc, nat, h, w_branch_a[l].astype(_bf16), w_branch_b[l].astype(_bf16),
                   w_branch_c[l].astype(_bf16), w_out[l].astype(_bf16), row(norm_post_mix[l]))
        h = _ffn(h, row(norm_pre_ffn[l]), w_ffn_up[l].astype(_bf16), ffn_conv_w[l], ffn_conv_b[l],
                 w_ffn_down[l].astype(_bf16), row(norm_post_ffn[l]), seq=seq)
    return h.reshape(bsz, seq, d)
```

```python
import functools

import jax
import jax.numpy as jnp
from jax import lax
from jax.experimental import pallas as pl
from jax.experimental.pallas import tpu as pltpu

D_MODEL = 1024
HEAD_DIM = 64
DIL_PATTERNS = ((128, 1), (512, 4), (2048, 16))
SWA_WINDOW = 128
SWA_KV_HEADS = 2
HGRN_HEADS = 4
HGRN_DK = 128
HGRN_CHUNK = 32
HGRN_ATT_BLOCK = 128
D_FF = 4 * D_MODEL
BAND = 128
GROUP_W = 256
QKV_W = 3 * GROUP_W
EPS = 1e-6
LANES = 128
NEG = -1e30
ATTN_UNROLL = 2

SRC_AQ, SRC_AK, SRC_AV = 0, 768, 1536
SRC_B, SRC_C, SRC_GATES = 2304, 3072, 5120
NAT_GATES = 0
NAT_A0 = 3072
NAT_BQ, NAT_BK, NAT_BV = 3840, 4352, 4480
NAT_W = 4608
HGRN_W = HGRN_HEADS * HGRN_DK
COL_C = NAT_W
COL_G1 = COL_C + 4 * HGRN_W
COL_G2 = COL_G1 + QKV_W
D_IN = COL_G2 + QKV_W

VMEM_LIMIT = 56 * 1024 * 1024

_f32 = jnp.float32
_bf16 = jnp.bfloat16


def _dot(a, b):
    return jnp.dot(a, b, preferred_element_type=_f32)


def _dot_nt(a, b):
    return lax.dot_general(a, b, (((1,), (1,)), ((), ())), preferred_element_type=_f32)


def _dot_tn(a, b):
    return lax.dot_general(a, b, (((0,), (0,)), ((), ())), preferred_element_type=_f32)


def _rms(x, g):
    return x * lax.rsqrt(jnp.mean(x * x, axis=-1, keepdims=True) + EPS) * g


def _sigmoid(x):
    return 1.0 / (1.0 + jnp.exp(-x))


def _params(*sem):
    return pltpu.CompilerParams(dimension_semantics=sem, vmem_limit_bytes=VMEM_LIMIT)


def _permute_w_in(w):
    cols = lambda start, width: w[:, start:start + width]
    grp = lambda gi: [cols(SRC_AQ + gi * GROUP_W, GROUP_W), cols(SRC_AK + gi * GROUP_W, GROUP_W),
                      cols(SRC_AV + gi * GROUP_W, GROUP_W)]
    parts = [cols(SRC_GATES, 3 * D_MODEL)] + grp(0) + [cols(SRC_B, SRC_C - SRC_B), cols(SRC_C, SRC_GATES - SRC_C)]
    parts += grp(1) + grp(2)
    return jnp.concatenate(parts, axis=1).astype(_bf16)


def _in_proj_kernel(x_ref, g_ref, w_ref, lbl_ref, gain_ref, nat_ref, oc_ref, g1_ref, g2_ref,
                    hn_ref, c_ref, st_ref, *, tn, layer, tiles_per_seq):
    tm = x_ref.shape[0]

    @pl.when(pl.program_id(0) % tiles_per_seq == 0)
    def _():
        st_ref[...] = jnp.zeros_like(st_ref)

    hn = _rms(x_ref[...], g_ref[...])
    nlc = hn_ref.shape[0]
    for c in range(nlc):
        hn_ref[c] = hn[:, c * LANES:(c + 1) * LANES]
    hb = hn.astype(_bf16)
    def hgrn_chunk(c):
        c_ref[:, c:c + tn] = _dot(hb, w_ref[:, COL_C + c:COL_C + c + tn]).astype(_bf16)

    def nat_chunk(c):
        nat_ref[:, c:c + tn] = _dot(hb, w_ref[:, c:c + tn]).astype(_bf16)

    def dilated(out_ref, col0):
        d = out_ref.shape[0]
        rows = tm // d
        lhs = jnp.concatenate(
            [jnp.concatenate([hn_ref[c, pl.ds(r, rows, stride=d), :] for c in range(nlc)], axis=1)
             for r in range(d)], axis=0).astype(_bf16)
        y = _dot(lhs, w_ref[:, col0:col0 + QKV_W]).astype(_bf16)
        for r in range(d):
            out_ref[r] = y[r * rows:(r + 1) * rows]

    for c in range(0, 2 * HGRN_W, tn):
        hgrn_chunk(c)
    matmuls = [functools.partial(hgrn_chunk, c) for c in range(2 * HGRN_W, 4 * HGRN_W, tn)]
    matmuls += [functools.partial(nat_chunk, c) for c in range(0, NAT_W, tn)]
    matmuls += [functools.partial(dilated, g1_ref, COL_G1), functools.partial(dilated, g2_ref, COL_G2)]
    hgrn = _hgrn_stages(c_ref, lbl_ref, gain_ref, st_ref, oc_ref, layer=layer)
    per_matmul = -(-_hgrn_stage_count(tm) // len(matmuls))
    for mm in matmuls:
        mm()
        for _ in range(per_matmul):
            next(hgrn, None)
    for _ in hgrn:
        pass


def _in_proj(x, g, w, lb_logits, gain, *, layer, bsz, seq, tm=512, tn=512):
    t, d = x.shape
    tps = seq // tm
    d1, d2 = DIL_PATTERNS[1][1], DIL_PATTERNS[2][1]
    return pl.pallas_call(
        functools.partial(_in_proj_kernel, tn=tn, layer=layer, tiles_per_seq=tps),
        out_shape=(jax.ShapeDtypeStruct((t, NAT_W), _bf16),
                   jax.ShapeDtypeStruct((t, HGRN_W), _bf16),
                   jax.ShapeDtypeStruct((bsz, d1, seq // d1, QKV_W), _bf16),
                   jax.ShapeDtypeStruct((bsz, d2, seq // d2, QKV_W), _bf16)),
        grid=(t // tm,),
        in_specs=[
            pl.BlockSpec((tm, d), lambda i: (i, 0)),
            pl.BlockSpec((1, d), lambda i: (0, 0)),
            pl.BlockSpec((d, D_IN), lambda i: (0, 0), pipeline_mode=pl.Buffered(1)),
            pl.BlockSpec(lb_logits.shape, lambda i: (0, 0)),
            pl.BlockSpec((1, HGRN_DK), lambda i: (0, 0)),
        ],
        out_specs=(
            pl.BlockSpec((tm, NAT_W), lambda i: (i, 0)),
            pl.BlockSpec((tm, HGRN_W), lambda i: (i, 0)),
            pl.BlockSpec((None, d1, tm // d1, QKV_W), lambda i: (i // tps, 0, i % tps, 0)),
            pl.BlockSpec((None, d2, tm // d2, QKV_W), lambda i: (i // tps, 0, i % tps, 0)),
        ),
        scratch_shapes=[pltpu.VMEM((d // LANES, tm, LANES), _f32),
                        pltpu.VMEM((tm, 4 * HGRN_W), _bf16),
                        pltpu.VMEM((HGRN_HEADS, HGRN_DK, HGRN_DK), _f32)],
        compiler_params=_params("arbitrary"),
        name="in_proj",
    )(x, g, w, lb_logits, gain)


def _band_bias(max_dist, nk, first):
    qi = lax.broadcasted_iota(jnp.int32, (BAND, nk), 0)
    ki = lax.broadcasted_iota(jnp.int32, (BAND, nk), 1)
    dist = qi + (nk - BAND) - ki
    ok = (dist >= 0) & (dist <= max_dist)
    if first:
        ok = ok & (ki >= nk - BAND)
    return jnp.where(ok, 0.0, NEG).astype(_f32)


def _band_blocks(blocks, *, sink_of_head, want_lse):
    nh = GROUP_W // HEAD_DIM
    nk = blocks[0][1].shape[0]
    has_prev = nk == 2 * BAND
    lane_head = lax.broadcasted_iota(jnp.int32, (BAND, GROUP_W), 1) // HEAD_DIM
    fold = (lambda x, op: op(x[..., :BAND], x[..., BAND:])) if has_prev else (lambda x, op: x)
    scale = jnp.asarray(HEAD_DIM ** -0.5, _bf16)

    def stack_heads(q):
        q = q * scale
        return jnp.concatenate([jnp.where(lane_head == h, q, jnp.zeros_like(q)) for h in range(nh)], axis=0)

    def pick_heads(x):
        out = jnp.broadcast_to(x[0], (BAND, GROUP_W))
        for h in range(1, nh):
            out = jnp.where(lane_head == h, x[h], out)
        return out

    if sink_of_head is not None:
        head = lax.broadcasted_iota(jnp.int32, (nh, BAND, 1), 0)
        sink = jnp.zeros((nh, BAND, 1), _f32)
        for h in range(nh):
            sink = jnp.where(head == h, sink_of_head(h), sink)

    def scores(qh, kcat, bias):
        x = _dot_nt(qh, kcat).reshape(nh, BAND, nk) + bias[None]
        mx = jnp.max(fold(x, jnp.maximum), axis=-1, keepdims=True)
        return x, (mx if sink_of_head is None else jnp.maximum(mx, sink))

    def probs(x, mx):
        p = jnp.exp(x - mx)
        den = jnp.sum(fold(p, jnp.add), axis=-1, keepdims=True)
        if sink_of_head is not None:
            den = den + jnp.exp(sink - mx)
        return p.reshape(nh * BAND, nk).astype(_bf16), den

    qs = [stack_heads(q) for q, _, _, _ in blocks]
    sm = [scores(qh, kcat, bias) for qh, (_, kcat, _, bias) in zip(qs, blocks)]
    pd = [probs(x, mx) for x, mx in sm]
    pv = [_dot(p, vcat).reshape(nh, BAND, GROUP_W) for (p, _), (_, _, vcat, _) in zip(pd, blocks)]
    o = [pick_heads(x * (1.0 / d)) for x, (_, d) in zip(pv, pd)]
    if not want_lse:
        return [(x, None) for x in o]
    return [(x, pick_heads(mx + jnp.log(d))) for x, (_, mx), (_, d) in zip(o, sm, pd)]


def _qkv_cols(ref, rows, lead=()):
    idx = lambda c: lead + (rows, slice(c * GROUP_W, (c + 1) * GROUP_W))
    return ref[idx(0)], ref[idx(1)], ref[idx(2)]


def _dil_kernel(a0_ref, g1_ref, g2_ref, oa_ref, o_s, lse_s, bias_s):
    seq = a0_ref.shape[0]
    blks = functools.partial(_band_blocks, sink_of_head=None, want_lse=True)
    halves = [slice(hf * LANES, (hf + 1) * LANES) for hf in range(GROUP_W // LANES)]
    nu = ATTN_UNROLL
    win0, win1, win2 = (w // d for w, d in DIL_PATTERNS)
    d1, d2 = DIL_PATTERNS[1][1], DIL_PATTERNS[2][1]
    assert win0 == win1 and seq // d2 == BAND and win2 >= BAND - 1
    bias_s[0] = _band_bias(win0, 2 * BAND, True)
    bias_s[1] = _band_bias(win0, 2 * BAND, False)
    bias_s[2, :, 0:BAND] = _band_bias(win2, BAND, True)

    def put(g, rows, o, lse):
        for hf, ls in enumerate(halves):
            o_s[g, hf, rows, :] = o[:, ls]
            lse_s[g, hf, rows, :] = lse[:, ls]

    def band(ref, lead, i):
        cur = pl.ds(pl.multiple_of(i * BAND, BAND), BAND)
        prev = pl.ds(pl.multiple_of(jnp.maximum(i - 1, 0) * BAND, BAND), BAND)
        q, kc, vc = _qkv_cols(ref, cur, lead)
        _, kp, vp = _qkv_cols(ref, prev, lead)
        return (q, jnp.concatenate([kp, kc], axis=0), jnp.concatenate([vp, vc], axis=0),
                bias_s[jnp.minimum(i, 1)])

    def g0_body(n, c):
        ids = [n * nu + u for u in range(nu)]
        for i, (o, lse) in zip(ids, blks([band(a0_ref, (), i) for i in ids])):
            put(0, pl.ds(pl.multiple_of(i * BAND, BAND), BAND), o, lse)
        return c

    lax.fori_loop(0, seq // BAND // nu, g0_body, 0)

    nb1 = seq // d1 // BAND

    def g1_body(n, c):
        ids = [((n * nu + u) // nb1, (n * nu + u) % nb1) for u in range(nu)]
        for (r, i), (o, lse) in zip(ids, blks([band(g1_ref, (r,), i) for r, i in ids])):
            put(1, pl.ds(i * (BAND * d1) + r, BAND, stride=d1), o, lse)
        return c

    lax.fori_loop(0, d1 * nb1 // nu, g1_body, 0)

    def g2_body(n, c):
        ids = [n * nu + u for u in range(nu)]
        outs = blks([_qkv_cols(g2_ref, slice(None), (r,)) + (bias_s[2, :, 0:BAND],) for r in ids])
        for r, (o, lse) in zip(ids, outs):
            put(2, pl.ds(r, BAND, stride=d2), o, lse)
        return c

    lax.fori_loop(0, d2 // nu, g2_body, 0)

    def mix_body(i, c):
        rows = pl.ds(pl.multiple_of(i * BAND, BAND), BAND)
        for hf, ls in enumerate(halves):
            l0, l1, l2 = lse_s[0, hf, rows, :], lse_s[1, hf, rows, :], lse_s[2, hf, rows, :]
            m = jnp.maximum(jnp.maximum(l0, l1), l2)
            e0, e1, e2 = jnp.exp(l0 - m), jnp.exp(l1 - m), jnp.exp(l2 - m)
            oa = (e0 * o_s[0, hf, rows, :] + e1 * o_s[1, hf, rows, :] + e2 * o_s[2, hf, rows, :]) / (e0 + e1 + e2)
            oa_ref[rows, ls] = oa.astype(oa_ref.dtype)
        return c

    lax.fori_loop(0, seq // BAND, mix_body, 0)


def _dilated(nat3, g1, g2):
    bsz, seq, _ = nat3.shape
    n_grp = len(DIL_PATTERNS)
    return pl.pallas_call(
        _dil_kernel,
        out_shape=jax.ShapeDtypeStruct((bsz, seq, GROUP_W), _bf16),
        grid=(bsz,),
        in_specs=[
            pl.BlockSpec((None, seq, QKV_W), lambda b: (b, 0, NAT_A0 // QKV_W)),
            pl.BlockSpec((None,) + g1.shape[1:], lambda b: (b, 0, 0, 0)),
            pl.BlockSpec((None,) + g2.shape[1:], lambda b: (b, 0, 0, 0)),
        ],
        out_specs=pl.BlockSpec((None, seq, GROUP_W), lambda b: (b, 0, 0)),
        scratch_shapes=[pltpu.VMEM((n_grp, GROUP_W // LANES, seq, LANES), _f32)] * 2
        + [pltpu.VMEM((3, BAND, 2 * BAND), _f32)],
        compiler_params=_params("arbitrary"),
        name="dil_attn",
    )(nat3, g1, g2).reshape(bsz * seq, GROUP_W)


def _swa_kernel(sink_ref, q_ref, k_ref, v_ref, o_ref, kx_ref, vx_ref, bias_s):
    seq = q_ref.shape[0]
    nb = seq // BAND
    kvh = pl.program_id(1)
    bias_s[0] = _band_bias(SWA_WINDOW - 1, 2 * BAND, True)
    bias_s[1] = _band_bias(SWA_WINDOW - 1, 2 * BAND, False)
    lane = lax.broadcasted_iota(jnp.int32, (BAND, 2 * HEAD_DIM), 1)
    keep = (lane < HEAD_DIM) == (kvh == 0)

    def expand(i, c):
        r0 = pl.multiple_of(i * BAND, BAND)
        for src, dst in ((k_ref, kx_ref), (v_ref, vx_ref)):
            x = src[pl.ds(r0, BAND), :].astype(_f32)
            x2 = jnp.where(keep, x, pltpu.roll(x, HEAD_DIM, axis=1)).astype(_bf16)
            dst[pl.ds(r0, BAND), :] = jnp.concatenate([x2, x2], axis=1)
        return c

    lax.fori_loop(0, nb, expand, 0)
    heads_per_kv = GROUP_W // HEAD_DIM

    def band(i):
        cur = pl.ds(pl.multiple_of(i * BAND, BAND), BAND)
        prev = pl.ds(pl.multiple_of(jnp.maximum(i - 1, 0) * BAND, BAND), BAND)
        return (q_ref[cur, :], jnp.concatenate([kx_ref[prev, :], kx_ref[cur, :]], axis=0),
                jnp.concatenate([vx_ref[prev, :], vx_ref[cur, :]], axis=0), bias_s[jnp.minimum(i, 1)])

    def body(n, c):
        ids = [n * ATTN_UNROLL + u for u in range(ATTN_UNROLL)]
        outs = _band_blocks([band(i) for i in ids], want_lse=False,
                            sink_of_head=lambda h: sink_ref[kvh * heads_per_kv + h])
        for i, (o, _) in zip(ids, outs):
            o_ref[pl.ds(pl.multiple_of(i * BAND, BAND), BAND), :] = o.astype(o_ref.dtype)
        return c

    lax.fori_loop(0, nb // ATTN_UNROLL, body, 0)


def _swa(nat3, sinks):
    bsz, seq, _ = nat3.shape
    kvw = SWA_KV_HEADS * HEAD_DIM
    return pl.pallas_call(
        _swa_kernel,
        out_shape=jax.ShapeDtypeStruct((bsz, seq, SWA_KV_HEADS * GROUP_W), _bf16),
        grid=(bsz, SWA_KV_HEADS),
        in_specs=[
            pl.BlockSpec(memory_space=pltpu.SMEM),
            pl.BlockSpec((None, seq, GROUP_W), lambda b, g: (b, 0, NAT_BQ // GROUP_W + g)),
            pl.BlockSpec((None, seq, kvw), lambda b, g: (b, 0, NAT_BK // kvw)),
            pl.BlockSpec((None, seq, kvw), lambda b, g: (b, 0, NAT_BV // kvw)),
        ],
        out_specs=pl.BlockSpec((None, seq, GROUP_W), lambda b, g: (b, 0, g)),
        scratch_shapes=[pltpu.VMEM((seq, GROUP_W), _bf16), pltpu.VMEM((seq, GROUP_W), _bf16),
                        pltpu.VMEM((2, BAND, 2 * BAND), _f32)],
        compiler_params=_params("arbitrary", "arbitrary"),
        name="swa_attn",
    )(sinks, nat3, nat3, nat3).reshape(bsz * seq, SWA_KV_HEADS * GROUP_W)


HGRN_TILE = 256


def _hgrn_stage_count(tm):
    return 4 + (HGRN_TILE // HGRN_CHUNK + 1) * (tm // HGRN_TILE) + 1


def _hgrn_stages(c_ref, lbl_ref, gain_ref, st_ref, o_ref, *, layer):
    tm = c_ref.shape[0]
    tc = HGRN_TILE
    col = lambda k, sl: slice(k * HGRN_W + sl.start, k * HGRN_W + sl.stop)

    logits = lbl_ref[...]
    e = jnp.exp(logits - jnp.max(logits, axis=0, keepdims=True))
    sm = e / jnp.sum(e, axis=0, keepdims=True)
    lb = jnp.zeros((1, sm.shape[1]), _f32)
    for j in range(1, layer + 1):
        lb = lb + sm[j:j + 1, :]
    one_m_lb = 1.0 - lb

    nchunk = tc // HGRN_CHUNK
    row_in_chunk = lax.broadcasted_iota(jnp.int32, (tc, HGRN_DK), 0) & (HGRN_CHUNK - 1)
    scan_steps = [(s, row_in_chunk >= s) for s in (1, 2, 4, 8, 16) if s < HGRN_CHUNK]

    def chunk_prefix(x):
        for s, keep in scan_steps:
            x = x + jnp.where(keep, pltpu.roll(x, s, axis=0), 0.0)
        return x

    blk = HGRN_ATT_BLOCK
    tb = lax.broadcasted_iota(jnp.int32, (blk, blk), 0)
    sb = lax.broadcasted_iota(jnp.int32, (blk, blk), 1)
    causal_blk = jnp.logical_and((tb // HGRN_CHUNK) == (sb // HGRN_CHUNK), sb <= tb)

    heads = [slice(h * HGRN_DK, (h + 1) * HGRN_DK) for h in range(HGRN_HEADS)]
    subs = [slice(r0, r0 + tc) for r0 in range(0, tm, tc)]
    items = [(rows, sl) for rows in subs for sl in heads]
    def gates(rows, sl):
        qraw = c_ref[rows, col(0, sl)].astype(_f32)
        fz = c_ref[rows, col(1, sl)].astype(_f32)
        t = jnp.exp(-jnp.abs(fz))
        r = 1.0 / (1.0 + t)
        tr = t * r
        pos = fz >= 0.0
        logf = jnp.log(lb[:, sl] + one_m_lb[:, sl] * jnp.where(pos, r, tr))
        k = one_m_lb[:, sl] * jnp.where(pos, tr, r)
        return qraw * _sigmoid(qraw), k, logf

    def decayed(q, k, logf):
        eb = jnp.exp(chunk_prefix(logf))
        last = [eb[n * HGRN_CHUNK + HGRN_CHUNK - 1:(n + 1) * HGRN_CHUNK, :] for n in range(nchunk)]
        dec = jnp.concatenate([jnp.broadcast_to(d, (HGRN_CHUNK, HGRN_DK)) for d in last], axis=0)
        kinv = k * (1.0 / eb)
        return (q * eb).astype(_bf16), kinv.astype(_bf16), (kinv * dec).astype(_bf16), last

    qkl = [gates(rows, sl) for rows, sl in items]
    yield
    qkd = [decayed(q, k, logf) for q, k, logf in qkl]
    vs = [c_ref[rows, col(2, sl)] for rows, sl in items]
    yield
    att = [[jnp.where(causal_blk, _dot_nt(qe[r0:r0 + blk], ke[r0:r0 + blk]), 0.0).astype(_bf16)
            for r0 in range(0, tc, blk)] for qe, ke, _, _ in qkd]
    yield
    intra = [jnp.concatenate([_dot(a, v[i * blk:(i + 1) * blk]) for i, a in enumerate(att_h)], axis=0)
             for att_h, v in zip(att, vs)]
    yield

    states = [st_ref[h] for h in range(HGRN_HEADS)]
    for s, rows in enumerate(subs):
        per_head = [(qkd[i][0], qkd[i][2], vs[i], qkd[i][3], intra[i])
                    for i in range(s * HGRN_HEADS, (s + 1) * HGRN_HEADS)]
        inter = [[] for _ in heads]
        for n in range(nchunk):
            rs = slice(n * HGRN_CHUNK, (n + 1) * HGRN_CHUNK)
            for h, (qe, kd, v, dec, _) in enumerate(per_head):
                inter[h].append(_dot_nt(qe[rs], states[h].astype(_bf16)))
                states[h] = states[h] * dec[n] + _dot_tn(v[rs], kd[rs])
            yield
        for h, sl in enumerate(heads):
            o = per_head[h][4] + jnp.concatenate(inter[h], axis=0)
            y = _rms(o, gain_ref[...])
            graw = c_ref[rows, col(3, sl)].astype(_f32)
            o_ref[rows, sl] = (y * (graw * _sigmoid(graw))).astype(o_ref.dtype)
        yield
    for h in range(HGRN_HEADS):
        st_ref[h] = states[h]
    yield


MERGE_PIECE = 256


def _merge_kernel(oa_ref, ob_ref, oc_ref, g0_ref, g1_ref, g2_ref, h_ref, wa_ref, wb_ref, wc_ref, wo_ref, gp_ref,
                  out_ref):
    d = wo_ref.shape[0]
    pieces = [slice(c, c + MERGE_PIECE) for c in range(0, d, MERGE_PIECE)]

    def branches(cols):
        return (_dot(oa_ref[...], wa_ref[:, cols]), _dot(ob_ref[...], wb_ref[:, cols]),
                _dot(oc_ref[...], wc_ref[:, cols]))

    def gated(cols, ya, yb, yc):
        sig = lambda ref: _sigmoid(ref[:, cols].astype(_f32))
        return (sig(g0_ref) * ya + sig(g1_ref) * yb + sig(g2_ref) * yc).astype(_bf16)

    y = branches(pieces[0])
    mix = []
    for c, cols in enumerate(pieces):
        y_next = branches(pieces[c + 1]) if c + 1 < len(pieces) else None
        mix.append(gated(cols, *y))
        y = y_next
    out = _dot(jnp.concatenate(mix, axis=1), wo_ref[...])
    out_ref[...] = h_ref[...] + _rms(out, gp_ref[...])


def _merge(oa, ob, oc, nat, h, wa, wb, wc, wo, gp, *, tm=1024):
    t, d = h.shape
    row = lambda a: pl.BlockSpec((tm, a.shape[1]), lambda i: (i, 0))
    full = lambda a: pl.BlockSpec(a.shape, lambda i: (0, 0))
    gate = lambda j: pl.BlockSpec((tm, d), lambda i: (i, NAT_GATES // d + j))
    return pl.pallas_call(
        _merge_kernel,
        out_shape=jax.ShapeDtypeStruct((t, d), _f32),
        grid=(t // tm,),
        in_specs=[row(oa), row(ob), row(oc), gate(0), gate(1), gate(2), row(h),
                  full(wa), full(wb), full(wc), full(wo), full(gp)],
        out_specs=row(h),
        compiler_params=_params("arbitrary"),
        name="merge_out",
    )(oa, ob, oc, nat, nat, nat, h, wa, wb, wc, wo, gp)


HALO = 8
FFN_PIECE = 256

def _shift_rows(u, carry, s):
    r = pltpu.roll(u, s, axis=0)
    c = pltpu.roll(carry, s, axis=0)
    row = lax.broadcasted_iota(jnp.int32, carry.shape, 0)
    head = jnp.where(row < s, c, r[0:HALO])
    return jnp.concatenate([head, r[HALO:]], axis=0)


def _ffn_kernel(x_ref, gpre_ref, wa_ref, wb_ref, cwa_ref, cwb_ref, cba_ref, cbb_ref, wd_ref, gpost_ref,
                out_ref, hn_ref, acc_ref, ca_ref, cb_ref, *, tiles_per_seq):
    i, j = pl.program_id(0), pl.program_id(1)
    tm = x_ref.shape[0]
    tf = wa_ref.shape[1]

    @pl.when(j == 0)
    def _():
        hn_ref[...] = _rms(x_ref[...], gpre_ref[...]).astype(_bf16)
        acc_ref[...] = jnp.zeros_like(acc_ref)

    @pl.when(i == 0)
    def _():
        ca_ref[j] = jnp.zeros(ca_ref.shape[1:], _f32)
        cb_ref[j] = jnp.zeros(cb_ref.shape[1:], _f32)

    seq_start = (i % tiles_per_seq) == 0

    def up(cols):
        return _dot(hn_ref[...], wa_ref[:, cols]), _dot(hn_ref[...], wb_ref[:, cols])

    def conv(u, cols, cw_ref, cbias_ref, carry_ref):
        carry = jnp.where(seq_start, 0.0, carry_ref[j, :, cols])
        carry_ref[j, :, cols] = u[tm - HALO:tm]
        cw = cw_ref[:, cols]
        return (cw[2:3] * u + cw[1:2] * _shift_rows(u, carry, 1) + cw[0:1] * _shift_rows(u, carry, 2)
                + cbias_ref[:, cols])

    def gate(cols, ua, ub):
        a = conv(ua, cols, cwa_ref, cba_ref, ca_ref)
        b = conv(ub, cols, cwb_ref, cbb_ref, cb_ref)
        gelu = 0.5 * a * (1.0 + jnp.tanh(0.7978845608028654 * (a + 0.044715 * (a * a * a))))
        return (gelu * b).astype(_bf16)

    pieces = [slice(c, c + FFN_PIECE) for c in range(0, tf, FFN_PIECE)]
    u = up(pieces[0])
    ts = []
    for c, cols in enumerate(pieces):
        u_next = up(pieces[c + 1]) if c + 1 < len(pieces) else None
        ts.append(gate(cols, *u))
        u = u_next
    acc_ref[...] += _dot(jnp.concatenate(ts, axis=1), wd_ref[...])

    @pl.when(j == pl.num_programs(1) - 1)
    def _():
        out_ref[...] = x_ref[...] + _rms(acc_ref[...], gpost_ref[...])


def _ffn(h, gpre, w_up, conv_w, conv_b, w_down, gpost, *, seq, tm=1024, tf=1024):
    t, d = h.shape
    nf = D_FF // tf
    cb2 = conv_b.reshape(1, 2 * D_FF)
    return pl.pallas_call(
        functools.partial(_ffn_kernel, tiles_per_seq=seq // tm),
        out_shape=jax.ShapeDtypeStruct((t, d), _f32),
        grid=(t // tm, nf),
        in_specs=[
            pl.BlockSpec((tm, d), lambda i, j: (i, 0)),
            pl.BlockSpec((1, d), lambda i, j: (0, 0)),
            pl.BlockSpec((d, tf), lambda i, j: (0, j)),
            pl.BlockSpec((d, tf), lambda i, j: (0, nf + j)),
            pl.BlockSpec((3, tf), lambda i, j: (0, j)),
            pl.BlockSpec((3, tf), lambda i, j: (0, nf + j)),
            pl.BlockSpec((1, tf), lambda i, j: (0, j)),
            pl.BlockSpec((1, tf), lambda i, j: (0, nf + j)),
            pl.BlockSpec((tf, d), lambda i, j: (j, 0)),
            pl.BlockSpec((1, d), lambda i, j: (0, 0)),
        ],
        out_specs=pl.BlockSpec((tm, d), lambda i, j: (i, 0)),
        scratch_shapes=[
            pltpu.VMEM((tm, d), _bf16),
            pltpu.VMEM((tm, d), _f32),
            pltpu.VMEM((nf, HALO, tf), _f32),
            pltpu.VMEM((nf, HALO, tf), _f32),
        ],
        compiler_params=_params("arbitrary", "arbitrary"),
        name="conv_ffn",
    )(h, gpre, w_up, w_up, conv_w, conv_w, cb2, cb2, w_down, gpost)


def kernel(x, norm_pre_mix, norm_post_mix, norm_pre_ffn, norm_post_ffn, w_in, attn_sinks, hgrn_lb_logits,
           hgrn_out_norm, w_branch_a, w_branch_b, w_branch_c, w_out, w_ffn_up, ffn_conv_w, ffn_conv_b,
           w_ffn_down):
    bsz, seq, d = x.shape
    depth = w_in.shape[0]
    h = x.reshape(bsz * seq, d)
    row = lambda a: a.reshape(1, -1)
    for l in range(depth):
        nat, oc, g1, g2 = _in_proj(h, row(norm_pre_mix[l]), _permute_w_in(w_in[l]), hgrn_lb_logits,
                                   row(hgrn_out_norm[l]), layer=l, bsz=bsz, seq=seq)
        nat3 = nat.reshape(bsz, seq, NAT_W)
        oa = _dilated(nat3, g1, g2)
        ob = _swa(nat3, attn_sinks[l])
        h = _merge(oa, ob, oc, nat, h, w_branch_a[l].astype(_bf16), w_branch_b[l].astype(_bf16),
                   w_branch_c[l].astype(_bf16), w_out[l].astype(_bf16), row(norm_post_mix[l]))
        h = _ffn(h, row(norm_pre_ffn[l]), w_ffn_up[l].astype(_bf16), ffn_conv_w[l], ffn_conv_b[l],
                 w_ffn_down[l].astype(_bf16), row(norm_post_ffn[l]), seq=seq)
    return h.reshape(bsz, seq, d)
```

```python
import functools

import jax
import jax.numpy as jnp
from jax import lax
from jax.experimental import pallas as pl
from jax.experimental.pallas import tpu as pltpu

D_MODEL = 1024
HEAD_DIM = 64
DIL_PATTERNS = ((128, 1), (512, 4), (2048, 16))
SWA_WINDOW = 128
SWA_KV_HEADS = 2
HGRN_HEADS = 4
HGRN_DK = 128
HGRN_CHUNK = 32
HGRN_ATT_BLOCK = 128
D_FF = 4 * D_MODEL
BAND = 128
GROUP_W = 256
QKV_W = 3 * GROUP_W
EPS = 1e-6
LANES = 128
NEG = -1e30
ATTN_UNROLL = 4

SRC_AQ, SRC_AK, SRC_AV = 0, 768, 1536
SRC_B, SRC_C, SRC_GATES = 2304, 3072, 5120
NAT_GATES = 0
NAT_A0 = 3072
NAT_BQ, NAT_BK, NAT_BV = 3840, 4352, 4480
NAT_W = 4608
HGRN_W = HGRN_HEADS * HGRN_DK
COL_C = NAT_W
COL_G1 = COL_C + 4 * HGRN_W
COL_G2 = COL_G1 + QKV_W
D_IN = COL_G2 + QKV_W

VMEM_LIMIT = 56 * 1024 * 1024

_f32 = jnp.float32
_bf16 = jnp.bfloat16


def _dot(a, b):
    return jnp.dot(a, b, preferred_element_type=_f32)


def _dot_nt(a, b):
    return lax.dot_general(a, b, (((1,), (1,)), ((), ())), preferred_element_type=_f32)


def _dot_tn(a, b):
    return lax.dot_general(a, b, (((0,), (0,)), ((), ())), preferred_element_type=_f32)


def _rms(x, g):
    return x * lax.rsqrt(jnp.mean(x * x, axis=-1, keepdims=True) + EPS) * g


def _sigmoid(x):
    return 1.0 / (1.0 + jnp.exp(-x))


def _params(*sem):
    return pltpu.CompilerParams(dimension_semantics=sem, vmem_limit_bytes=VMEM_LIMIT)


def _permute_w_in(w):
    cols = lambda start, width: w[:, start:start + width]
    grp = lambda gi: [cols(SRC_AQ + gi * GROUP_W, GROUP_W), cols(SRC_AK + gi * GROUP_W, GROUP_W),
                      cols(SRC_AV + gi * GROUP_W, GROUP_W)]
    parts = [cols(SRC_GATES, 3 * D_MODEL)] + grp(0) + [cols(SRC_B, SRC_C - SRC_B), cols(SRC_C, SRC_GATES - SRC_C)]
    parts += grp(1) + grp(2)
    return jnp.concatenate(parts, axis=1).astype(_bf16)


def _in_proj_kernel(x_ref, g_ref, w_ref, lbl_ref, gain_ref, nat_ref, oc_ref, g1_ref, g2_ref,
                    hn_ref, c_ref, st_ref, *, tn, layer, tiles_per_seq):
    tm = x_ref.shape[0]

    @pl.when(pl.program_id(0) % tiles_per_seq == 0)
    def _():
        st_ref[...] = jnp.zeros_like(st_ref)

    hn = _rms(x_ref[...], g_ref[...])
    nlc = hn_ref.shape[0]
    for c in range(nlc):
        hn_ref[c] = hn[:, c * LANES:(c + 1) * LANES]
    hb = hn.astype(_bf16)
    def hgrn_chunk(c):
        c_ref[:, c:c + tn] = _dot(hb, w_ref[:, COL_C + c:COL_C + c + tn]).astype(_bf16)

    def nat_chunk(c):
        nat_ref[:, c:c + tn] = _dot(hb, w_ref[:, c:c + tn]).astype(_bf16)

    def dilated(out_ref, col0):
        d = out_ref.shape[0]
        rows = tm // d
        lhs = jnp.concatenate(
            [jnp.concatenate([hn_ref[c, pl.ds(r, rows, stride=d), :] for c in range(nlc)], axis=1)
             for r in range(d)], axis=0).astype(_bf16)
        y = _dot(lhs, w_ref[:, col0:col0 + QKV_W]).astype(_bf16)
        for r in range(d):
            out_ref[r] = y[r * rows:(r + 1) * rows]

    for c in range(0, 2 * HGRN_W, tn):
        hgrn_chunk(c)
    matmuls = [functools.partial(hgrn_chunk, c) for c in range(2 * HGRN_W, 4 * HGRN_W, tn)]
    matmuls += [functools.partial(nat_chunk, c) for c in range(0, NAT_W, tn)]
    matmuls += [functools.partial(dilated, g1_ref, COL_G1), functools.partial(dilated, g2_ref, COL_G2)]
    hgrn = _hgrn_stages(c_ref, lbl_ref, gain_ref, st_ref, oc_ref, layer=layer)
    per_matmul = -(-_hgrn_stage_count(tm) // len(matmuls))
    for mm in matmuls:
        mm()
        for _ in range(per_matmul):
            next(hgrn, None)
    for _ in hgrn:
        pass


def _in_proj(x, g, w, lb_logits, gain, *, layer, bsz, seq, tm=512, tn=512):
    t, d = x.shape
    tps = seq // tm
    d1, d2 = DIL_PATTERNS[1][1], DIL_PATTERNS[2][1]
    return pl.pallas_call(
        functools.partial(_in_proj_kernel, tn=tn, layer=layer, tiles_per_seq=tps),
        out_shape=(jax.ShapeDtypeStruct((t, NAT_W), _bf16),
                   jax.ShapeDtypeStruct((t, HGRN_W), _bf16),
                   jax.ShapeDtypeStruct((bsz, d1, seq // d1, QKV_W), _bf16),
                   jax.ShapeDtypeStruct((bsz, d2, seq // d2, QKV_W), _bf16)),
        grid=(t // tm,),
        in_specs=[
            pl.BlockSpec((tm, d), lambda i: (i, 0)),
            pl.BlockSpec((1, d), lambda i: (0, 0)),
            pl.BlockSpec((d, D_IN), lambda i: (0, 0), pipeline_mode=pl.Buffered(1)),
            pl.BlockSpec(lb_logits.shape, lambda i: (0, 0)),
            pl.BlockSpec((1, HGRN_DK), lambda i: (0, 0)),
        ],
        out_specs=(
            pl.BlockSpec((tm, NAT_W), lambda i: (i, 0)),
            pl.BlockSpec((tm, HGRN_W), lambda i: (i, 0)),
            pl.BlockSpec((None, d1, tm // d1, QKV_W), lambda i: (i // tps, 0, i % tps, 0)),
            pl.BlockSpec((None, d2, tm // d2, QKV_W), lambda i: (i // tps, 0, i % tps, 0)),
        ),
        scratch_shapes=[pltpu.VMEM((d // LANES, tm, LANES), _f32),
                        pltpu.VMEM((tm, 4 * HGRN_W), _bf16),
                        pltpu.VMEM((HGRN_HEADS, HGRN_DK, HGRN_DK), _f32)],
        compiler_params=_params("arbitrary"),
        name="in_proj",
    )(x, g, w, lb_logits, gain)


def _band_bias(max_dist, nk, first):
    qi = lax.broadcasted_iota(jnp.int32, (BAND, nk), 0)
    ki = lax.broadcasted_iota(jnp.int32, (BAND, nk), 1)
    dist = qi + (nk - BAND) - ki
    ok = (dist >= 0) & (dist <= max_dist)
    if first:
        ok = ok & (ki >= nk - BAND)
    return jnp.where(ok, 0.0, NEG).astype(_f32)


def _band_blocks(blocks, *, sink_of_head, want_lse):
    nh = GROUP_W // HEAD_DIM
    nk = blocks[0][1].shape[0]
    has_prev = nk == 2 * BAND
    lane_head = lax.broadcasted_iota(jnp.int32, (BAND, GROUP_W), 1) // HEAD_DIM
    fold = (lambda x, op: op(x[..., :BAND], x[..., BAND:])) if has_prev else (lambda x, op: x)
    scale = jnp.asarray(HEAD_DIM ** -0.5, _bf16)

    def stack_heads(q):
        q = q * scale
        return jnp.concatenate([jnp.where(lane_head == h, q, jnp.zeros_like(q)) for h in range(nh)], axis=0)

    def pick_heads(x):
        out = jnp.broadcast_to(x[0], (BAND, GROUP_W))
        for h in range(1, nh):
            out = jnp.where(lane_head == h, x[h], out)
        return out

    if sink_of_head is not None:
        head = lax.broadcasted_iota(jnp.int32, (nh, BAND, 1), 0)
        sink = jnp.zeros((nh, BAND, 1), _f32)
        for h in range(nh):
            sink = jnp.where(head == h, sink_of_head(h), sink)

    def scores(qh, kcat, bias):
        x = _dot_nt(qh, kcat).reshape(nh, BAND, nk) + bias[None]
        mx = jnp.max(fold(x, jnp.maximum), axis=-1, keepdims=True)
        return x, (mx if sink_of_head is None else jnp.maximum(mx, sink))

    def probs(x, mx):
        p = jnp.exp(x - mx)
        den = jnp.sum(fold(p, jnp.add), axis=-1, keepdims=True)
        if sink_of_head is not None:
            den = den + jnp.exp(sink - mx)
        return p.reshape(nh * BAND, nk).astype(_bf16), den

    qs = [stack_heads(q) for q, _, _, _ in blocks]
    sm = [scores(qh, kcat, bias) for qh, (_, kcat, _, bias) in zip(qs, blocks)]
    pd = [probs(x, mx) for x, mx in sm]
    pv = [_dot(p, vcat).reshape(nh, BAND, GROUP_W) for (p, _), (_, _, vcat, _) in zip(pd, blocks)]
    o = [pick_heads(x * (1.0 / d)) for x, (_, d) in zip(pv, pd)]
    if not want_lse:
        return [(x, None) for x in o]
    return [(x, pick_heads(mx + jnp.log(d))) for x, (_, mx), (_, d) in zip(o, sm, pd)]


def _qkv_cols(ref, rows, lead=()):
    idx = lambda c: lead + (rows, slice(c * GROUP_W, (c + 1) * GROUP_W))
    return ref[idx(0)], ref[idx(1)], ref[idx(2)]


def _dil_kernel(a0_ref, g1_ref, g2_ref, oa_ref, o_s, lse_s, bias_s):
    seq = a0_ref.shape[0]
    blks = functools.partial(_band_blocks, sink_of_head=None, want_lse=True)
    halves = [slice(hf * LANES, (hf + 1) * LANES) for hf in range(GROUP_W // LANES)]
    nu = ATTN_UNROLL
    win0, win1, win2 = (w // d for w, d in DIL_PATTERNS)
    d1, d2 = DIL_PATTERNS[1][1], DIL_PATTERNS[2][1]
    assert win0 == win1 and seq // d2 == BAND and win2 >= BAND - 1
    bias_s[0] = _band_bias(win0, 2 * BAND, True)
    bias_s[1] = _band_bias(win0, 2 * BAND, False)
    bias_s[2, :, 0:BAND] = _band_bias(win2, BAND, True)

    def put(g, rows, o, lse):
        for hf, ls in enumerate(halves):
            o_s[g, hf, rows, :] = o[:, ls]
            lse_s[g, hf, rows, :] = lse[:, ls]

    def band(ref, lead, i):
        cur = pl.ds(pl.multiple_of(i * BAND, BAND), BAND)
        prev = pl.ds(pl.multiple_of(jnp.maximum(i - 1, 0) * BAND, BAND), BAND)
        q, kc, vc = _qkv_cols(ref, cur, lead)
        _, kp, vp = _qkv_cols(ref, prev, lead)
        return (q, jnp.concatenate([kp, kc], axis=0), jnp.concatenate([vp, vc], axis=0),
                bias_s[jnp.minimum(i, 1)])

    def g0_body(n, c):
        ids = [n * nu + u for u in range(nu)]
        for i, (o, lse) in zip(ids, blks([band(a0_ref, (), i) for i in ids])):
            put(0, pl.ds(pl.multiple_of(i * BAND, BAND), BAND), o, lse)
        return c

    lax.fori_loop(0, seq // BAND // nu, g0_body, 0)

    nb1 = seq // d1 // BAND

    def g1_body(n, c):
        ids = [((n * nu + u) // nb1, (n * nu + u) % nb1) for u in range(nu)]
        for (r, i), (o, lse) in zip(ids, blks([band(g1_ref, (r,), i) for r, i in ids])):
            put(1, pl.ds(i * (BAND * d1) + r, BAND, stride=d1), o, lse)
        return c

    lax.fori_loop(0, d1 * nb1 // nu, g1_body, 0)

    def g2_body(n, c):
        ids = [n * nu + u for u in range(nu)]
        outs = blks([_qkv_cols(g2_ref, slice(None), (r,)) + (bias_s[2, :, 0:BAND],) for r in ids])
        for r, (o, lse) in zip(ids, outs):
            put(2, pl.ds(r, BAND, stride=d2), o, lse)
        return c

    lax.fori_loop(0, d2 // nu, g2_body, 0)

    def mix_body(i, c):
        rows = pl.ds(pl.multiple_of(i * BAND, BAND), BAND)
        for hf, ls in enumerate(halves):
            l0, l1, l2 = lse_s[0, hf, rows, :], lse_s[1, hf, rows, :], lse_s[2, hf, rows, :]
            m = jnp.maximum(jnp.maximum(l0, l1), l2)
            e0, e1, e2 = jnp.exp(l0 - m), jnp.exp(l1 - m), jnp.exp(l2 - m)
            oa = (e0 * o_s[0, hf, rows, :] + e1 * o_s[1, hf, rows, :] + e2 * o_s[2, hf, rows, :]) / (e0 + e1 + e2)
            oa_ref[rows, ls] = oa.astype(oa_ref.dtype)
        return c

    lax.fori_loop(0, seq // BAND, mix_body, 0)


def _dilated(nat3, g1, g2):
    bsz, seq, _ = nat3.shape
    n_grp = len(DIL_PATTERNS)
    return pl.pallas_call(
        _dil_kernel,
        out_shape=jax.ShapeDtypeStruct((bsz, seq, GROUP_W), _bf16),
        grid=(bsz,),
        in_specs=[
            pl.BlockSpec((None, seq, QKV_W), lambda b: (b, 0, NAT_A0 // QKV_W)),
            pl.BlockSpec((None,) + g1.shape[1:], lambda b: (b, 0, 0, 0)),
            pl.BlockSpec((None,) + g2.shape[1:], lambda b: (b, 0, 0, 0)),
        ],
        out_specs=pl.BlockSpec((None, seq, GROUP_W), lambda b: (b, 0, 0)),
        scratch_shapes=[pltpu.VMEM((n_grp, GROUP_W // LANES, seq, LANES), _f32)] * 2
        + [pltpu.VMEM((3, BAND, 2 * BAND), _f32)],
        compiler_params=_params("arbitrary"),
        name="dil_attn",
    )(nat3, g1, g2).reshape(bsz * seq, GROUP_W)


def _swa_kernel(sink_ref, q_ref, k_ref, v_ref, o_ref, kx_ref, vx_ref, bias_s):
    seq = q_ref.shape[0]
    nb = seq // BAND
    kvh = pl.program_id(1)
    bias_s[0] = _band_bias(SWA_WINDOW - 1, 2 * BAND, True)
    bias_s[1] = _band_bias(SWA_WINDOW - 1, 2 * BAND, False)
    lane = lax.broadcasted_iota(jnp.int32, (BAND, 2 * HEAD_DIM), 1)
    keep = (lane < HEAD_DIM) == (kvh == 0)

    def expand(i, c):
        r0 = pl.multiple_of(i * BAND, BAND)
        for src, dst in ((k_ref, kx_ref), (v_ref, vx_ref)):
            x = src[pl.ds(r0, BAND), :].astype(_f32)
            x2 = jnp.where(keep, x, pltpu.roll(x, HEAD_DIM, axis=1)).astype(_bf16)
            dst[pl.ds(r0, BAND), :] = jnp.concatenate([x2, x2], axis=1)
        return c

    lax.fori_loop(0, nb, expand, 0)
    heads_per_kv = GROUP_W // HEAD_DIM

    def band(i):
        cur = pl.ds(pl.multiple_of(i * BAND, BAND), BAND)
        prev = pl.ds(pl.multiple_of(jnp.maximum(i - 1, 0) * BAND, BAND), BAND)
        return (q_ref[cur, :], jnp.concatenate([kx_ref[prev, :], kx_ref[cur, :]], axis=0),
                jnp.concatenate([vx_ref[prev, :], vx_ref[cur, :]], axis=0), bias_s[jnp.minimum(i, 1)])

    def body(n, c):
        ids = [n * ATTN_UNROLL + u for u in range(ATTN_UNROLL)]
        outs = _band_blocks([band(i) for i in ids], want_lse=False,
                            sink_of_head=lambda h: sink_ref[kvh * heads_per_kv + h])
        for i, (o, _) in zip(ids, outs):
            o_ref[pl.ds(pl.multiple_of(i * BAND, BAND), BAND), :] = o.astype(o_ref.dtype)
        return c

    lax.fori_loop(0, nb // ATTN_UNROLL, body, 0)


def _swa(nat3, sinks):
    bsz, seq, _ = nat3.shape
    kvw = SWA_KV_HEADS * HEAD_DIM
    return pl.pallas_call(
        _swa_kernel,
        out_shape=jax.ShapeDtypeStruct((bsz, seq, SWA_KV_HEADS * GROUP_W), _bf16),
        grid=(bsz, SWA_KV_HEADS),
        in_specs=[
            pl.BlockSpec(memory_space=pltpu.SMEM),
            pl.BlockSpec((None, seq, GROUP_W), lambda b, g: (b, 0, NAT_BQ // GROUP_W + g)),
            pl.BlockSpec((None, seq, kvw), lambda b, g: (b, 0, NAT_BK // kvw)),
            pl.BlockSpec((None, seq, kvw), lambda b, g: (b, 0, NAT_BV // kvw)),
        ],
        out_specs=pl.BlockSpec((None, seq, GROUP_W), lambda b, g: (b, 0, g)),
        scratch_shapes=[pltpu.VMEM((seq, GROUP_W), _bf16), pltpu.VMEM((seq, GROUP_W), _bf16),
                        pltpu.VMEM((2, BAND, 2 * BAND), _f32)],
        compiler_params=_params("arbitrary", "arbitrary"),
        name="swa_attn",
    )(sinks, nat3, nat3, nat3).reshape(bsz * seq, SWA_KV_HEADS * GROUP_W)


HGRN_TILE = 256


def _hgrn_stage_count(tm):
    return 4 + (HGRN_TILE // HGRN_CHUNK + 1) * (tm // HGRN_TILE) + 1


def _hgrn_stages(c_ref, lbl_ref, gain_ref, st_ref, o_ref, *, layer):
    tm = c_ref.shape[0]
    tc = HGRN_TILE
    col = lambda k, sl: slice(k * HGRN_W + sl.start, k * HGRN_W + sl.stop)

    logits = lbl_ref[...]
    e = jnp.exp(logits - jnp.max(logits, axis=0, keepdims=True))
    sm = e / jnp.sum(e, axis=0, keepdims=True)
    lb = jnp.zeros((1, sm.shape[1]), _f32)
    for j in range(1, layer + 1):
        lb = lb + sm[j:j + 1, :]
    one_m_lb = 1.0 - lb

    nchunk = tc // HGRN_CHUNK
    row_in_chunk = lax.broadcasted_iota(jnp.int32, (tc, HGRN_DK), 0) & (HGRN_CHUNK - 1)
    scan_steps = [(s, row_in_chunk >= s) for s in (1, 2, 4, 8, 16) if s < HGRN_CHUNK]

    def chunk_prefix(x):
        for s, keep in scan_steps:
            x = x + jnp.where(keep, pltpu.roll(x, s, axis=0), 0.0)
        return x

    blk = HGRN_ATT_BLOCK
    tb = lax.broadcasted_iota(jnp.int32, (blk, blk), 0)
    sb = lax.broadcasted_iota(jnp.int32, (blk, blk), 1)
    causal_blk = jnp.logical_and((tb // HGRN_CHUNK) == (sb // HGRN_CHUNK), sb <= tb)

    heads = [slice(h * HGRN_DK, (h + 1) * HGRN_DK) for h in range(HGRN_HEADS)]
    subs = [slice(r0, r0 + tc) for r0 in range(0, tm, tc)]
    items = [(rows, sl) for rows in subs for sl in heads]
    def gates(rows, sl):
        qraw = c_ref[rows, col(0, sl)].astype(_f32)
        fz = c_ref[rows, col(1, sl)].astype(_f32)
        t = jnp.exp(-jnp.abs(fz))
        r = 1.0 / (1.0 + t)
        tr = t * r
        pos = fz >= 0.0
        logf = jnp.log(lb[:, sl] + one_m_lb[:, sl] * jnp.where(pos, r, tr))
        k = one_m_lb[:, sl] * jnp.where(pos, tr, r)
        return qraw * _sigmoid(qraw), k, logf

    def decayed(q, k, logf):
        eb = jnp.exp(chunk_prefix(logf))
        last = [eb[n * HGRN_CHUNK + HGRN_CHUNK - 1:(n + 1) * HGRN_CHUNK, :] for n in range(nchunk)]
        dec = jnp.concatenate([jnp.broadcast_to(d, (HGRN_CHUNK, HGRN_DK)) for d in last], axis=0)
        kinv = k * (1.0 / eb)
        return (q * eb).astype(_bf16), kinv.astype(_bf16), (kinv * dec).astype(_bf16), last

    qkl = [gates(rows, sl) for rows, sl in items]
    yield
    qkd = [decayed(q, k, logf) for q, k, logf in qkl]
    vs = [c_ref[rows, col(2, sl)] for rows, sl in items]
    yield
    att = [[jnp.where(causal_blk, _dot_nt(qe[r0:r0 + blk], ke[r0:r0 + blk]), 0.0).astype(_bf16)
            for r0 in range(0, tc, blk)] for qe, ke, _, _ in qkd]
    yield
    intra = [jnp.concatenate([_dot(a, v[i * blk:(i + 1) * blk]) for i, a in enumerate(att_h)], axis=0)
             for att_h, v in zip(att, vs)]
    yield

    states = [st_ref[h] for h in range(HGRN_HEADS)]
    for s, rows in enumerate(subs):
        per_head = [(qkd[i][0], qkd[i][2], vs[i], qkd[i][3], intra[i])
                    for i in range(s * HGRN_HEADS, (s + 1) * HGRN_HEADS)]
        inter = [[] for _ in heads]
        for n in range(nchunk):
            rs = slice(n * HGRN_CHUNK, (n + 1) * HGRN_CHUNK)
            for h, (qe, kd, v, dec, _) in enumerate(per_head):
                inter[h].append(_dot_nt(qe[rs], states[h].astype(_bf16)))
                states[h] = states[h] * dec[n] + _dot_tn(v[rs], kd[rs])
            yield
        for h, sl in enumerate(heads):
            o = per_head[h][4] + jnp.concatenate(inter[h], axis=0)
            y = _rms(o, gain_ref[...])
            graw = c_ref[rows, col(3, sl)].astype(_f32)
            o_ref[rows, sl] = (y * (graw * _sigmoid(graw))).astype(o_ref.dtype)
        yield
    for h in range(HGRN_HEADS):
        st_ref[h] = states[h]
    yield


MERGE_PIECE = 256


def _merge_kernel(oa_ref, ob_ref, oc_ref, g0_ref, g1_ref, g2_ref, h_ref, wa_ref, wb_ref, wc_ref, wo_ref, gp_ref,
                  out_ref):
    d = wo_ref.shape[0]
    pieces = [slice(c, c + MERGE_PIECE) for c in range(0, d, MERGE_PIECE)]

    def branches(cols):
        return (_dot(oa_ref[...], wa_ref[:, cols]), _dot(ob_ref[...], wb_ref[:, cols]),
                _dot(oc_ref[...], wc_ref[:, cols]))

    def gated(cols, ya, yb, yc):
        sig = lambda ref: _sigmoid(ref[:, cols].astype(_f32))
        return (sig(g0_ref) * ya + sig(g1_ref) * yb + sig(g2_ref) * yc).astype(_bf16)

    y = branches(pieces[0])
    mix = []
    for c, cols in enumerate(pieces):
        y_next = branches(pieces[c + 1]) if c + 1 < len(pieces) else None
        mix.append(gated(cols, *y))
        y = y_next
    out = _dot(jnp.concatenate(mix, axis=1), wo_ref[...])
    out_ref[...] = h_ref[...] + _rms(out, gp_ref[...])


def _merge(oa, ob, oc, nat, h, wa, wb, wc, wo, gp, *, tm=1024):
    t, d = h.shape
    row = lambda a: pl.BlockSpec((tm, a.shape[1]), lambda i: (i, 0))
    full = lambda a: pl.BlockSpec(a.shape, lambda i: (0, 0))
    gate = lambda j: pl.BlockSpec((tm, d), lambda i: (i, NAT_GATES // d + j))
    return pl.pallas_call(
        _merge_kernel,
        out_shape=jax.ShapeDtypeStruct((t, d), _f32),
        grid=(t // tm,),
        in_specs=[row(oa), row(ob), row(oc), gate(0), gate(1), gate(2), row(h),
                  full(wa), full(wb), full(wc), full(wo), full(gp)],
        out_specs=row(h),
        compiler_params=_params("arbitrary"),
        name="merge_out",
    )(oa, ob, oc, nat, nat, nat, h, wa, wb, wc, wo, gp)


HALO = 8
FFN_PIECE = 256

def _shift_rows(u, carry, s):
    r = pltpu.roll(u, s, axis=0)
    c = pltpu.roll(carry, s, axis=0)
    row = lax.broadcasted_iota(jnp.int32, carry.shape, 0)
    head = jnp.where(row < s, c, r[0:HALO])
    return jnp.concatenate([head, r[HALO:]], axis=0)


def _ffn_kernel(x_ref, gpre_ref, wa_ref, wb_ref, cwa_ref, cwb_ref, cba_ref, cbb_ref, wd_ref, gpost_ref,
                out_ref, hn_ref, acc_ref, ca_ref, cb_ref, *, tiles_per_seq):
    i, j = pl.program_id(0), pl.program_id(1)
    tm = x_ref.shape[0]
    tf = wa_ref.shape[1]

    @pl.when(j == 0)
    def _():
        hn_ref[...] = _rms(x_ref[...], gpre_ref[...]).astype(_bf16)
        acc_ref[...] = jnp.zeros_like(acc_ref)

    @pl.when(i == 0)
    def _():
        ca_ref[j] = jnp.zeros(ca_ref.shape[1:], _f32)
        cb_ref[j] = jnp.zeros(cb_ref.shape[1:], _f32)

    seq_start = (i % tiles_per_seq) == 0

    def up(cols):
        return _dot(hn_ref[...], wa_ref[:, cols]), _dot(hn_ref[...], wb_ref[:, cols])

    def conv(u, cols, cw_ref, cbias_ref, carry_ref):
        carry = jnp.where(seq_start, 0.0, carry_ref[j, :, cols])
        carry_ref[j, :, cols] = u[tm - HALO:tm]
        cw = cw_ref[:, cols]
        return (cw[2:3] * u + cw[1:2] * _shift_rows(u, carry, 1) + cw[0:1] * _shift_rows(u, carry, 2)
                + cbias_ref[:, cols])

    def gate(cols, ua, ub):
        a = conv(ua, cols, cwa_ref, cba_ref, ca_ref)
        b = conv(ub, cols, cwb_ref, cbb_ref, cb_ref)
        gelu = 0.5 * a * (1.0 + jnp.tanh(0.7978845608028654 * (a + 0.044715 * (a * a * a))))
        return (gelu * b).astype(_bf16)

    pieces = [slice(c, c + FFN_PIECE) for c in range(0, tf, FFN_PIECE)]
    u = up(pieces[0])
    ts = []
    for c, cols in enumerate(pieces):
        u_next = up(pieces[c + 1]) if c + 1 < len(pieces) else None
        ts.append(gate(cols, *u))
        u = u_next
    acc_ref[...] += _dot(jnp.concatenate(ts, axis=1), wd_ref[...])

    @pl.when(j == pl.num_programs(1) - 1)
    def _():
        out_ref[...] = x_ref[...] + _rms(acc_ref[...], gpost_ref[...])


def _ffn(h, gpre, w_up, conv_w, conv_b, w_down, gpost, *, seq, tm=1024, tf=1024):
    t, d = h.shape
    nf = D_FF // tf
    cb2 = conv_b.reshape(1, 2 * D_FF)
    return pl.pallas_call(
        functools.partial(_ffn_kernel, tiles_per_seq=seq // tm),
        out_shape=jax.ShapeDtypeStruct((t, d), _f32),
        grid=(t // tm, nf),
        in_specs=[
            pl.BlockSpec((tm, d), lambda i, j: (i, 0)),
            pl.BlockSpec((1, d), lambda i, j: (0, 0)),
            pl.BlockSpec((d, tf), lambda i, j: (0, j)),
            pl.BlockSpec((d, tf), lambda i, j: (0, nf + j)),
            pl.BlockSpec((3, tf), lambda i, j: (0, j)),
            pl.BlockSpec((3, tf), lambda i, j: (0, nf + j)),
            pl.BlockSpec((1, tf), lambda i, j: (0, j)),
            pl.BlockSpec((1, tf), lambda i, j: (0, nf + j)),
            pl.BlockSpec((tf, d), lambda i, j: (j, 0)),
            pl.BlockSpec((1, d), lambda i, j: (0, 0)),
        ],
        out_specs=pl.BlockSpec((tm, d), lambda i, j: (i, 0)),
        scratch_shapes=[
            pltpu.VMEM((tm, d), _bf16),
            pltpu.VMEM((tm, d), _f32),
            pltpu.VMEM((nf, HALO, tf), _f32),
            pltpu.VMEM((nf, HALO, tf), _f32),
        ],
        compiler_params=_params("arbitrary", "arbitrary"),
        name="conv_ffn",
    )(h, gpre, w_up, w_up, conv_w, conv_w, cb2, cb2, w_down, gpost)


def kernel(x, norm_pre_mix, norm_post_mix, norm_pre_ffn, norm_post_ffn, w_in, attn_sinks, hgrn_lb_logits,
           hgrn_out_norm, w_branch_a, w_branch_b, w_branch_c, w_out, w_ffn_up, ffn_conv_w, ffn_conv_b,
           w_ffn_down):
    bsz, seq, d = x.shape
    depth = w_in.shape[0]
    h = x.reshape(bsz * seq, d)
    row = lambda a: a.reshape(1, -1)
    for l in range(depth):
        nat, oc, g1, g2 = _in_proj(h, row(norm_pre_mix[l]), _permute_w_in(w_in[l]), hgrn_lb_logits,
                                   row(hgrn_out_norm[l]), layer=l, bsz=bsz, seq=seq)
        nat3 = nat.reshape(bsz, seq, NAT_W)
        oa = _dilated(nat3, g1, g2)
        ob = _swa(nat3, attn_sinks[l])
        h = _merge(oa, ob, oc, nat, h, w_branch_a[l].astype(_bf16), w_branch_b[l].astype(_bf16),
                   w_branch_c[l].astype(_bf16), w_out[l].astype(_bf16), row(norm_post_mix[l]))
        h = _ffn(h, row(norm_pre_ffn[l]), w_ffn_up[l].astype(_bf16), ffn_conv_w[l], ffn_conv_b[l],
                 w_ffn_down[l].astype(_bf16), row(norm_post_ffn[l]), seq=seq)
    return h.reshape(bsz, seq, d)
```

```python
import functools

import jax
import jax.numpy as jnp
from jax import lax
from jax.experimental import pallas as pl
from jax.experimental.pallas import tpu as pltpu

D_MODEL = 1024
HEAD_DIM = 64
DIL_PATTERNS = ((128, 1), (512, 4), (2048, 16))
SWA_WINDOW = 128
SWA_KV_HEADS = 2
HGRN_HEADS = 4
HGRN_DK = 128
HGRN_CHUNK = 32
HGRN_ATT_BLOCK = 128
D_FF = 4 * D_MODEL
BAND = 128
GROUP_W = 256
QKV_W = 3 * GROUP_W
EPS = 1e-6
LANES = 128
NEG = -1e30
ATTN_HEAD_STACK = 2
ATTN_UNROLL = 4

SRC_AQ, SRC_AK, SRC_AV = 0, 768, 1536
SRC_B, SRC_C, SRC_GATES = 2304, 3072, 5120
NAT_GATES = 0
NAT_A0 = 3072
NAT_BQ, NAT_BK, NAT_BV = 3840, 4352, 4480
NAT_W = 4608
HGRN_W = HGRN_HEADS * HGRN_DK
COL_C = NAT_W
COL_G1 = COL_C + 4 * HGRN_W
COL_G2 = COL_G1 + QKV_W
D_IN = COL_G2 + QKV_W

VMEM_LIMIT = 56 * 1024 * 1024

_f32 = jnp.float32
_bf16 = jnp.bfloat16


def _dot(a, b):
    return jnp.dot(a, b, preferred_element_type=_f32)


def _dot_nt(a, b):
    return lax.dot_general(a, b, (((1,), (1,)), ((), ())), preferred_element_type=_f32)


def _dot_tn(a, b):
    return lax.dot_general(a, b, (((0,), (0,)), ((), ())), preferred_element_type=_f32)


def _rms(x, g):
    return x * lax.rsqrt(jnp.mean(x * x, axis=-1, keepdims=True) + EPS) * g


def _sigmoid(x):
    return 1.0 / (1.0 + jnp.exp(-x))


def _params(*sem):
    return pltpu.CompilerParams(dimension_semantics=sem, vmem_limit_bytes=VMEM_LIMIT)


def _permute_w_in(w):
    cols = lambda start, width: w[:, start:start + width]
    grp = lambda gi: [cols(SRC_AQ + gi * GROUP_W, GROUP_W), cols(SRC_AK + gi * GROUP_W, GROUP_W),
                      cols(SRC_AV + gi * GROUP_W, GROUP_W)]
    parts = [cols(SRC_GATES, 3 * D_MODEL)] + grp(0) + [cols(SRC_B, SRC_C - SRC_B), cols(SRC_C, SRC_GATES - SRC_C)]
    parts += grp(1) + grp(2)
    return jnp.concatenate(parts, axis=1).astype(_bf16)


def _in_proj_kernel(x_ref, g_ref, w_ref, lbl_ref, gain_ref, nat_ref, oc_ref, g1_ref, g2_ref,
                    hn_ref, c_ref, st_ref, *, tn, layer, tiles_per_seq):
    tm = x_ref.shape[0]

    @pl.when(pl.program_id(0) % tiles_per_seq == 0)
    def _():
        st_ref[...] = jnp.zeros_like(st_ref)

    hn = _rms(x_ref[...], g_ref[...])
    nlc = hn_ref.shape[0]
    for c in range(nlc):
        hn_ref[c] = hn[:, c * LANES:(c + 1) * LANES]
    hb = hn.astype(_bf16)
    def hgrn_chunk(c):
        c_ref[:, c:c + tn] = _dot(hb, w_ref[:, COL_C + c:COL_C + c + tn]).astype(_bf16)

    def nat_chunk(c):
        nat_ref[:, c:c + tn] = _dot(hb, w_ref[:, c:c + tn]).astype(_bf16)

    def dilated(out_ref, col0):
        d = out_ref.shape[0]
        rows = tm // d
        lhs = jnp.concatenate(
            [jnp.concatenate([hn_ref[c, pl.ds(r, rows, stride=d), :] for c in range(nlc)], axis=1)
             for r in range(d)], axis=0).astype(_bf16)
        y = _dot(lhs, w_ref[:, col0:col0 + QKV_W]).astype(_bf16)
        for r in range(d):
            out_ref[r] = y[r * rows:(r + 1) * rows]

    for c in range(0, 2 * HGRN_W, tn):
        hgrn_chunk(c)
    matmuls = [functools.partial(hgrn_chunk, c) for c in range(2 * HGRN_W, 4 * HGRN_W, tn)]
    matmuls += [functools.partial(nat_chunk, c) for c in range(0, NAT_W, tn)]
    matmuls += [functools.partial(dilated, g1_ref, COL_G1), functools.partial(dilated, g2_ref, COL_G2)]
    hgrn = _hgrn_stages(c_ref, lbl_ref, gain_ref, st_ref, oc_ref, layer=layer)
    per_matmul = -(-_hgrn_stage_count(tm) // len(matmuls))
    for mm in matmuls:
        mm()
        for _ in range(per_matmul):
            next(hgrn, None)
    for _ in hgrn:
        pass


def _in_proj(x, g, w, lb_logits, gain, *, layer, bsz, seq, tm=512, tn=512):
    t, d = x.shape
    tps = seq // tm
    d1, d2 = DIL_PATTERNS[1][1], DIL_PATTERNS[2][1]
    return pl.pallas_call(
        functools.partial(_in_proj_kernel, tn=tn, layer=layer, tiles_per_seq=tps),
        out_shape=(jax.ShapeDtypeStruct((t, NAT_W), _bf16),
                   jax.ShapeDtypeStruct((t, HGRN_W), _bf16),
                   jax.ShapeDtypeStruct((bsz, d1, seq // d1, QKV_W), _bf16),
                   jax.ShapeDtypeStruct((bsz, d2, seq // d2, QKV_W), _bf16)),
        grid=(t // tm,),
        in_specs=[
            pl.BlockSpec((tm, d), lambda i: (i, 0)),
            pl.BlockSpec((1, d), lambda i: (0, 0)),
            pl.BlockSpec((d, D_IN), lambda i: (0, 0), pipeline_mode=pl.Buffered(1)),
            pl.BlockSpec(lb_logits.shape, lambda i: (0, 0)),
            pl.BlockSpec((1, HGRN_DK), lambda i: (0, 0)),
        ],
        out_specs=(
            pl.BlockSpec((tm, NAT_W), lambda i: (i, 0)),
            pl.BlockSpec((tm, HGRN_W), lambda i: (i, 0)),
            pl.BlockSpec((None, d1, tm // d1, QKV_W), lambda i: (i // tps, 0, i % tps, 0)),
            pl.BlockSpec((None, d2, tm // d2, QKV_W), lambda i: (i // tps, 0, i % tps, 0)),
        ),
        scratch_shapes=[pltpu.VMEM((d // LANES, tm, LANES), _f32),
                        pltpu.VMEM((tm, 4 * HGRN_W), _bf16),
                        pltpu.VMEM((HGRN_HEADS, HGRN_DK, HGRN_DK), _f32)],
        compiler_params=_params("arbitrary"),
        name="in_proj",
    )(x, g, w, lb_logits, gain)


def _band_bias(max_dist, nk, first):
    qi = lax.broadcasted_iota(jnp.int32, (BAND, nk), 0)
    ki = lax.broadcasted_iota(jnp.int32, (BAND, nk), 1)
    dist = qi + (nk - BAND) - ki
    ok = (dist >= 0) & (dist <= max_dist)
    if first:
        ok = ok & (ki >= nk - BAND)
    return jnp.where(ok, 0.0, NEG).astype(_f32)


def _band_blocks(blocks, *, sink_of_head, want_lse):
    nh = GROUP_W // HEAD_DIM
    nk = blocks[0][1].shape[0]
    has_prev = nk == 2 * BAND
    lane_head = lax.broadcasted_iota(jnp.int32, (BAND, GROUP_W), 1) // HEAD_DIM
    fold = (lambda x, op: op(x[..., :BAND], x[..., BAND:])) if has_prev else (lambda x, op: x)
    scale = jnp.asarray(HEAD_DIM ** -0.5, _bf16)

    hs = ATTN_HEAD_STACK
    stacks = [tuple(range(h0, h0 + hs)) for h0 in range(0, nh, hs)]
    items = [(b, st) for b in range(len(blocks)) for st in stacks]

    def stack_heads(q, st):
        q = q * scale
        return jnp.concatenate([jnp.where(lane_head == h, q, jnp.zeros_like(q)) for h in st], axis=0)

    def pick_heads(per_head):
        out = jnp.broadcast_to(per_head[0], (BAND, GROUP_W))
        for h in range(1, nh):
            out = jnp.where(lane_head == h, per_head[h], out)
        return out

    def sinks(st):
        head = lax.broadcasted_iota(jnp.int32, (hs, BAND, 1), 0)
        sink = jnp.zeros((hs, BAND, 1), _f32)
        for i, h in enumerate(st):
            sink = jnp.where(head == i, sink_of_head(h), sink)
        return sink

    sink = {st: sinks(st) for st in stacks} if sink_of_head is not None else None

    def scores(qh, kcat, bias, st):
        x = _dot_nt(qh, kcat).reshape(hs, BAND, nk) + bias[None]
        mx = jnp.max(fold(x, jnp.maximum), axis=-1, keepdims=True)
        return x, (mx if sink is None else jnp.maximum(mx, sink[st]))

    def probs(x, mx, st):
        p = jnp.exp(x - mx)
        den = jnp.sum(fold(p, jnp.add), axis=-1, keepdims=True)
        if sink is not None:
            den = den + jnp.exp(sink[st] - mx)
        return p.reshape(hs * BAND, nk).astype(_bf16), den

    qs = [stack_heads(blocks[b][0], st) for b, st in items]
    sm = [scores(qh, blocks[b][1], blocks[b][3], st) for qh, (b, st) in zip(qs, items)]
    pd = [probs(x, mx, st) for (x, mx), (_, st) in zip(sm, items)]
    pv = [_dot(p, blocks[b][2]).reshape(hs, BAND, GROUP_W) * (1.0 / d) for (p, d), (b, _) in zip(pd, items)]
    per_block = lambda vals, b: [vals[b * len(stacks) + h // hs][h % hs] for h in range(nh)]
    o = [pick_heads(per_block(pv, b)) for b in range(len(blocks))]
    if not want_lse:
        return [(x, None) for x in o]
    lse = [mx + jnp.log(d) for (_, mx), (_, d) in zip(sm, pd)]
    return [(x, pick_heads(per_block(lse, b))) for b, x in enumerate(o)]


def _qkv_cols(ref, rows, lead=()):
    idx = lambda c: lead + (rows, slice(c * GROUP_W, (c + 1) * GROUP_W))
    return ref[idx(0)], ref[idx(1)], ref[idx(2)]


def _dil_kernel(a0_ref, g1_ref, g2_ref, oa_ref, o_s, lse_s, bias_s):
    seq = a0_ref.shape[0]
    blks = functools.partial(_band_blocks, sink_of_head=None, want_lse=True)
    halves = [slice(hf * LANES, (hf + 1) * LANES) for hf in range(GROUP_W // LANES)]
    nu = ATTN_UNROLL
    win0, win1, win2 = (w // d for w, d in DIL_PATTERNS)
    d1, d2 = DIL_PATTERNS[1][1], DIL_PATTERNS[2][1]
    assert win0 == win1 and seq // d2 == BAND and win2 >= BAND - 1
    bias_s[0] = _band_bias(win0, 2 * BAND, True)
    bias_s[1] = _band_bias(win0, 2 * BAND, False)
    bias_s[2, :, 0:BAND] = _band_bias(win2, BAND, True)

    def put(g, rows, o, lse):
        for hf, ls in enumerate(halves):
            o_s[g, hf, rows, :] = o[:, ls]
            lse_s[g, hf, rows, :] = lse[:, ls]

    def band(ref, lead, i):
        cur = pl.ds(pl.multiple_of(i * BAND, BAND), BAND)
        prev = pl.ds(pl.multiple_of(jnp.maximum(i - 1, 0) * BAND, BAND), BAND)
        q, kc, vc = _qkv_cols(ref, cur, lead)
        _, kp, vp = _qkv_cols(ref, prev, lead)
        return (q, jnp.concatenate([kp, kc], axis=0), jnp.concatenate([vp, vc], axis=0),
                bias_s[jnp.minimum(i, 1)])

    def g0_body(n, c):
        ids = [n * nu + u for u in range(nu)]
        for i, (o, lse) in zip(ids, blks([band(a0_ref, (), i) for i in ids])):
            put(0, pl.ds(pl.multiple_of(i * BAND, BAND), BAND), o, lse)
        return c

    lax.fori_loop(0, seq // BAND // nu, g0_body, 0)

    nb1 = seq // d1 // BAND

    def g1_body(n, c):
        ids = [((n * nu + u) // nb1, (n * nu + u) % nb1) for u in range(nu)]
        for (r, i), (o, lse) in zip(ids, blks([band(g1_ref, (r,), i) for r, i in ids])):
            put(1, pl.ds(i * (BAND * d1) + r, BAND, stride=d1), o, lse)
        return c

    lax.fori_loop(0, d1 * nb1 // nu, g1_body, 0)

    def g2_body(n, c):
        ids = [n * nu + u for u in range(nu)]
        outs = blks([_qkv_cols(g2_ref, slice(None), (r,)) + (bias_s[2, :, 0:BAND],) for r in ids])
        for r, (o, lse) in zip(ids, outs):
            put(2, pl.ds(r, BAND, stride=d2), o, lse)
        return c

    lax.fori_loop(0, d2 // nu, g2_body, 0)

    def mix_body(i, c):
        rows = pl.ds(pl.multiple_of(i * BAND, BAND), BAND)
        for hf, ls in enumerate(halves):
            l0, l1, l2 = lse_s[0, hf, rows, :], lse_s[1, hf, rows, :], lse_s[2, hf, rows, :]
            m = jnp.maximum(jnp.maximum(l0, l1), l2)
            e0, e1, e2 = jnp.exp(l0 - m), jnp.exp(l1 - m), jnp.exp(l2 - m)
            oa = (e0 * o_s[0, hf, rows, :] + e1 * o_s[1, hf, rows, :] + e2 * o_s[2, hf, rows, :]) / (e0 + e1 + e2)
            oa_ref[rows, ls] = oa.astype(oa_ref.dtype)
        return c

    lax.fori_loop(0, seq // BAND, mix_body, 0)


def _dilated(nat3, g1, g2):
    bsz, seq, _ = nat3.shape
    n_grp = len(DIL_PATTERNS)
    return pl.pallas_call(
        _dil_kernel,
        out_shape=jax.ShapeDtypeStruct((bsz, seq, GROUP_W), _bf16),
        grid=(bsz,),
        in_specs=[
            pl.BlockSpec((None, seq, QKV_W), lambda b: (b, 0, NAT_A0 // QKV_W)),
            pl.BlockSpec((None,) + g1.shape[1:], lambda b: (b, 0, 0, 0)),
            pl.BlockSpec((None,) + g2.shape[1:], lambda b: (b, 0, 0, 0)),
        ],
        out_specs=pl.BlockSpec((None, seq, GROUP_W), lambda b: (b, 0, 0)),
        scratch_shapes=[pltpu.VMEM((n_grp, GROUP_W // LANES, seq, LANES), _f32)] * 2
        + [pltpu.VMEM((3, BAND, 2 * BAND), _f32)],
        compiler_params=_params("arbitrary"),
        name="dil_attn",
    )(nat3, g1, g2).reshape(bsz * seq, GROUP_W)


def _swa_kernel(sink_ref, q_ref, k_ref, v_ref, o_ref, kx_ref, vx_ref, bias_s):
    seq = q_ref.shape[0]
    nb = seq // BAND
    kvh = pl.program_id(1)
    bias_s[0] = _band_bias(SWA_WINDOW - 1, 2 * BAND, True)
    bias_s[1] = _band_bias(SWA_WINDOW - 1, 2 * BAND, False)
    lane = lax.broadcasted_iota(jnp.int32, (BAND, 2 * HEAD_DIM), 1)
    keep = (lane < HEAD_DIM) == (kvh == 0)

    def expand(i, c):
        r0 = pl.multiple_of(i * BAND, BAND)
        for src, dst in ((k_ref, kx_ref), (v_ref, vx_ref)):
            x = src[pl.ds(r0, BAND), :].astype(_f32)
            x2 = jnp.where(keep, x, pltpu.roll(x, HEAD_DIM, axis=1)).astype(_bf16)
            dst[pl.ds(r0, BAND), :] = jnp.concatenate([x2, x2], axis=1)
        return c

    lax.fori_loop(0, nb, expand, 0)
    heads_per_kv = GROUP_W // HEAD_DIM

    def band(i):
        cur = pl.ds(pl.multiple_of(i * BAND, BAND), BAND)
        prev = pl.ds(pl.multiple_of(jnp.maximum(i - 1, 0) * BAND, BAND), BAND)
        return (q_ref[cur, :], jnp.concatenate([kx_ref[prev, :], kx_ref[cur, :]], axis=0),
                jnp.concatenate([vx_ref[prev, :], vx_ref[cur, :]], axis=0), bias_s[jnp.minimum(i, 1)])

    def body(n, c):
        ids = [n * ATTN_UNROLL + u for u in range(ATTN_UNROLL)]
        outs = _band_blocks([band(i) for i in ids], want_lse=False,
                            sink_of_head=lambda h: sink_ref[kvh * heads_per_kv + h])
        for i, (o, _) in zip(ids, outs):
            o_ref[pl.ds(pl.multiple_of(i * BAND, BAND), BAND), :] = o.astype(o_ref.dtype)
        return c

    lax.fori_loop(0, nb // ATTN_UNROLL, body, 0)


def _swa(nat3, sinks):
    bsz, seq, _ = nat3.shape
    kvw = SWA_KV_HEADS * HEAD_DIM
    return pl.pallas_call(
        _swa_kernel,
        out_shape=jax.ShapeDtypeStruct((bsz, seq, SWA_KV_HEADS * GROUP_W), _bf16),
        grid=(bsz, SWA_KV_HEADS),
        in_specs=[
            pl.BlockSpec(memory_space=pltpu.SMEM),
            pl.BlockSpec((None, seq, GROUP_W), lambda b, g: (b, 0, NAT_BQ // GROUP_W + g)),
            pl.BlockSpec((None, seq, kvw), lambda b, g: (b, 0, NAT_BK // kvw)),
            pl.BlockSpec((None, seq, kvw), lambda b, g: (b, 0, NAT_BV // kvw)),
        ],
        out_specs=pl.BlockSpec((None, seq, GROUP_W), lambda b, g: (b, 0, g)),
        scratch_shapes=[pltpu.VMEM((seq, GROUP_W), _bf16), pltpu.VMEM((seq, GROUP_W), _bf16),
                        pltpu.VMEM((2, BAND, 2 * BAND), _f32)],
        compiler_params=_params("arbitrary", "arbitrary"),
        name="swa_attn",
    )(sinks, nat3, nat3, nat3).reshape(bsz * seq, SWA_KV_HEADS * GROUP_W)


HGRN_TILE = 256


def _hgrn_stage_count(tm):
    return 4 + (HGRN_TILE // HGRN_CHUNK + 1) * (tm // HGRN_TILE) + 1


def _hgrn_stages(c_ref, lbl_ref, gain_ref, st_ref, o_ref, *, layer):
    tm = c_ref.shape[0]
    tc = HGRN_TILE
    col = lambda k, sl: slice(k * HGRN_W + sl.start, k * HGRN_W + sl.stop)

    logits = lbl_ref[...]
    e = jnp.exp(logits - jnp.max(logits, axis=0, keepdims=True))
    sm = e / jnp.sum(e, axis=0, keepdims=True)
    lb = jnp.zeros((1, sm.shape[1]), _f32)
    for j in range(1, layer + 1):
        lb = lb + sm[j:j + 1, :]
    one_m_lb = 1.0 - lb

    nchunk = tc // HGRN_CHUNK
    row_in_chunk = lax.broadcasted_iota(jnp.int32, (tc, HGRN_DK), 0) & (HGRN_CHUNK - 1)
    scan_steps = [(s, row_in_chunk >= s) for s in (1, 2, 4, 8, 16) if s < HGRN_CHUNK]

    def chunk_prefix(x):
        for s, keep in scan_steps:
            x = x + jnp.where(keep, pltpu.roll(x, s, axis=0), 0.0)
        return x

    blk = HGRN_ATT_BLOCK
    tb = lax.broadcasted_iota(jnp.int32, (blk, blk), 0)
    sb = lax.broadcasted_iota(jnp.int32, (blk, blk), 1)
    causal_blk = jnp.logical_and((tb // HGRN_CHUNK) == (sb // HGRN_CHUNK), sb <= tb)

    heads = [slice(h * HGRN_DK, (h + 1) * HGRN_DK) for h in range(HGRN_HEADS)]
    subs = [slice(r0, r0 + tc) for r0 in range(0, tm, tc)]
    items = [(rows, sl) for rows in subs for sl in heads]
    def gates(rows, sl):
        qraw = c_ref[rows, col(0, sl)].astype(_f32)
        fz = c_ref[rows, col(1, sl)].astype(_f32)
        t = jnp.exp(-jnp.abs(fz))
        r = 1.0 / (1.0 + t)
        tr = t * r
        pos = fz >= 0.0
        logf = jnp.log(lb[:, sl] + one_m_lb[:, sl] * jnp.where(pos, r, tr))
        k = one_m_lb[:, sl] * jnp.where(pos, tr, r)
        return qraw * _sigmoid(qraw), k, logf

    def decayed(q, k, logf):
        eb = jnp.exp(chunk_prefix(logf))
        last = [eb[n * HGRN_CHUNK + HGRN_CHUNK - 1:(n + 1) * HGRN_CHUNK, :] for n in range(nchunk)]
        dec = jnp.concatenate([jnp.broadcast_to(d, (HGRN_CHUNK, HGRN_DK)) for d in last], axis=0)
        kinv = k * (1.0 / eb)
        return (q * eb).astype(_bf16), kinv.astype(_bf16), (kinv * dec).astype(_bf16), last

    qkl = [gates(rows, sl) for rows, sl in items]
    yield
    qkd = [decayed(q, k, logf) for q, k, logf in qkl]
    vs = [c_ref[rows, col(2, sl)] for rows, sl in items]
    yield
    att = [[jnp.where(causal_blk, _dot_nt(qe[r0:r0 + blk], ke[r0:r0 + blk]), 0.0).astype(_bf16)
            for r0 in range(0, tc, blk)] for qe, ke, _, _ in qkd]
    yield
    intra = [jnp.concatenate([_dot(a, v[i * blk:(i + 1) * blk]) for i, a in enumerate(att_h)], axis=0)
             for att_h, v in zip(att, vs)]
    yield

    states = [st_ref[h] for h in range(HGRN_HEADS)]
    for s, rows in enumerate(subs):
        per_head = [(qkd[i][0], qkd[i][2], vs[i], qkd[i][3], intra[i])
                    for i in range(s * HGRN_HEADS, (s + 1) * HGRN_HEADS)]
        inter = [[] for _ in heads]
        for n in range(nchunk):
            rs = slice(n * HGRN_CHUNK, (n + 1) * HGRN_CHUNK)
            for h, (qe, kd, v, dec, _) in enumerate(per_head):
                inter[h].append(_dot_nt(qe[rs], states[h].astype(_bf16)))
                states[h] = states[h] * dec[n] + _dot_tn(v[rs], kd[rs])
            yield
        for h, sl in enumerate(heads):
            o = per_head[h][4] + jnp.concatenate(inter[h], axis=0)
            y = _rms(o, gain_ref[...])
            graw = c_ref[rows, col(3, sl)].astype(_f32)
            o_ref[rows, sl] = (y * (graw * _sigmoid(graw))).astype(o_ref.dtype)
        yield
    for h in range(HGRN_HEADS):
        st_ref[h] = states[h]
    yield


MERGE_PIECE = 256


def _merge_kernel(oa_ref, ob_ref, oc_ref, g0_ref, g1_ref, g2_ref, h_ref, wa_ref, wb_ref, wc_ref, wo_ref, gp_ref,
                  out_ref):
    d = wo_ref.shape[0]
    pieces = [slice(c, c + MERGE_PIECE) for c in range(0, d, MERGE_PIECE)]

    def branches(cols):
        return (_dot(oa_ref[...], wa_ref[:, cols]), _dot(ob_ref[...], wb_ref[:, cols]),
                _dot(oc_ref[...], wc_ref[:, cols]))

    def gated(cols, ya, yb, yc):
        sig = lambda ref: _sigmoid(ref[:, cols].astype(_f32))
        return (sig(g0_ref) * ya + sig(g1_ref) * yb + sig(g2_ref) * yc).astype(_bf16)

    y = branches(pieces[0])
    mix = []
    for c, cols in enumerate(pieces):
        y_next = branches(pieces[c + 1]) if c + 1 < len(pieces) else None
        mix.append(gated(cols, *y))
        y = y_next
    out = _dot(jnp.concatenate(mix, axis=1), wo_ref[...])
    out_ref[...] = h_ref[...] + _rms(out, gp_ref[...])


def _merge(oa, ob, oc, nat, h, wa, wb, wc, wo, gp, *, tm=1024):
    t, d = h.shape
    row = lambda a: pl.BlockSpec((tm, a.shape[1]), lambda i: (i, 0))
    full = lambda a: pl.BlockSpec(a.shape, lambda i: (0, 0))
    gate = lambda j: pl.BlockSpec((tm, d), lambda i: (i, NAT_GATES // d + j))
    return pl.pallas_call(
        _merge_kernel,
        out_shape=jax.ShapeDtypeStruct((t, d), _f32),
        grid=(t // tm,),
        in_specs=[row(oa), row(ob), row(oc), gate(0), gate(1), gate(2), row(h),
                  full(wa), full(wb), full(wc), full(wo), full(gp)],
        out_specs=row(h),
        compiler_params=_params("arbitrary"),
        name="merge_out",
    )(oa, ob, oc, nat, nat, nat, h, wa, wb, wc, wo, gp)


HALO = 8
FFN_PIECE = 256

def _shift_rows(u, carry, s):
    r = pltpu.roll(u, s, axis=0)
    c = pltpu.roll(carry, s, axis=0)
    row = lax.broadcasted_iota(jnp.int32, carry.shape, 0)
    head = jnp.where(row < s, c, r[0:HALO])
    return jnp.concatenate([head, r[HALO:]], axis=0)


def _ffn_kernel(x_ref, gpre_ref, wa_ref, wb_ref, cwa_ref, cwb_ref, cba_ref, cbb_ref, wd_ref, gpost_ref,
                out_ref, hn_ref, acc_ref, ca_ref, cb_ref, *, tiles_per_seq):
    i, j = pl.program_id(0), pl.program_id(1)
    tm = x_ref.shape[0]
    tf = wa_ref.shape[1]

    @pl.when(j == 0)
    def _():
        hn_ref[...] = _rms(x_ref[...], gpre_ref[...]).astype(_bf16)
        acc_ref[...] = jnp.zeros_like(acc_ref)

    @pl.when(i == 0)
    def _():
        ca_ref[j] = jnp.zeros(ca_ref.shape[1:], _f32)
        cb_ref[j] = jnp.zeros(cb_ref.shape[1:], _f32)

    seq_start = (i % tiles_per_seq) == 0

    def up(cols):
        return _dot(hn_ref[...], wa_ref[:, cols]), _dot(hn_ref[...], wb_ref[:, cols])

    def conv(u, cols, cw_ref, cbias_ref, carry_ref):
        carry = jnp.where(seq_start, 0.0, carry_ref[j, :, cols])
        carry_ref[j, :, cols] = u[tm - HALO:tm]
        cw = cw_ref[:, cols]
        return (cw[2:3] * u + cw[1:2] * _shift_rows(u, carry, 1) + cw[0:1] * _shift_rows(u, carry, 2)
                + cbias_ref[:, cols])

    def gate(cols, ua, ub):
        a = conv(ua, cols, cwa_ref, cba_ref, ca_ref)
        b = conv(ub, cols, cwb_ref, cbb_ref, cb_ref)
        gelu = 0.5 * a * (1.0 + jnp.tanh(0.7978845608028654 * (a + 0.044715 * (a * a * a))))
        return (gelu * b).astype(_bf16)

    pieces = [slice(c, c + FFN_PIECE) for c in range(0, tf, FFN_PIECE)]
    u = up(pieces[0])
    ts = []
    for c, cols in enumerate(pieces):
        u_next = up(pieces[c + 1]) if c + 1 < len(pieces) else None
        ts.append(gate(cols, *u))
        u = u_next
    acc_ref[...] += _dot(jnp.concatenate(ts, axis=1), wd_ref[...])

    @pl.when(j == pl.num_programs(1) - 1)
    def _():
        out_ref[...] = x_ref[...] + _rms(acc_ref[...], gpost_ref[...])


def _ffn(h, gpre, w_up, conv_w, conv_b, w_down, gpost, *, seq, tm=1024, tf=1024):
    t, d = h.shape
    nf = D_FF // tf
    cb2 = conv_b.reshape(1, 2 * D_FF)
    return pl.pallas_call(
        functools.partial(_ffn_kernel, tiles_per_seq=seq // tm),
        out_shape=jax.ShapeDtypeStruct((t, d), _f32),
        grid=(t // tm, nf),
        in_specs=[
            pl.BlockSpec((tm, d), lambda i, j: (i, 0)),
            pl.BlockSpec((1, d), lambda i, j: (0, 0)),
            pl.BlockSpec((d, tf), lambda i, j: (0, j)),
            pl.BlockSpec((d, tf), lambda i, j: (0, nf + j)),
            pl.BlockSpec((3, tf), lambda i, j: (0, j)),
            pl.BlockSpec((3, tf), lambda i, j: (0, nf + j)),
            pl.BlockSpec((1, tf), lambda i, j: (0, j)),
            pl.BlockSpec((1, tf), lambda i, j: (0, nf + j)),
            pl.BlockSpec((tf, d), lambda i, j: (j, 0)),
            pl.BlockSpec((1, d), lambda i, j: (0, 0)),
        ],
        out_specs=pl.BlockSpec((tm, d), lambda i, j: (i, 0)),
        scratch_shapes=[
            pltpu.VMEM((tm, d), _bf16),
            pltpu.VMEM((tm, d), _f32),
            pltpu.VMEM((nf, HALO, tf), _f32),
            pltpu.VMEM((nf, HALO, tf), _f32),
        ],
        compiler_params=_params("arbitrary", "arbitrary"),
        name="conv_ffn",
    )(h, gpre, w_up, w_up, conv_w, conv_w, cb2, cb2, w_down, gpost)


def kernel(x, norm_pre_mix, norm_post_mix, norm_pre_ffn, norm_post_ffn, w_in, attn_sinks, hgrn_lb_logits,
           hgrn_out_norm, w_branch_a, w_branch_b, w_branch_c, w_out, w_ffn_up, ffn_conv_w, ffn_conv_b,
           w_ffn_down):
    bsz, seq, d = x.shape
    depth = w_in.shape[0]
    h = x.reshape(bsz * seq, d)
    row = lambda a: a.reshape(1, -1)
    for l in range(depth):
        nat, oc, g1, g2 = _in_proj(h, row(norm_pre_mix[l]), _permute_w_in(w_in[l]), hgrn_lb_logits,
                                   row(hgrn_out_norm[l]), layer=l, bsz=bsz, seq=seq)
        nat3 = nat.reshape(bsz, seq, NAT_W)
        oa = _dilated(nat3, g1, g2)
        ob = _swa(nat3, attn_sinks[l])
        h = _merge(oa, ob, oc, nat, h, w_branch_a[l].astype(_bf16), w_branch_b[l].astype(_bf16),
                   w_branch_c[l].astype(_bf16), w_out[l].astype(_bf16), row(norm_post_mix[l]))
        h = _ffn(h, row(norm_pre_ffn[l]), w_ffn_up[l].astype(_bf16), ffn_conv_w[l], ffn_conv_b[l],
                 w_ffn_down[l].astype(_bf16), row(norm_post_ffn[l]), seq=seq)
    return h.reshape(bsz, seq, d)
```

```python
import functools

import jax
import jax.numpy as jnp
from jax import lax
from jax.experimental import pallas as pl
from jax.experimental.pallas import tpu as pltpu

D_MODEL = 1024
HEAD_DIM = 64
DIL_PATTERNS = ((128, 1), (512, 4), (2048, 16))
SWA_WINDOW = 128
SWA_KV_HEADS = 2
HGRN_HEADS = 4
HGRN_DK = 128
HGRN_CHUNK = 32
HGRN_ATT_BLOCK = 128
D_FF = 4 * D_MODEL
BAND = 128
GROUP_W = 256
QKV_W = 3 * GROUP_W
EPS = 1e-6
LANES = 128
NEG = -1e30
ATTN_UNROLL = 4

SRC_AQ, SRC_AK, SRC_AV = 0, 768, 1536
SRC_B, SRC_C, SRC_GATES = 2304, 3072, 5120
NAT_GATES = 0
NAT_A0 = 3072
NAT_BQ, NAT_BK, NAT_BV = 3840, 4352, 4480
NAT_W = 4608
HGRN_W = HGRN_HEADS * HGRN_DK
COL_C = NAT_W
COL_G1 = COL_C + 4 * HGRN_W
COL_G2 = COL_G1 + QKV_W
D_IN = COL_G2 + QKV_W

VMEM_LIMIT = 56 * 1024 * 1024

_f32 = jnp.float32
_bf16 = jnp.bfloat16


def _dot(a, b):
    return jnp.dot(a, b, preferred_element_type=_f32)


def _dot_nt(a, b):
    return lax.dot_general(a, b, (((1,), (1,)), ((), ())), preferred_element_type=_f32)


def _dot_tn(a, b):
    return lax.dot_general(a, b, (((0,), (0,)), ((), ())), preferred_element_type=_f32)


def _rms(x, g):
    return x * lax.rsqrt(jnp.mean(x * x, axis=-1, keepdims=True) + EPS) * g


def _sigmoid(x):
    return 1.0 / (1.0 + jnp.exp(-x))


def _params(*sem):
    return pltpu.CompilerParams(dimension_semantics=sem, vmem_limit_bytes=VMEM_LIMIT)


def _permute_w_in(w):
    cols = lambda start, width: w[:, start:start + width]
    grp = lambda gi: [cols(SRC_AQ + gi * GROUP_W, GROUP_W), cols(SRC_AK + gi * GROUP_W, GROUP_W),
                      cols(SRC_AV + gi * GROUP_W, GROUP_W)]
    parts = [cols(SRC_GATES, 3 * D_MODEL)] + grp(0) + [cols(SRC_B, SRC_C - SRC_B), cols(SRC_C, SRC_GATES - SRC_C)]
    parts += grp(1) + grp(2)
    return jnp.concatenate(parts, axis=1).astype(_bf16)


def _in_proj_kernel(x_ref, g_ref, w_ref, lbl_ref, gain_ref, nat_ref, oc_ref, g1_ref, g2_ref,
                    hn_ref, c_ref, st_ref, *, tn, layer, tiles_per_seq):
    tm = x_ref.shape[0]

    @pl.when(pl.program_id(0) % tiles_per_seq == 0)
    def _():
        st_ref[...] = jnp.zeros_like(st_ref)

    hn = _rms(x_ref[...], g_ref[...])
    nlc = hn_ref.shape[0]
    for c in range(nlc):
        hn_ref[c] = hn[:, c * LANES:(c + 1) * LANES]
    hb = hn.astype(_bf16)
    th = 2 * HGRN_W

    def hgrn_chunk(c):
        c_ref[:, c:c + th] = _dot(hb, w_ref[:, COL_C + c:COL_C + c + th]).astype(_bf16)

    def nat_chunk(c):
        nat_ref[:, c:c + tn] = _dot(hb, w_ref[:, c:c + tn]).astype(_bf16)

    def dilated(out_ref, col0):
        d = out_ref.shape[0]
        rows = tm // d
        lhs = jnp.concatenate(
            [jnp.concatenate([hn_ref[c, pl.ds(r, rows, stride=d), :] for c in range(nlc)], axis=1)
             for r in range(d)], axis=0).astype(_bf16)
        y = _dot(lhs, w_ref[:, col0:col0 + QKV_W]).astype(_bf16)
        for r in range(d):
            out_ref[r] = y[r * rows:(r + 1) * rows]

    hgrn_chunk(0)
    matmuls = [functools.partial(hgrn_chunk, th)]
    matmuls += [functools.partial(nat_chunk, c) for c in range(0, NAT_W, tn)]
    matmuls += [functools.partial(dilated, g1_ref, COL_G1), functools.partial(dilated, g2_ref, COL_G2)]
    hgrn = _hgrn_stages(c_ref, lbl_ref, gain_ref, st_ref, oc_ref, layer=layer)
    per_matmul = -(-_hgrn_stage_count(tm) // len(matmuls))
    for mm in matmuls:
        mm()
        for _ in range(per_matmul):
            next(hgrn, None)
    for _ in hgrn:
        pass


def _in_proj(x, g, w, lb_logits, gain, *, layer, bsz, seq, tm=512, tn=768):
    t, d = x.shape
    tps = seq // tm
    d1, d2 = DIL_PATTERNS[1][1], DIL_PATTERNS[2][1]
    return pl.pallas_call(
        functools.partial(_in_proj_kernel, tn=tn, layer=layer, tiles_per_seq=tps),
        out_shape=(jax.ShapeDtypeStruct((t, NAT_W), _bf16),
                   jax.ShapeDtypeStruct((t, HGRN_W), _bf16),
                   jax.ShapeDtypeStruct((bsz, d1, seq // d1, QKV_W), _bf16),
                   jax.ShapeDtypeStruct((bsz, d2, seq // d2, QKV_W), _bf16)),
        grid=(t // tm,),
        in_specs=[
            pl.BlockSpec((tm, d), lambda i: (i, 0)),
            pl.BlockSpec((1, d), lambda i: (0, 0)),
            pl.BlockSpec((d, D_IN), lambda i: (0, 0), pipeline_mode=pl.Buffered(1)),
            pl.BlockSpec(lb_logits.shape, lambda i: (0, 0)),
            pl.BlockSpec((1, HGRN_DK), lambda i: (0, 0)),
        ],
        out_specs=(
            pl.BlockSpec((tm, NAT_W), lambda i: (i, 0)),
            pl.BlockSpec((tm, HGRN_W), lambda i: (i, 0)),
            pl.BlockSpec((None, d1, tm // d1, QKV_W), lambda i: (i // tps, 0, i % tps, 0)),
            pl.BlockSpec((None, d2, tm // d2, QKV_W), lambda i: (i // tps, 0, i % tps, 0)),
        ),
        scratch_shapes=[pltpu.VMEM((d // LANES, tm, LANES), _f32),
                        pltpu.VMEM((tm, 4 * HGRN_W), _bf16),
                        pltpu.VMEM((HGRN_HEADS, HGRN_DK, HGRN_DK), _f32)],
        compiler_params=_params("arbitrary"),
        name="in_proj",
    )(x, g, w, lb_logits, gain)


def _band_bias(max_dist, nk, first):
    qi = lax.broadcasted_iota(jnp.int32, (BAND, nk), 0)
    ki = lax.broadcasted_iota(jnp.int32, (BAND, nk), 1)
    dist = qi + (nk - BAND) - ki
    ok = (dist >= 0) & (dist <= max_dist)
    if first:
        ok = ok & (ki >= nk - BAND)
    return jnp.where(ok, 0.0, NEG).astype(_f32)


def _band_blocks(blocks, *, sink_of_head, want_lse):
    nh = GROUP_W // HEAD_DIM
    nk = blocks[0][1].shape[0]
    has_prev = nk == 2 * BAND
    lane_head = lax.broadcasted_iota(jnp.int32, (BAND, GROUP_W), 1) // HEAD_DIM
    fold = (lambda x, op: op(x[..., :BAND], x[..., BAND:])) if has_prev else (lambda x, op: x)
    scale = jnp.asarray(HEAD_DIM ** -0.5, _bf16)

    def stack_heads(q):
        q = q * scale
        return jnp.concatenate([jnp.where(lane_head == h, q, jnp.zeros_like(q)) for h in range(nh)], axis=0)

    def pick_heads(x):
        out = jnp.broadcast_to(x[0], (BAND, GROUP_W))
        for h in range(1, nh):
            out = jnp.where(lane_head == h, x[h], out)
        return out

    if sink_of_head is not None:
        head = lax.broadcasted_iota(jnp.int32, (nh, BAND, 1), 0)
        sink = jnp.zeros((nh, BAND, 1), _f32)
        for h in range(nh):
            sink = jnp.where(head == h, sink_of_head(h), sink)

    def scores(qh, kcat, bias):
        x = _dot_nt(qh, kcat).reshape(nh, BAND, nk) + bias[None]
        mx = jnp.max(fold(x, jnp.maximum), axis=-1, keepdims=True)
        return x, (mx if sink_of_head is None else jnp.maximum(mx, sink))

    def probs(x, mx):
        p = jnp.exp(x - mx)
        den = jnp.sum(fold(p, jnp.add), axis=-1, keepdims=True)
        if sink_of_head is not None:
            den = den + jnp.exp(sink - mx)
        return p.reshape(nh * BAND, nk).astype(_bf16), den

    qs = [stack_heads(q) for q, _, _, _ in blocks]
    sm = [scores(qh, kcat, bias) for qh, (_, kcat, _, bias) in zip(qs, blocks)]
    pd = [probs(x, mx) for x, mx in sm]
    pv = [_dot(p, vcat).reshape(nh, BAND, GROUP_W) for (p, _), (_, _, vcat, _) in zip(pd, blocks)]
    o = [pick_heads(x * (1.0 / d)) for x, (_, d) in zip(pv, pd)]
    if not want_lse:
        return [(x, None) for x in o]
    return [(x, pick_heads(mx + jnp.log(d))) for x, (_, mx), (_, d) in zip(o, sm, pd)]


def _qkv_cols(ref, rows, lead=()):
    idx = lambda c: lead + (rows, slice(c * GROUP_W, (c + 1) * GROUP_W))
    return ref[idx(0)], ref[idx(1)], ref[idx(2)]


def _dil_kernel(a0_ref, g1_ref, g2_ref, oa_ref, o_s, lse_s, bias_s):
    seq = a0_ref.shape[0]
    blks = functools.partial(_band_blocks, sink_of_head=None, want_lse=True)
    halves = [slice(hf * LANES, (hf + 1) * LANES) for hf in range(GROUP_W // LANES)]
    nu = ATTN_UNROLL
    win0, win1, win2 = (w // d for w, d in DIL_PATTERNS)
    d1, d2 = DIL_PATTERNS[1][1], DIL_PATTERNS[2][1]
    assert win0 == win1 and seq // d2 == BAND and win2 >= BAND - 1
    bias_s[0] = _band_bias(win0, 2 * BAND, True)
    bias_s[1] = _band_bias(win0, 2 * BAND, False)
    bias_s[2, :, 0:BAND] = _band_bias(win2, BAND, True)

    def put(g, rows, o, lse):
        for hf, ls in enumerate(halves):
            o_s[g, hf, rows, :] = o[:, ls]
            lse_s[g, hf, rows, :] = lse[:, ls]

    def band(ref, lead, i):
        cur = pl.ds(pl.multiple_of(i * BAND, BAND), BAND)
        prev = pl.ds(pl.multiple_of(jnp.maximum(i - 1, 0) * BAND, BAND), BAND)
        q, kc, vc = _qkv_cols(ref, cur, lead)
        _, kp, vp = _qkv_cols(ref, prev, lead)
        return (q, jnp.concatenate([kp, kc], axis=0), jnp.concatenate([vp, vc], axis=0),
                bias_s[jnp.minimum(i, 1)])

    def g0_body(n, c):
        ids = [n * nu + u for u in range(nu)]
        for i, (o, lse) in zip(ids, blks([band(a0_ref, (), i) for i in ids])):
            put(0, pl.ds(pl.multiple_of(i * BAND, BAND), BAND), o, lse)
        return c

    lax.fori_loop(0, seq // BAND // nu, g0_body, 0)

    nb1 = seq // d1 // BAND

    def g1_body(n, c):
        ids = [((n * nu + u) // nb1, (n * nu + u) % nb1) for u in range(nu)]
        for (r, i), (o, lse) in zip(ids, blks([band(g1_ref, (r,), i) for r, i in ids])):
            put(1, pl.ds(i * (BAND * d1) + r, BAND, stride=d1), o, lse)
        return c

    lax.fori_loop(0, d1 * nb1 // nu, g1_body, 0)

    def g2_body(n, c):
        ids = [n * nu + u for u in range(nu)]
        outs = blks([_qkv_cols(g2_ref, slice(None), (r,)) + (bias_s[2, :, 0:BAND],) for r in ids])
        for r, (o, lse) in zip(ids, outs):
            put(2, pl.ds(r, BAND, stride=d2), o, lse)
        return c

    lax.fori_loop(0, d2 // nu, g2_body, 0)

    def mix_body(i, c):
        rows = pl.ds(pl.multiple_of(i * BAND, BAND), BAND)
        for hf, ls in enumerate(halves):
            l0, l1, l2 = lse_s[0, hf, rows, :], lse_s[1, hf, rows, :], lse_s[2, hf, rows, :]
            m = jnp.maximum(jnp.maximum(l0, l1), l2)
            e0, e1, e2 = jnp.exp(l0 - m), jnp.exp(l1 - m), jnp.exp(l2 - m)
            oa = (e0 * o_s[0, hf, rows, :] + e1 * o_s[1, hf, rows, :] + e2 * o_s[2, hf, rows, :]) / (e0 + e1 + e2)
            oa_ref[rows, ls] = oa.astype(oa_ref.dtype)
        return c

    lax.fori_loop(0, seq // BAND, mix_body, 0)


def _dilated(nat3, g1, g2):
    bsz, seq, _ = nat3.shape
    n_grp = len(DIL_PATTERNS)
    return pl.pallas_call(
        _dil_kernel,
        out_shape=jax.ShapeDtypeStruct((bsz, seq, GROUP_W), _bf16),
        grid=(bsz,),
        in_specs=[
            pl.BlockSpec((None, seq, QKV_W), lambda b: (b, 0, NAT_A0 // QKV_W)),
            pl.BlockSpec((None,) + g1.shape[1:], lambda b: (b, 0, 0, 0)),
            pl.BlockSpec((None,) + g2.shape[1:], lambda b: (b, 0, 0, 0)),
        ],
        out_specs=pl.BlockSpec((None, seq, GROUP_W), lambda b: (b, 0, 0)),
        scratch_shapes=[pltpu.VMEM((n_grp, GROUP_W // LANES, seq, LANES), _f32)] * 2
        + [pltpu.VMEM((3, BAND, 2 * BAND), _f32)],
        compiler_params=_params("arbitrary"),
        name="dil_attn",
    )(nat3, g1, g2).reshape(bsz * seq, GROUP_W)


def _swa_kernel(sink_ref, q_ref, k_ref, v_ref, o_ref, kx_ref, vx_ref, bias_s):
    seq = q_ref.shape[0]
    nb = seq // BAND
    kvh = pl.program_id(1)
    bias_s[0] = _band_bias(SWA_WINDOW - 1, 2 * BAND, True)
    bias_s[1] = _band_bias(SWA_WINDOW - 1, 2 * BAND, False)
    lane = lax.broadcasted_iota(jnp.int32, (BAND, 2 * HEAD_DIM), 1)
    keep = (lane < HEAD_DIM) == (kvh == 0)

    def expand(i, c):
        r0 = pl.multiple_of(i * BAND, BAND)
        for src, dst in ((k_ref, kx_ref), (v_ref, vx_ref)):
            x = src[pl.ds(r0, BAND), :].astype(_f32)
            x2 = jnp.where(keep, x, pltpu.roll(x, HEAD_DIM, axis=1)).astype(_bf16)
            dst[pl.ds(r0, BAND), :] = jnp.concatenate([x2, x2], axis=1)
        return c

    lax.fori_loop(0, nb, expand, 0)
    heads_per_kv = GROUP_W // HEAD_DIM

    def band(i):
        cur = pl.ds(pl.multiple_of(i * BAND, BAND), BAND)
        prev = pl.ds(pl.multiple_of(jnp.maximum(i - 1, 0) * BAND, BAND), BAND)
        return (q_ref[cur, :], jnp.concatenate([kx_ref[prev, :], kx_ref[cur, :]], axis=0),
                jnp.concatenate([vx_ref[prev, :], vx_ref[cur, :]], axis=0), bias_s[jnp.minimum(i, 1)])

    def body(n, c):
        ids = [n * ATTN_UNROLL + u for u in range(ATTN_UNROLL)]
        outs = _band_blocks([band(i) for i in ids], want_lse=False,
                            sink_of_head=lambda h: sink_ref[kvh * heads_per_kv + h])
        for i, (o, _) in zip(ids, outs):
            o_ref[pl.ds(pl.multiple_of(i * BAND, BAND), BAND), :] = o.astype(o_ref.dtype)
        return c

    lax.fori_loop(0, nb // ATTN_UNROLL, body, 0)


def _swa(nat3, sinks):
    bsz, seq, _ = nat3.shape
    kvw = SWA_KV_HEADS * HEAD_DIM
    return pl.pallas_call(
        _swa_kernel,
        out_shape=jax.ShapeDtypeStruct((bsz, seq, SWA_KV_HEADS * GROUP_W), _bf16),
        grid=(bsz, SWA_KV_HEADS),
        in_specs=[
            pl.BlockSpec(memory_space=pltpu.SMEM),
            pl.BlockSpec((None, seq, GROUP_W), lambda b, g: (b, 0, NAT_BQ // GROUP_W + g)),
            pl.BlockSpec((None, seq, kvw), lambda b, g: (b, 0, NAT_BK // kvw)),
            pl.BlockSpec((None, seq, kvw), lambda b, g: (b, 0, NAT_BV // kvw)),
        ],
        out_specs=pl.BlockSpec((None, seq, GROUP_W), lambda b, g: (b, 0, g)),
        scratch_shapes=[pltpu.VMEM((seq, GROUP_W), _bf16), pltpu.VMEM((seq, GROUP_W), _bf16),
                        pltpu.VMEM((2, BAND, 2 * BAND), _f32)],
        compiler_params=_params("arbitrary", "arbitrary"),
        name="swa_attn",
    )(sinks, nat3, nat3, nat3).reshape(bsz * seq, SWA_KV_HEADS * GROUP_W)


HGRN_TILE = 256


def _hgrn_stage_count(tm):
    return 4 + (HGRN_TILE // HGRN_CHUNK + 1) * (tm // HGRN_TILE) + 1


def _hgrn_stages(c_ref, lbl_ref, gain_ref, st_ref, o_ref, *, layer):
    tm = c_ref.shape[0]
    tc = HGRN_TILE
    col = lambda k, sl: slice(k * HGRN_W + sl.start, k * HGRN_W + sl.stop)

    logits = lbl_ref[...]
    e = jnp.exp(logits - jnp.max(logits, axis=0, keepdims=True))
    sm = e / jnp.sum(e, axis=0, keepdims=True)
    lb = jnp.zeros((1, sm.shape[1]), _f32)
    for j in range(1, layer + 1):
        lb = lb + sm[j:j + 1, :]
    one_m_lb = 1.0 - lb

    nchunk = tc // HGRN_CHUNK
    row_in_chunk = lax.broadcasted_iota(jnp.int32, (tc, HGRN_DK), 0) & (HGRN_CHUNK - 1)
    scan_steps = [(s, row_in_chunk >= s) for s in (1, 2, 4, 8, 16) if s < HGRN_CHUNK]

    def chunk_prefix(x):
        for s, keep in scan_steps:
            x = x + jnp.where(keep, pltpu.roll(x, s, axis=0), 0.0)
        return x

    blk = HGRN_ATT_BLOCK
    tb = lax.broadcasted_iota(jnp.int32, (blk, blk), 0)
    sb = lax.broadcasted_iota(jnp.int32, (blk, blk), 1)
    causal_blk = jnp.logical_and((tb // HGRN_CHUNK) == (sb // HGRN_CHUNK), sb <= tb)

    heads = [slice(h * HGRN_DK, (h + 1) * HGRN_DK) for h in range(HGRN_HEADS)]
    subs = [slice(r0, r0 + tc) for r0 in range(0, tm, tc)]
    items = [(rows, sl) for rows in subs for sl in heads]
    def gates(rows, sl):
        qraw = c_ref[rows, col(0, sl)].astype(_f32)
        fz = c_ref[rows, col(1, sl)].astype(_f32)
        t = jnp.exp(-jnp.abs(fz))
        r = 1.0 / (1.0 + t)
        tr = t * r
        pos = fz >= 0.0
        logf = jnp.log(lb[:, sl] + one_m_lb[:, sl] * jnp.where(pos, r, tr))
        k = one_m_lb[:, sl] * jnp.where(pos, tr, r)
        return qraw * _sigmoid(qraw), k, logf

    def decayed(q, k, logf):
        eb = jnp.exp(chunk_prefix(logf))
        last = [eb[n * HGRN_CHUNK + HGRN_CHUNK - 1:(n + 1) * HGRN_CHUNK, :] for n in range(nchunk)]
        dec = jnp.concatenate([jnp.broadcast_to(d, (HGRN_CHUNK, HGRN_DK)) for d in last], axis=0)
        kinv = k * (1.0 / eb)
        return (q * eb).astype(_bf16), kinv.astype(_bf16), (kinv * dec).astype(_bf16), last

    qkl = [gates(rows, sl) for rows, sl in items]
    yield
    qkd = [decayed(q, k, logf) for q, k, logf in qkl]
    vs = [c_ref[rows, col(2, sl)] for rows, sl in items]
    yield
    att = [[jnp.where(causal_blk, _dot_nt(qe[r0:r0 + blk], ke[r0:r0 + blk]), 0.0).astype(_bf16)
            for r0 in range(0, tc, blk)] for qe, ke, _, _ in qkd]
    yield
    intra = [jnp.concatenate([_dot(a, v[i * blk:(i + 1) * blk]) for i, a in enumerate(att_h)], axis=0)
             for att_h, v in zip(att, vs)]
    yield

    states = [st_ref[h] for h in range(HGRN_HEADS)]
    for s, rows in enumerate(subs):
        per_head = [(qkd[i][0], qkd[i][2], vs[i], qkd[i][3], intra[i])
                    for i in range(s * HGRN_HEADS, (s + 1) * HGRN_HEADS)]
        inter = [[] for _ in heads]
        for n in range(nchunk):
            rs = slice(n * HGRN_CHUNK, (n + 1) * HGRN_CHUNK)
            for h, (qe, kd, v, dec, _) in enumerate(per_head):
                inter[h].append(_dot_nt(qe[rs], states[h].astype(_bf16)))
                states[h] = states[h] * dec[n] + _dot_tn(v[rs], kd[rs])
            yield
        for h, sl in enumerate(heads):
            o = per_head[h][4] + jnp.concatenate(inter[h], axis=0)
            y = _rms(o, gain_ref[...])
            graw = c_ref[rows, col(3, sl)].astype(_f32)
            o_ref[rows, sl] = (y * (graw * _sigmoid(graw))).astype(o_ref.dtype)
        yield
    for h in range(HGRN_HEADS):
        st_ref[h] = states[h]
    yield


MERGE_PIECE = 256


def _merge_kernel(oa_ref, ob_ref, oc_ref, g0_ref, g1_ref, g2_ref, h_ref, wa_ref, wb_ref, wc_ref, wo_ref, gp_ref,
                  out_ref):
    d = wo_ref.shape[0]
    pieces = [slice(c, c + MERGE_PIECE) for c in range(0, d, MERGE_PIECE)]

    def branches(cols):
        return (_dot(oa_ref[...], wa_ref[:, cols]), _dot(ob_ref[...], wb_ref[:, cols]),
                _dot(oc_ref[...], wc_ref[:, cols]))

    def gated(cols, ya, yb, yc):
        sig = lambda ref: _sigmoid(ref[:, cols].astype(_f32))
        return (sig(g0_ref) * ya + sig(g1_ref) * yb + sig(g2_ref) * yc).astype(_bf16)

    y = branches(pieces[0])
    mix = []
    for c, cols in enumerate(pieces):
        y_next = branches(pieces[c + 1]) if c + 1 < len(pieces) else None
        mix.append(gated(cols, *y))
        y = y_next
    out = _dot(jnp.concatenate(mix, axis=1), wo_ref[...])
    out_ref[...] = h_ref[...] + _rms(out, gp_ref[...])


def _merge(oa, ob, oc, nat, h, wa, wb, wc, wo, gp, *, tm=1024):
    t, d = h.shape
    row = lambda a: pl.BlockSpec((tm, a.shape[1]), lambda i: (i, 0))
    full = lambda a: pl.BlockSpec(a.shape, lambda i: (0, 0))
    gate = lambda j: pl.BlockSpec((tm, d), lambda i: (i, NAT_GATES // d + j))
    return pl.pallas_call(
        _merge_kernel,
        out_shape=jax.ShapeDtypeStruct((t, d), _f32),
        grid=(t // tm,),
        in_specs=[row(oa), row(ob), row(oc), gate(0), gate(1), gate(2), row(h),
                  full(wa), full(wb), full(wc), full(wo), full(gp)],
        out_specs=row(h),
        compiler_params=_params("arbitrary"),
        name="merge_out",
    )(oa, ob, oc, nat, nat, nat, h, wa, wb, wc, wo, gp)


HALO = 8
FFN_PIECE = 256

def _shift_rows(u, carry, s):
    r = pltpu.roll(u, s, axis=0)
    c = pltpu.roll(carry, s, axis=0)
    row = lax.broadcasted_iota(jnp.int32, carry.shape, 0)
    head = jnp.where(row < s, c, r[0:HALO])
    return jnp.concatenate([head, r[HALO:]], axis=0)


def _ffn_kernel(x_ref, gpre_ref, wa_ref, wb_ref, cwa_ref, cwb_ref, cba_ref, cbb_ref, wd_ref, gpost_ref,
                out_ref, hn_ref, acc_ref, ca_ref, cb_ref, *, tiles_per_seq):
    i, j = pl.program_id(0), pl.program_id(1)
    tm = x_ref.shape[0]
    tf = wa_ref.shape[1]

    @pl.when(j == 0)
    def _():
        hn_ref[...] = _rms(x_ref[...], gpre_ref[...]).astype(_bf16)
        acc_ref[...] = jnp.zeros_like(acc_ref)

    @pl.when(i == 0)
    def _():
        ca_ref[j] = jnp.zeros(ca_ref.shape[1:], _f32)
        cb_ref[j] = jnp.zeros(cb_ref.shape[1:], _f32)

    seq_start = (i % tiles_per_seq) == 0

    def up(cols):
        return _dot(hn_ref[...], wa_ref[:, cols]), _dot(hn_ref[...], wb_ref[:, cols])

    def conv(u, cols, cw_ref, cbias_ref, carry_ref):
        carry = jnp.where(seq_start, 0.0, carry_ref[j, :, cols])
        carry_ref[j, :, cols] = u[tm - HALO:tm]
        cw = cw_ref[:, cols]
        return (cw[2:3] * u + cw[1:2] * _shift_rows(u, carry, 1) + cw[0:1] * _shift_rows(u, carry, 2)
                + cbias_ref[:, cols])

    def gate(cols, ua, ub):
        a = conv(ua, cols, cwa_ref, cba_ref, ca_ref)
        b = conv(ub, cols, cwb_ref, cbb_ref, cb_ref)
        gelu = 0.5 * a * (1.0 + jnp.tanh(0.7978845608028654 * (a + 0.044715 * (a * a * a))))
        return (gelu * b).astype(_bf16)

    pieces = [slice(c, c + FFN_PIECE) for c in range(0, tf, FFN_PIECE)]
    u = up(pieces[0])
    ts = []
    for c, cols in enumerate(pieces):
        u_next = up(pieces[c + 1]) if c + 1 < len(pieces) else None
        ts.append(gate(cols, *u))
        u = u_next
    acc_ref[...] += _dot(jnp.concatenate(ts, axis=1), wd_ref[...])

    @pl.when(j == pl.num_programs(1) - 1)
    def _():
        out_ref[...] = x_ref[...] + _rms(acc_ref[...], gpost_ref[...])


def _ffn(h, gpre, w_up, conv_w, conv_b, w_down, gpost, *, seq, tm=1024, tf=1024):
    t, d = h.shape
    nf = D_FF // tf
    cb2 = conv_b.reshape(1, 2 * D_FF)
    return pl.pallas_call(
        functools.partial(_ffn_kernel, tiles_per_seq=seq // tm),
        out_shape=jax.ShapeDtypeStruct((t, d), _f32),
        grid=(t // tm, nf),
        in_specs=[
            pl.BlockSpec((tm, d), lambda i, j: (i, 0)),
            pl.BlockSpec((1, d), lambda i, j: (0, 0)),
            pl.BlockSpec((d, tf), lambda i, j: (0, j)),
            pl.BlockSpec((d, tf), lambda i, j: (0, nf + j)),
            pl.BlockSpec((3, tf), lambda i, j: (0, j)),
            pl.BlockSpec((3, tf), lambda i, j: (0, nf + j)),
            pl.BlockSpec((1, tf), lambda i, j: (0, j)),
            pl.BlockSpec((1, tf), lambda i, j: (0, nf + j)),
            pl.BlockSpec((tf, d), lambda i, j: (j, 0)),
            pl.BlockSpec((1, d), lambda i, j: (0, 0)),
        ],
        out_specs=pl.BlockSpec((tm, d), lambda i, j: (i, 0)),
        scratch_shapes=[
            pltpu.VMEM((tm, d), _bf16),
            pltpu.VMEM((tm, d), _f32),
            pltpu.VMEM((nf, HALO, tf), _f32),
            pltpu.VMEM((nf, HALO, tf), _f32),
        ],
        compiler_params=_params("arbitrary", "arbitrary"),
        name="conv_ffn",
    )(h, gpre, w_up, w_up, conv_w, conv_w, cb2, cb2, w_down, gpost)


def kernel(x, norm_pre_mix, norm_post_mix, norm_pre_ffn, norm_post_ffn, w_in, attn_sinks, hgrn_lb_logits,
           hgrn_out_norm, w_branch_a, w_branch_b, w_branch_c, w_out, w_ffn_up, ffn_conv_w, ffn_conv_b,
           w_ffn_down):
    bsz, seq, d = x.shape
    depth = w_in.shape[0]
    h = x.reshape(bsz * seq, d)
    row = lambda a: a.reshape(1, -1)
    for l in range(depth):
        nat, oc, g1, g2 = _in_proj(h, row(norm_pre_mix[l]), _permute_w_in(w_in[l]), hgrn_lb_logits,
                                   row(hgrn_out_norm[l]), layer=l, bsz=bsz, seq=seq)
        nat3 = nat.reshape(bsz, seq, NAT_W)
        oa = _dilated(nat3, g1, g2)
        ob = _swa(nat3, attn_sinks[l])
        h = _merge(oa, ob, oc, nat, h, w_branch_a[l].astype(_bf16), w_branch_b[l].astype(_bf16),
                   w_branch_c[l].astype(_bf16), w_out[l].astype(_bf16), row(norm_post_mix[l]))
        h = _ffn(h, row(norm_pre_ffn[l]), w_ffn_up[l].astype(_bf16), ffn_conv_w[l], ffn_conv_b[l],
                 w_ffn_down[l].astype(_bf16), row(norm_post_ffn[l]), seq=seq)
    return h.reshape(bsz, seq, d)
```

```python
import functools

import jax
import jax.numpy as jnp
from jax import lax
from jax.experimental import pallas as pl
from jax.experimental.pallas import tpu as pltpu

D_MODEL = 1024
HEAD_DIM = 64
DIL_PATTERNS = ((128, 1), (512, 4), (2048, 16))
SWA_WINDOW = 128
SWA_KV_HEADS = 2
HGRN_HEADS = 4
HGRN_DK = 128
HGRN_CHUNK = 32
HGRN_ATT_BLOCK = 128
D_FF = 4 * D_MODEL
BAND = 128
GROUP_W = 256
QKV_W = 3 * GROUP_W
EPS = 1e-6
LANES = 128
NEG = -1e30
ATTN_UNROLL = 4

SRC_AQ, SRC_B, SRC_GATES = 0, 2304, 5120
NAT_GATES = 0
NAT_A0 = 3072
NAT_BQ, NAT_BK, NAT_BV = 3840, 4352, 4480
NAT_W = 4608
HGRN_W = HGRN_HEADS * HGRN_DK
COL_C = NAT_W
COL_G1 = COL_C + 4 * HGRN_W
COL_G2 = COL_G1 + QKV_W
D_IN = COL_G2 + QKV_W

VMEM_LIMIT = 56 * 1024 * 1024

_f32 = jnp.float32
_bf16 = jnp.bfloat16


def _dot(a, b):
    return jnp.dot(a, b, preferred_element_type=_f32)


def _dot_nt(a, b):
    return lax.dot_general(a, b, (((1,), (1,)), ((), ())), preferred_element_type=_f32)


def _dot_tn(a, b):
    return lax.dot_general(a, b, (((0,), (0,)), ((), ())), preferred_element_type=_f32)


def _rms(x, g):
    return x * lax.rsqrt(jnp.mean(x * x, axis=-1, keepdims=True) + EPS) * g


def _sigmoid(x):
    return 1.0 / (1.0 + jnp.exp(-x))


def _params(*sem):
    return pltpu.CompilerParams(dimension_semantics=sem, vmem_limit_bytes=VMEM_LIMIT)


def _permute_w_in(w):
    w = w.astype(_bf16)
    d, ng = w.shape[0], len(DIL_PATTERNS)
    a = w[:, SRC_AQ:SRC_B].reshape(d, 3, ng, GROUP_W).transpose(0, 2, 1, 3).reshape(d, ng * QKV_W)
    return jnp.concatenate([w[:, SRC_GATES:], a[:, :QKV_W], w[:, SRC_B:SRC_GATES], a[:, QKV_W:]], axis=1)


def _in_proj_kernel(x_ref, g_ref, w_ref, lbl_ref, gain_ref, nat_ref, oc_ref, g1_ref, g2_ref,
                    hn_ref, c_ref, st_ref, *, tn, layer, tiles_per_seq):
    tm = x_ref.shape[0]

    @pl.when(pl.program_id(0) % tiles_per_seq == 0)
    def _():
        st_ref[...] = jnp.zeros_like(st_ref)

    hn = _rms(x_ref[...], g_ref[...])
    nlc = hn_ref.shape[0]
    for c in range(nlc):
        hn_ref[c] = hn[:, c * LANES:(c + 1) * LANES]
    hb = hn.astype(_bf16)
    th = 2 * HGRN_W

    def hgrn_chunk(c):
        c_ref[:, c:c + th] = _dot(hb, w_ref[:, COL_C + c:COL_C + c + th]).astype(_bf16)

    def nat_chunk(c):
        nat_ref[:, c:c + tn] = _dot(hb, w_ref[:, c:c + tn]).astype(_bf16)

    def dilated(out_ref, col0):
        d = out_ref.shape[0]
        rows = tm // d
        lhs = jnp.concatenate(
            [jnp.concatenate([hn_ref[c, pl.ds(r, rows, stride=d), :] for c in range(nlc)], axis=1)
             for r in range(d)], axis=0).astype(_bf16)
        y = _dot(lhs, w_ref[:, col0:col0 + QKV_W]).astype(_bf16)
        for r in range(d):
            out_ref[r] = y[r * rows:(r + 1) * rows]

    hgrn_chunk(0)
    matmuls = [functools.partial(hgrn_chunk, th)]
    matmuls += [functools.partial(nat_chunk, c) for c in range(0, NAT_W, tn)]
    matmuls += [functools.partial(dilated, g1_ref, COL_G1), functools.partial(dilated, g2_ref, COL_G2)]
    hgrn = _hgrn_stages(c_ref, lbl_ref, gain_ref, st_ref, oc_ref, layer=layer)
    per_matmul = -(-_hgrn_stage_count(tm) // len(matmuls))
    for mm in matmuls:
        mm()
        for _ in range(per_matmul):
            next(hgrn, None)
    for _ in hgrn:
        pass


def _in_proj(x, g, w, lb_logits, gain, *, layer, bsz, seq, tm=512, tn=768):
    t, d = x.shape
    tps = seq // tm
    d1, d2 = DIL_PATTERNS[1][1], DIL_PATTERNS[2][1]
    return pl.pallas_call(
        functools.partial(_in_proj_kernel, tn=tn, layer=layer, tiles_per_seq=tps),
        out_shape=(jax.ShapeDtypeStruct((t, NAT_W), _bf16),
                   jax.ShapeDtypeStruct((t, HGRN_W), _bf16),
                   jax.ShapeDtypeStruct((bsz, d1, seq // d1, QKV_W), _bf16),
                   jax.ShapeDtypeStruct((bsz, d2, seq // d2, QKV_W), _bf16)),
        grid=(t // tm,),
        in_specs=[
            pl.BlockSpec((tm, d), lambda i: (i, 0)),
            pl.BlockSpec((1, d), lambda i: (0, 0)),
            pl.BlockSpec((d, D_IN), lambda i: (0, 0), pipeline_mode=pl.Buffered(1)),
            pl.BlockSpec(lb_logits.shape, lambda i: (0, 0)),
            pl.BlockSpec((1, HGRN_DK), lambda i: (0, 0)),
        ],
        out_specs=(
            pl.BlockSpec((tm, NAT_W), lambda i: (i, 0)),
            pl.BlockSpec((tm, HGRN_W), lambda i: (i, 0)),
            pl.BlockSpec((None, d1, tm // d1, QKV_W), lambda i: (i // tps, 0, i % tps, 0)),
            pl.BlockSpec((None, d2, tm // d2, QKV_W), lambda i: (i // tps, 0, i % tps, 0)),
        ),
        scratch_shapes=[pltpu.VMEM((d // LANES, tm, LANES), _f32),
                        pltpu.VMEM((tm, 4 * HGRN_W), _bf16),
                        pltpu.VMEM((HGRN_HEADS, HGRN_DK, HGRN_DK), _f32)],
        compiler_params=_params("arbitrary"),
        name="in_proj",
    )(x, g, w, lb_logits, gain)


def _band_bias(max_dist, nk, first):
    qi = lax.broadcasted_iota(jnp.int32, (BAND, nk), 0)
    ki = lax.broadcasted_iota(jnp.int32, (BAND, nk), 1)
    dist = qi + (nk - BAND) - ki
    ok = (dist >= 0) & (dist <= max_dist)
    if first:
        ok = ok & (ki >= nk - BAND)
    return jnp.where(ok, 0.0, NEG).astype(_f32)


def _band_blocks(blocks, *, sink_of_head, want_lse):
    nh = GROUP_W // HEAD_DIM
    nk = blocks[0][1].shape[0]
    has_prev = nk == 2 * BAND
    lane_head = lax.broadcasted_iota(jnp.int32, (BAND, GROUP_W), 1) // HEAD_DIM
    fold = (lambda x, op: op(x[..., :BAND], x[..., BAND:])) if has_prev else (lambda x, op: x)
    scale = jnp.asarray(HEAD_DIM ** -0.5, _bf16)

    def stack_heads(q):
        q = q * scale
        return jnp.concatenate([jnp.where(lane_head == h, q, jnp.zeros_like(q)) for h in range(nh)], axis=0)

    def pick_heads(x):
        out = jnp.broadcast_to(x[0], (BAND, GROUP_W))
        for h in range(1, nh):
            out = jnp.where(lane_head == h, x[h], out)
        return out

    if sink_of_head is not None:
        head = lax.broadcasted_iota(jnp.int32, (nh, BAND, 1), 0)
        sink = jnp.zeros((nh, BAND, 1), _f32)
        for h in range(nh):
            sink = jnp.where(head == h, sink_of_head(h), sink)

    def scores(qh, kcat, bias):
        x = _dot_nt(qh, kcat).reshape(nh, BAND, nk) + bias[None]
        mx = jnp.max(fold(x, jnp.maximum), axis=-1, keepdims=True)
        return x, (mx if sink_of_head is None else jnp.maximum(mx, sink))

    def probs(x, mx):
        p = jnp.exp(x - mx)
        den = jnp.sum(fold(p, jnp.add), axis=-1, keepdims=True)
        if sink_of_head is not None:
            den = den + jnp.exp(sink - mx)
        return p.reshape(nh * BAND, nk).astype(_bf16), den

    qs = [stack_heads(q) for q, _, _, _ in blocks]
    sm = [scores(qh, kcat, bias) for qh, (_, kcat, _, bias) in zip(qs, blocks)]
    pd = [probs(x, mx) for x, mx in sm]
    pv = [_dot(p, vcat).reshape(nh, BAND, GROUP_W) for (p, _), (_, _, vcat, _) in zip(pd, blocks)]
    o = [pick_heads(x * (1.0 / d)) for x, (_, d) in zip(pv, pd)]
    if not want_lse:
        return [(x, None) for x in o]
    return [(x, pick_heads(mx + jnp.log(d))) for x, (_, mx), (_, d) in zip(o, sm, pd)]


def _qkv_cols(ref, rows, lead=()):
    idx = lambda c: lead + (rows, slice(c * GROUP_W, (c + 1) * GROUP_W))
    return ref[idx(0)], ref[idx(1)], ref[idx(2)]


def _dil_kernel(a0_ref, g1_ref, g2_ref, oa_ref, o_s, lse_s, bias_s):
    seq = a0_ref.shape[0]
    blks = functools.partial(_band_blocks, sink_of_head=None, want_lse=True)
    halves = [slice(hf * LANES, (hf + 1) * LANES) for hf in range(GROUP_W // LANES)]
    nu = ATTN_UNROLL
    win0, win1, win2 = (w // d for w, d in DIL_PATTERNS)
    d1, d2 = DIL_PATTERNS[1][1], DIL_PATTERNS[2][1]
    assert win0 == win1 and seq // d2 == BAND and win2 >= BAND - 1
    bias_s[0] = _band_bias(win0, 2 * BAND, True)
    bias_s[1] = _band_bias(win0, 2 * BAND, False)
    bias_s[2, :, 0:BAND] = _band_bias(win2, BAND, True)

    def put(g, rows, o, lse):
        for hf, ls in enumerate(halves):
            o_s[g, hf, rows, :] = o[:, ls]
            lse_s[g, hf, rows, :] = lse[:, ls]

    def band(ref, lead, i):
        cur = pl.ds(pl.multiple_of(i * BAND, BAND), BAND)
        prev = pl.ds(pl.multiple_of(jnp.maximum(i - 1, 0) * BAND, BAND), BAND)
        q, kc, vc = _qkv_cols(ref, cur, lead)
        _, kp, vp = _qkv_cols(ref, prev, lead)
        return (q, jnp.concatenate([kp, kc], axis=0), jnp.concatenate([vp, vc], axis=0),
                bias_s[jnp.minimum(i, 1)])

    def g0_body(n, c):
        ids = [n * nu + u for u in range(nu)]
        for i, (o, lse) in zip(ids, blks([band(a0_ref, (), i) for i in ids])):
            put(0, pl.ds(pl.multiple_of(i * BAND, BAND), BAND), o, lse)
        return c

    lax.fori_loop(0, seq // BAND // nu, g0_body, 0)

    nb1 = seq // d1 // BAND

    def g1_body(n, c):
        ids = [((n * nu + u) // nb1, (n * nu + u) % nb1) for u in range(nu)]
        for (r, i), (o, lse) in zip(ids, blks([band(g1_ref, (r,), i) for r, i in ids])):
            put(1, pl.ds(i * (BAND * d1) + r, BAND, stride=d1), o, lse)
        return c

    lax.fori_loop(0, d1 * nb1 // nu, g1_body, 0)

    def g2_body(n, c):
        ids = [n * nu + u for u in range(nu)]
        outs = blks([_qkv_cols(g2_ref, slice(None), (r,)) + (bias_s[2, :, 0:BAND],) for r in ids])
        for r, (o, lse) in zip(ids, outs):
            put(2, pl.ds(r, BAND, stride=d2), o, lse)
        return c

    lax.fori_loop(0, d2 // nu, g2_body, 0)

    def mix_body(i, c):
        rows = pl.ds(pl.multiple_of(i * BAND, BAND), BAND)
        for hf, ls in enumerate(halves):
            l0, l1, l2 = lse_s[0, hf, rows, :], lse_s[1, hf, rows, :], lse_s[2, hf, rows, :]
            m = jnp.maximum(jnp.maximum(l0, l1), l2)
            e0, e1, e2 = jnp.exp(l0 - m), jnp.exp(l1 - m), jnp.exp(l2 - m)
            oa = (e0 * o_s[0, hf, rows, :] + e1 * o_s[1, hf, rows, :] + e2 * o_s[2, hf, rows, :]) / (e0 + e1 + e2)
            oa_ref[rows, ls] = oa.astype(oa_ref.dtype)
        return c

    lax.fori_loop(0, seq // BAND, mix_body, 0)


def _dilated(nat3, g1, g2):
    bsz, seq, _ = nat3.shape
    n_grp = len(DIL_PATTERNS)
    return pl.pallas_call(
        _dil_kernel,
        out_shape=jax.ShapeDtypeStruct((bsz, seq, GROUP_W), _bf16),
        grid=(bsz,),
        in_specs=[
            pl.BlockSpec((None, seq, QKV_W), lambda b: (b, 0, NAT_A0 // QKV_W)),
            pl.BlockSpec((None,) + g1.shape[1:], lambda b: (b, 0, 0, 0)),
            pl.BlockSpec((None,) + g2.shape[1:], lambda b: (b, 0, 0, 0)),
        ],
        out_specs=pl.BlockSpec((None, seq, GROUP_W), lambda b: (b, 0, 0)),
        scratch_shapes=[pltpu.VMEM((n_grp, GROUP_W // LANES, seq, LANES), _f32)] * 2
        + [pltpu.VMEM((3, BAND, 2 * BAND), _f32)],
        compiler_params=_params("arbitrary"),
        name="dil_attn",
    )(nat3, g1, g2).reshape(bsz * seq, GROUP_W)


def _swa_kernel(sink_ref, q_ref, k_ref, v_ref, o_ref, kx_ref, vx_ref, bias_s):
    seq = q_ref.shape[0]
    nb = seq // BAND
    kvh = pl.program_id(1)
    bias_s[0] = _band_bias(SWA_WINDOW - 1, 2 * BAND, True)
    bias_s[1] = _band_bias(SWA_WINDOW - 1, 2 * BAND, False)
    lane = lax.broadcasted_iota(jnp.int32, (BAND, 2 * HEAD_DIM), 1)
    keep = (lane < HEAD_DIM) == (kvh == 0)

    def expand(i, c):
        r0 = pl.multiple_of(i * BAND, BAND)
        for src, dst in ((k_ref, kx_ref), (v_ref, vx_ref)):
            x = src[pl.ds(r0, BAND), :].astype(_f32)
            x2 = jnp.where(keep, x, pltpu.roll(x, HEAD_DIM, axis=1)).astype(_bf16)
            dst[pl.ds(r0, BAND), :] = jnp.concatenate([x2, x2], axis=1)
        return c

    lax.fori_loop(0, nb, expand, 0)
    heads_per_kv = GROUP_W // HEAD_DIM

    def band(i):
        cur = pl.ds(pl.multiple_of(i * BAND, BAND), BAND)
        prev = pl.ds(pl.multiple_of(jnp.maximum(i - 1, 0) * BAND, BAND), BAND)
        return (q_ref[cur, :], jnp.concatenate([kx_ref[prev, :], kx_ref[cur, :]], axis=0),
                jnp.concatenate([vx_ref[prev, :], vx_ref[cur, :]], axis=0), bias_s[jnp.minimum(i, 1)])

    def body(n, c):
        ids = [n * ATTN_UNROLL + u for u in range(ATTN_UNROLL)]
        outs = _band_blocks([band(i) for i in ids], want_lse=False,
                            sink_of_head=lambda h: sink_ref[kvh * heads_per_kv + h])
        for i, (o, _) in zip(ids, outs):
            o_ref[pl.ds(pl.multiple_of(i * BAND, BAND), BAND), :] = o.astype(o_ref.dtype)
        return c

    lax.fori_loop(0, nb // ATTN_UNROLL, body, 0)


def _swa(nat3, sinks):
    bsz, seq, _ = nat3.shape
    kvw = SWA_KV_HEADS * HEAD_DIM
    return pl.pallas_call(
        _swa_kernel,
        out_shape=jax.ShapeDtypeStruct((bsz, seq, SWA_KV_HEADS * GROUP_W), _bf16),
        grid=(bsz, SWA_KV_HEADS),
        in_specs=[
            pl.BlockSpec(memory_space=pltpu.SMEM),
            pl.BlockSpec((None, seq, GROUP_W), lambda b, g: (b, 0, NAT_BQ // GROUP_W + g)),
            pl.BlockSpec((None, seq, kvw), lambda b, g: (b, 0, NAT_BK // kvw)),
            pl.BlockSpec((None, seq, kvw), lambda b, g: (b, 0, NAT_BV // kvw)),
        ],
        out_specs=pl.BlockSpec((None, seq, GROUP_W), lambda b, g: (b, 0, g)),
        scratch_shapes=[pltpu.VMEM((seq, GROUP_W), _bf16), pltpu.VMEM((seq, GROUP_W), _bf16),
                        pltpu.VMEM((2, BAND, 2 * BAND), _f32)],
        compiler_params=_params("arbitrary", "arbitrary"),
        name="swa_attn",
    )(sinks, nat3, nat3, nat3).reshape(bsz * seq, SWA_KV_HEADS * GROUP_W)


HGRN_TILE = 256


def _hgrn_stage_count(tm):
    return 4 + (HGRN_TILE // HGRN_CHUNK + 1) * (tm // HGRN_TILE) + 1


def _hgrn_stages(c_ref, lbl_ref, gain_ref, st_ref, o_ref, *, layer):
    tm = c_ref.shape[0]
    tc = HGRN_TILE
    col = lambda k, sl: slice(k * HGRN_W + sl.start, k * HGRN_W + sl.stop)

    logits = lbl_ref[...]
    e = jnp.exp(logits - jnp.max(logits, axis=0, keepdims=True))
    sm = e / jnp.sum(e, axis=0, keepdims=True)
    lb = jnp.zeros((1, sm.shape[1]), _f32)
    for j in range(1, layer + 1):
        lb = lb + sm[j:j + 1, :]
    one_m_lb = 1.0 - lb

    nchunk = tc // HGRN_CHUNK
    row_in_chunk = lax.broadcasted_iota(jnp.int32, (tc, HGRN_DK), 0) & (HGRN_CHUNK - 1)
    scan_steps = [(s, row_in_chunk >= s) for s in (1, 2, 4, 8, 16) if s < HGRN_CHUNK]

    def chunk_prefix(x):
        for s, keep in scan_steps:
            x = x + jnp.where(keep, pltpu.roll(x, s, axis=0), 0.0)
        return x

    blk = HGRN_ATT_BLOCK
    tb = lax.broadcasted_iota(jnp.int32, (blk, blk), 0)
    sb = lax.broadcasted_iota(jnp.int32, (blk, blk), 1)
    causal_blk = jnp.logical_and((tb // HGRN_CHUNK) == (sb // HGRN_CHUNK), sb <= tb)

    heads = [slice(h * HGRN_DK, (h + 1) * HGRN_DK) for h in range(HGRN_HEADS)]
    subs = [slice(r0, r0 + tc) for r0 in range(0, tm, tc)]
    items = [(rows, sl) for rows in subs for sl in heads]
    def gates(rows, sl):
        qraw = c_ref[rows, col(0, sl)].astype(_f32)
        fz = c_ref[rows, col(1, sl)].astype(_f32)
        t = jnp.exp(-jnp.abs(fz))
        r = 1.0 / (1.0 + t)
        tr = t * r
        pos = fz >= 0.0
        logf = jnp.log(lb[:, sl] + one_m_lb[:, sl] * jnp.where(pos, r, tr))
        k = one_m_lb[:, sl] * jnp.where(pos, tr, r)
        return qraw * _sigmoid(qraw), k, logf

    def decayed(q, k, logf):
        eb = jnp.exp(chunk_prefix(logf))
        last = [eb[n * HGRN_CHUNK + HGRN_CHUNK - 1:(n + 1) * HGRN_CHUNK, :] for n in range(nchunk)]
        dec = jnp.concatenate([jnp.broadcast_to(d, (HGRN_CHUNK, HGRN_DK)) for d in last], axis=0)
        kinv = k * (1.0 / eb)
        return (q * eb).astype(_bf16), kinv.astype(_bf16), (kinv * dec).astype(_bf16), last

    qkl = [gates(rows, sl) for rows, sl in items]
    yield
    qkd = [decayed(q, k, logf) for q, k, logf in qkl]
    vs = [c_ref[rows, col(2, sl)] for rows, sl in items]
    yield
    att = [[jnp.where(causal_blk, _dot_nt(qe[r0:r0 + blk], ke[r0:r0 + blk]), 0.0).astype(_bf16)
            for r0 in range(0, tc, blk)] for qe, ke, _, _ in qkd]
    yield
    intra = [jnp.concatenate([_dot(a, v[i * blk:(i + 1) * blk]) for i, a in enumerate(att_h)], axis=0)
             for att_h, v in zip(att, vs)]
    yield

    states = [st_ref[h] for h in range(HGRN_HEADS)]
    for s, rows in enumerate(subs):
        per_head = [(qkd[i][0], qkd[i][2], vs[i], qkd[i][3], intra[i])
                    for i in range(s * HGRN_HEADS, (s + 1) * HGRN_HEADS)]
        inter = [[] for _ in heads]
        for n in range(nchunk):
            rs = slice(n * HGRN_CHUNK, (n + 1) * HGRN_CHUNK)
            for h, (qe, kd, v, dec, _) in enumerate(per_head):
                inter[h].append(_dot_nt(qe[rs], states[h].astype(_bf16)))
                states[h] = states[h] * dec[n] + _dot_tn(v[rs], kd[rs])
            yield
        for h, sl in enumerate(heads):
            o = per_head[h][4] + jnp.concatenate(inter[h], axis=0)
            y = _rms(o, gain_ref[...])
            graw = c_ref[rows, col(3, sl)].astype(_f32)
            o_ref[rows, sl] = (y * (graw * _sigmoid(graw))).astype(o_ref.dtype)
        yield
    for h in range(HGRN_HEADS):
        st_ref[h] = states[h]
    yield


MERGE_PIECE = 256


def _merge_kernel(oa_ref, ob_ref, oc_ref, g0_ref, g1_ref, g2_ref, h_ref, wa_ref, wb_ref, wc_ref, wo_ref, gp_ref,
                  out_ref):
    d = wo_ref.shape[0]
    pieces = [slice(c, c + MERGE_PIECE) for c in range(0, d, MERGE_PIECE)]

    def branches(cols):
        return (_dot(oa_ref[...], wa_ref[:, cols]), _dot(ob_ref[...], wb_ref[:, cols]),
                _dot(oc_ref[...], wc_ref[:, cols]))

    def gated(cols, ya, yb, yc):
        sig = lambda ref: _sigmoid(ref[:, cols].astype(_f32))
        return (sig(g0_ref) * ya + sig(g1_ref) * yb + sig(g2_ref) * yc).astype(_bf16)

    y = branches(pieces[0])
    mix = []
    for c, cols in enumerate(pieces):
        y_next = branches(pieces[c + 1]) if c + 1 < len(pieces) else None
        mix.append(gated(cols, *y))
        y = y_next
    out = _dot(jnp.concatenate(mix, axis=1), wo_ref[...])
    out_ref[...] = h_ref[...] + _rms(out, gp_ref[...])


def _merge(oa, ob, oc, nat, h, wa, wb, wc, wo, gp, *, tm=1024):
    t, d = h.shape
    row = lambda a: pl.BlockSpec((tm, a.shape[1]), lambda i: (i, 0))
    full = lambda a: pl.BlockSpec(a.shape, lambda i: (0, 0))
    gate = lambda j: pl.BlockSpec((tm, d), lambda i: (i, NAT_GATES // d + j))
    return pl.pallas_call(
        _merge_kernel,
        out_shape=jax.ShapeDtypeStruct((t, d), _f32),
        grid=(t // tm,),
        in_specs=[row(oa), row(ob), row(oc), gate(0), gate(1), gate(2), row(h),
                  full(wa), full(wb), full(wc), full(wo), full(gp)],
        out_specs=row(h),
        compiler_params=_params("arbitrary"),
        name="merge_out",
    )(oa, ob, oc, nat, nat, nat, h, wa, wb, wc, wo, gp)


HALO = 8
FFN_PIECE = 256

def _shift_rows(u, carry, s):
    r = pltpu.roll(u, s, axis=0)
    c = pltpu.roll(carry, s, axis=0)
    row = lax.broadcasted_iota(jnp.int32, carry.shape, 0)
    head = jnp.where(row < s, c, r[0:HALO])
    return jnp.concatenate([head, r[HALO:]], axis=0)


def _ffn_kernel(x_ref, gpre_ref, wa_ref, wb_ref, cwa_ref, cwb_ref, cba_ref, cbb_ref, wd_ref, gpost_ref,
                out_ref, hn_ref, acc_ref, ca_ref, cb_ref, *, tiles_per_seq):
    i, j = pl.program_id(0), pl.program_id(1)
    tm = x_ref.shape[0]
    tf = wa_ref.shape[1]

    @pl.when(j == 0)
    def _():
        hn_ref[...] = _rms(x_ref[...], gpre_ref[...]).astype(_bf16)
        acc_ref[...] = jnp.zeros_like(acc_ref)

    @pl.when(i == 0)
    def _():
        ca_ref[j] = jnp.zeros(ca_ref.shape[1:], _f32)
        cb_ref[j] = jnp.zeros(cb_ref.shape[1:], _f32)

    seq_start = (i % tiles_per_seq) == 0

    def up(cols):
        return _dot(hn_ref[...], wa_ref[:, cols]), _dot(hn_ref[...], wb_ref[:, cols])

    def conv(u, cols, cw_ref, cbias_ref, carry_ref):
        carry = jnp.where(seq_start, 0.0, carry_ref[j, :, cols])
        carry_ref[j, :, cols] = u[tm - HALO:tm]
        cw = cw_ref[:, cols]
        return (cw[2:3] * u + cw[1:2] * _shift_rows(u, carry, 1) + cw[0:1] * _shift_rows(u, carry, 2)
                + cbias_ref[:, cols])

    def gate(cols, ua, ub):
        a = conv(ua, cols, cwa_ref, cba_ref, ca_ref)
        b = conv(ub, cols, cwb_ref, cbb_ref, cb_ref)
        gelu = 0.5 * a * (1.0 + jnp.tanh(0.7978845608028654 * (a + 0.044715 * (a * a * a))))
        return (gelu * b).astype(_bf16)

    pieces = [slice(c, c + FFN_PIECE) for c in range(0, tf, FFN_PIECE)]
    u = up(pieces[0])
    ts = []
    for c, cols in enumerate(pieces):
        u_next = up(pieces[c + 1]) if c + 1 < len(pieces) else None
        ts.append(gate(cols, *u))
        u = u_next
    acc_ref[...] += _dot(jnp.concatenate(ts, axis=1), wd_ref[...])

    @pl.when(j == pl.num_programs(1) - 1)
    def _():
        out_ref[...] = x_ref[...] + _rms(acc_ref[...], gpost_ref[...])


def _ffn(h, gpre, w_up, conv_w, conv_b, w_down, gpost, *, seq, tm=1024, tf=1024):
    t, d = h.shape
    nf = D_FF // tf
    cb2 = conv_b.reshape(1, 2 * D_FF)
    return pl.pallas_call(
        functools.partial(_ffn_kernel, tiles_per_seq=seq // tm),
        out_shape=jax.ShapeDtypeStruct((t, d), _f32),
        grid=(t // tm, nf),
        in_specs=[
            pl.BlockSpec((tm, d), lambda i, j: (i, 0)),
            pl.BlockSpec((1, d), lambda i, j: (0, 0)),
            pl.BlockSpec((d, tf), lambda i, j: (0, j)),
            pl.BlockSpec((d, tf), lambda i, j: (0, nf + j)),
            pl.BlockSpec((3, tf), lambda i, j: (0, j)),
            pl.BlockSpec((3, tf), lambda i, j: (0, nf + j)),
            pl.BlockSpec((1, tf), lambda i, j: (0, j)),
            pl.BlockSpec((1, tf), lambda i, j: (0, nf + j)),
            pl.BlockSpec((tf, d), lambda i, j: (j, 0)),
            pl.BlockSpec((1, d), lambda i, j: (0, 0)),
        ],
        out_specs=pl.BlockSpec((tm, d), lambda i, j: (i, 0)),
        scratch_shapes=[
            pltpu.VMEM((tm, d), _bf16),
            pltpu.VMEM((tm, d), _f32),
            pltpu.VMEM((nf, HALO, tf), _f32),
            pltpu.VMEM((nf, HALO, tf), _f32),
        ],
        compiler_params=_params("arbitrary", "arbitrary"),
        name="conv_ffn",
    )(h, gpre, w_up, w_up, conv_w, conv_w, cb2, cb2, w_down, gpost)


def kernel(x, norm_pre_mix, norm_post_mix, norm_pre_ffn, norm_post_ffn, w_in, attn_sinks, hgrn_lb_logits,
           hgrn_out_norm, w_branch_a, w_branch_b, w_branch_c, w_out, w_ffn_up, ffn_conv_w, ffn_conv_b,
           w_ffn_down):
    bsz, seq, d = x.shape
    depth = w_in.shape[0]
    h = x.reshape(bsz * seq, d)
    row = lambda a: a.reshape(1, -1)
    for l in range(depth):
        nat, oc, g1, g2 = _in_proj(h, row(norm_pre_mix[l]), _permute_w_in(w_in[l]), hgrn_lb_logits,
                                   row(hgrn_out_norm[l]), layer=l, bsz=bsz, seq=seq)
        nat3 = nat.reshape(bsz, seq, NAT_W)
        oa = _dilated(nat3, g1, g2)
        ob = _swa(nat3, attn_sinks[l])
        h = _merge(oa, ob, oc, nat, h, w_branch_a[l].astype(_bf16), w_branch_b[l].astype(_bf16),
                   w_branch_c[l].astype(_bf16), w_out[l].astype(_bf16), row(norm_post_mix[l]))
        h = _ffn(h, row(norm_pre_ffn[l]), w_ffn_up[l].astype(_bf16), ffn_conv_w[l], ffn_conv_b[l],
                 w_ffn_down[l].astype(_bf16), row(norm_post_ffn[l]), seq=seq)
    return h.reshape(bsz, seq, d)
```

```python
import functools

import jax
import jax.numpy as jnp
from jax import lax
from jax.experimental import pallas as pl
from jax.experimental.pallas import tpu as pltpu

D_MODEL = 1024
HEAD_DIM = 64
DIL_PATTERNS = ((128, 1), (512, 4), (2048, 16))
SWA_WINDOW = 128
SWA_KV_HEADS = 2
HGRN_HEADS = 4
HGRN_DK = 128
HGRN_CHUNK = 32
HGRN_ATT_BLOCK = 128
D_FF = 4 * D_MODEL
BAND = 128
GROUP_W = 256
QKV_W = 3 * GROUP_W
EPS = 1e-6
LANES = 128
NEG = -1e30
ATTN_UNROLL = 4

SRC_AQ, SRC_AK, SRC_AV = 0, 768, 1536
SRC_B, SRC_C, SRC_GATES = 2304, 3072, 5120
NAT_GATES = 0
NAT_A0 = 3072
NAT_BQ, NAT_BK, NAT_BV = 3840, 4352, 4480
NAT_W = 4608
HGRN_W = HGRN_HEADS * HGRN_DK
COL_C = NAT_W
COL_G1 = COL_C + 4 * HGRN_W
COL_G2 = COL_G1 + QKV_W
D_IN = COL_G2 + QKV_W

VMEM_LIMIT = 56 * 1024 * 1024

_f32 = jnp.float32
_bf16 = jnp.bfloat16


def _dot(a, b):
    return jnp.dot(a, b, preferred_element_type=_f32)


def _dot_nt(a, b):
    return lax.dot_general(a, b, (((1,), (1,)), ((), ())), preferred_element_type=_f32)


def _dot_tn(a, b):
    return lax.dot_general(a, b, (((0,), (0,)), ((), ())), preferred_element_type=_f32)


def _rms(x, g):
    return x * lax.rsqrt(jnp.mean(x * x, axis=-1, keepdims=True) + EPS) * g


def _sigmoid(x):
    return 1.0 / (1.0 + jnp.exp(-x))


def _params(*sem):
    return pltpu.CompilerParams(dimension_semantics=sem, vmem_limit_bytes=VMEM_LIMIT)


def _permute_w_in(w):
    cols = lambda start, width: w[:, start:start + width]
    grp = lambda gi: [cols(SRC_AQ + gi * GROUP_W, GROUP_W), cols(SRC_AK + gi * GROUP_W, GROUP_W),
                      cols(SRC_AV + gi * GROUP_W, GROUP_W)]
    parts = [cols(SRC_GATES, 3 * D_MODEL)] + grp(0) + [cols(SRC_B, SRC_C - SRC_B), cols(SRC_C, SRC_GATES - SRC_C)]
    parts += grp(1) + grp(2)
    return jnp.concatenate(parts, axis=1).astype(_bf16)


def _in_proj_kernel(x_ref, g_ref, w_ref, lbl_ref, gain_ref, nat_ref, oc_ref, g1_ref, g2_ref,
                    hn_ref, c_ref, st_ref, *, tn, layer, tiles_per_seq):
    tm = x_ref.shape[0]

    @pl.when(pl.program_id(0) % tiles_per_seq == 0)
    def _():
        st_ref[...] = jnp.zeros_like(st_ref)

    hn = _rms(x_ref[...], g_ref[...])
    nlc = hn_ref.shape[0]
    for c in range(nlc):
        hn_ref[c] = hn[:, c * LANES:(c + 1) * LANES]
    hb = hn.astype(_bf16)
    th = 2 * HGRN_W

    def hgrn_chunk(c):
        c_ref[:, c:c + th] = _dot(hb, w_ref[:, COL_C + c:COL_C + c + th]).astype(_bf16)

    def nat_chunk(c):
        nat_ref[:, c:c + tn] = _dot(hb, w_ref[:, c:c + tn]).astype(_bf16)

    def dilated(out_ref, col0):
        d = out_ref.shape[0]
        rows = tm // d
        lhs = jnp.concatenate(
            [jnp.concatenate([hn_ref[c, pl.ds(r, rows, stride=d), :] for c in range(nlc)], axis=1)
             for r in range(d)], axis=0).astype(_bf16)
        y = _dot(lhs, w_ref[:, col0:col0 + QKV_W]).astype(_bf16)
        for r in range(d):
            out_ref[r] = y[r * rows:(r + 1) * rows]

    hgrn_chunk(0)
    matmuls = [functools.partial(hgrn_chunk, th)]
    matmuls += [functools.partial(nat_chunk, c) for c in range(0, NAT_W, tn)]
    matmuls += [functools.partial(dilated, g1_ref, COL_G1), functools.partial(dilated, g2_ref, COL_G2)]
    hgrn = _hgrn_stages(c_ref, lbl_ref, gain_ref, st_ref, oc_ref, layer=layer)
    per_matmul = -(-_hgrn_stage_count(tm) // len(matmuls))
    for mm in matmuls:
        mm()
        for _ in range(per_matmul):
            next(hgrn, None)
    for _ in hgrn:
        pass


def _in_proj(x, g, w, lb_logits, gain, *, layer, bsz, seq, tm=512, tn=768):
    t, d = x.shape
    tps = seq // tm
    d1, d2 = DIL_PATTERNS[1][1], DIL_PATTERNS[2][1]
    return pl.pallas_call(
        functools.partial(_in_proj_kernel, tn=tn, layer=layer, tiles_per_seq=tps),
        out_shape=(jax.ShapeDtypeStruct((t, NAT_W), _bf16),
                   jax.ShapeDtypeStruct((t, HGRN_W), _bf16),
                   jax.ShapeDtypeStruct((bsz, d1, seq // d1, QKV_W), _bf16),
                   jax.ShapeDtypeStruct((bsz, d2, seq // d2, QKV_W), _bf16)),
        grid=(t // tm,),
        in_specs=[
            pl.BlockSpec((tm, d), lambda i: (i, 0)),
            pl.BlockSpec((1, d), lambda i: (0, 0)),
            pl.BlockSpec((d, D_IN), lambda i: (0, 0), pipeline_mode=pl.Buffered(1)),
            pl.BlockSpec(lb_logits.shape, lambda i: (0, 0)),
            pl.BlockSpec((1, HGRN_DK), lambda i: (0, 0)),
        ],
        out_specs=(
            pl.BlockSpec((tm, NAT_W), lambda i: (i, 0)),
            pl.BlockSpec((tm, HGRN_W), lambda i: (i, 0)),
            pl.BlockSpec((None, d1, tm // d1, QKV_W), lambda i: (i // tps, 0, i % tps, 0)),
            pl.BlockSpec((None, d2, tm // d2, QKV_W), lambda i: (i // tps, 0, i % tps, 0)),
        ),
        scratch_shapes=[pltpu.VMEM((d // LANES, tm, LANES), _f32),
                        pltpu.VMEM((tm, 4 * HGRN_W), _bf16),
                        pltpu.VMEM((HGRN_HEADS, HGRN_DK, HGRN_DK), _f32)],
        compiler_params=_params("arbitrary"),
        name="in_proj",
    )(x, g, w, lb_logits, gain)


def _band_bias(max_dist, nk, first):
    qi = lax.broadcasted_iota(jnp.int32, (BAND, nk), 0)
    ki = lax.broadcasted_iota(jnp.int32, (BAND, nk), 1)
    dist = qi + (nk - BAND) - ki
    ok = (dist >= 0) & (dist <= max_dist)
    if first:
        ok = ok & (ki >= nk - BAND)
    return jnp.where(ok, 0.0, NEG).astype(_f32)


def _band_blocks(blocks, *, sink_of_head, want_lse):
    nh = GROUP_W // HEAD_DIM
    nk = blocks[0][1].shape[0]
    has_prev = nk == 2 * BAND
    lane_head = lax.broadcasted_iota(jnp.int32, (BAND, GROUP_W), 1) // HEAD_DIM
    fold = (lambda x, op: op(x[..., :BAND], x[..., BAND:])) if has_prev else (lambda x, op: x)
    scale = jnp.asarray(HEAD_DIM ** -0.5, _bf16)

    def stack_heads(q):
        q = q * scale
        return jnp.concatenate([jnp.where(lane_head == h, q, jnp.zeros_like(q)) for h in range(nh)], axis=0)

    def pick_heads(x):
        out = jnp.broadcast_to(x[0], (BAND, GROUP_W))
        for h in range(1, nh):
            out = jnp.where(lane_head == h, x[h], out)
        return out

    if sink_of_head is not None:
        head = lax.broadcasted_iota(jnp.int32, (nh, BAND, 1), 0)
        sink = jnp.zeros((nh, BAND, 1), _f32)
        for h in range(nh):
            sink = jnp.where(head == h, sink_of_head(h), sink)

    def scores(qh, kcat, bias):
        x = _dot_nt(qh, kcat).reshape(nh, BAND, nk) + bias[None]
        mx = jnp.max(fold(x, jnp.maximum), axis=-1, keepdims=True)
        return x, (mx if sink_of_head is None else jnp.maximum(mx, sink))

    def probs(x, mx):
        p = jnp.exp(x - mx)
        den = jnp.sum(fold(p, jnp.add), axis=-1, keepdims=True)
        if sink_of_head is not None:
            den = den + jnp.exp(sink - mx)
        return p.reshape(nh * BAND, nk).astype(_bf16), den

    qs = [stack_heads(q) for q, _, _, _ in blocks]
    sm = [scores(qh, kcat, bias) for qh, (_, kcat, _, bias) in zip(qs, blocks)]
    pd = [probs(x, mx) for x, mx in sm]
    pv = [_dot(p, vcat).reshape(nh, BAND, GROUP_W) for (p, _), (_, _, vcat, _) in zip(pd, blocks)]
    o = [pick_heads(x * (1.0 / d)) for x, (_, d) in zip(pv, pd)]
    if not want_lse:
        return [(x, None) for x in o]
    return [(x, pick_heads(mx + jnp.log(d))) for x, (_, mx), (_, d) in zip(o, sm, pd)]


def _qkv_cols(ref, rows, lead=()):
    idx = lambda c: lead + (rows, slice(c * GROUP_W, (c + 1) * GROUP_W))
    return ref[idx(0)], ref[idx(1)], ref[idx(2)]


def _dil_kernel(a0_ref, g1_ref, g2_ref, oa_ref, o_s, lse_s, bias_s):
    seq = a0_ref.shape[0]
    blks = functools.partial(_band_blocks, sink_of_head=None, want_lse=True)
    halves = [slice(hf * LANES, (hf + 1) * LANES) for hf in range(GROUP_W // LANES)]
    nu = ATTN_UNROLL
    win0, win1, win2 = (w // d for w, d in DIL_PATTERNS)
    d1, d2 = DIL_PATTERNS[1][1], DIL_PATTERNS[2][1]
    assert win0 == win1 and seq // d2 == BAND and win2 >= BAND - 1
    bias_s[0] = _band_bias(win0, 2 * BAND, True)
    bias_s[1] = _band_bias(win0, 2 * BAND, False)
    bias_s[2, :, 0:BAND] = _band_bias(win2, BAND, True)

    def put(g, rows, o, lse):
        for hf, ls in enumerate(halves):
            o_s[g, hf, rows, :] = o[:, ls]
            lse_s[g, hf, rows, :] = lse[:, ls]

    def band(ref, lead, i):
        cur = pl.ds(pl.multiple_of(i * BAND, BAND), BAND)
        prev = pl.ds(pl.multiple_of(jnp.maximum(i - 1, 0) * BAND, BAND), BAND)
        q, kc, vc = _qkv_cols(ref, cur, lead)
        _, kp, vp = _qkv_cols(ref, prev, lead)
        return (q, jnp.concatenate([kp, kc], axis=0), jnp.concatenate([vp, vc], axis=0),
                bias_s[jnp.minimum(i, 1)])

    def g0_body(n, c):
        ids = [n * nu + u for u in range(nu)]
        for i, (o, lse) in zip(ids, blks([band(a0_ref, (), i) for i in ids])):
            put(0, pl.ds(pl.multiple_of(i * BAND, BAND), BAND), o, lse)
        return c

    lax.fori_loop(0, seq // BAND // nu, g0_body, 0)

    nb1 = seq // d1 // BAND

    def g1_body(n, c):
        ids = [((n * nu + u) // nb1, (n * nu + u) % nb1) for u in range(nu)]
        for (r, i), (o, lse) in zip(ids, blks([band(g1_ref, (r,), i) for r, i in ids])):
            put(1, pl.ds(i * (BAND * d1) + r, BAND, stride=d1), o, lse)
        return c

    lax.fori_loop(0, d1 * nb1 // nu, g1_body, 0)

    def g2_body(n, c):
        ids = [n * nu + u for u in range(nu)]
        outs = blks([_qkv_cols(g2_ref, slice(None), (r,)) + (bias_s[2, :, 0:BAND],) for r in ids])
        for r, (o, lse) in zip(ids, outs):
            put(2, pl.ds(r, BAND, stride=d2), o, lse)
        return c

    lax.fori_loop(0, d2 // nu, g2_body, 0)

    def mix_body(i, c):
        rows = pl.ds(pl.multiple_of(i * BAND, BAND), BAND)
        for hf, ls in enumerate(halves):
            l0, l1, l2 = lse_s[0, hf, rows, :], lse_s[1, hf, rows, :], lse_s[2, hf, rows, :]
            m = jnp.maximum(jnp.maximum(l0, l1), l2)
            e0, e1, e2 = jnp.exp(l0 - m), jnp.exp(l1 - m), jnp.exp(l2 - m)
            oa = (e0 * o_s[0, hf, rows, :] + e1 * o_s[1, hf, rows, :] + e2 * o_s[2, hf, rows, :]) / (e0 + e1 + e2)
            oa_ref[rows, ls] = oa.astype(oa_ref.dtype)
        return c

    lax.fori_loop(0, seq // BAND, mix_body, 0, unroll=ATTN_UNROLL)


def _dilated(nat3, g1, g2):
    bsz, seq, _ = nat3.shape
    n_grp = len(DIL_PATTERNS)
    return pl.pallas_call(
        _dil_kernel,
        out_shape=jax.ShapeDtypeStruct((bsz, seq, GROUP_W), _bf16),
        grid=(bsz,),
        in_specs=[
            pl.BlockSpec((None, seq, QKV_W), lambda b: (b, 0, NAT_A0 // QKV_W)),
            pl.BlockSpec((None,) + g1.shape[1:], lambda b: (b, 0, 0, 0)),
            pl.BlockSpec((None,) + g2.shape[1:], lambda b: (b, 0, 0, 0)),
        ],
        out_specs=pl.BlockSpec((None, seq, GROUP_W), lambda b: (b, 0, 0)),
        scratch_shapes=[pltpu.VMEM((n_grp, GROUP_W // LANES, seq, LANES), _f32)] * 2
        + [pltpu.VMEM((3, BAND, 2 * BAND), _f32)],
        compiler_params=_params("arbitrary"),
        name="dil_attn",
    )(nat3, g1, g2).reshape(bsz * seq, GROUP_W)


def _swa_kernel(sink_ref, q_ref, k_ref, v_ref, o_ref, kx_ref, vx_ref, bias_s):
    seq = q_ref.shape[0]
    nb = seq // BAND
    kvh = pl.program_id(1)
    bias_s[0] = _band_bias(SWA_WINDOW - 1, 2 * BAND, True)
    bias_s[1] = _band_bias(SWA_WINDOW - 1, 2 * BAND, False)
    lane = lax.broadcasted_iota(jnp.int32, (BAND, 2 * HEAD_DIM), 1)
    keep = (lane < HEAD_DIM) == (kvh == 0)

    def expand(i, c):
        r0 = pl.multiple_of(i * BAND, BAND)
        for src, dst in ((k_ref, kx_ref), (v_ref, vx_ref)):
            x = src[pl.ds(r0, BAND), :].astype(_f32)
            x2 = jnp.where(keep, x, pltpu.roll(x, HEAD_DIM, axis=1)).astype(_bf16)
            dst[pl.ds(r0, BAND), :] = jnp.concatenate([x2, x2], axis=1)
        return c

    lax.fori_loop(0, nb, expand, 0, unroll=ATTN_UNROLL)
    heads_per_kv = GROUP_W // HEAD_DIM

    def band(i):
        cur = pl.ds(pl.multiple_of(i * BAND, BAND), BAND)
        prev = pl.ds(pl.multiple_of(jnp.maximum(i - 1, 0) * BAND, BAND), BAND)
        return (q_ref[cur, :], jnp.concatenate([kx_ref[prev, :], kx_ref[cur, :]], axis=0),
                jnp.concatenate([vx_ref[prev, :], vx_ref[cur, :]], axis=0), bias_s[jnp.minimum(i, 1)])

    def body(n, c):
        ids = [n * ATTN_UNROLL + u for u in range(ATTN_UNROLL)]
        outs = _band_blocks([band(i) for i in ids], want_lse=False,
                            sink_of_head=lambda h: sink_ref[kvh * heads_per_kv + h])
        for i, (o, _) in zip(ids, outs):
            o_ref[pl.ds(pl.multiple_of(i * BAND, BAND), BAND), :] = o.astype(o_ref.dtype)
        return c

    lax.fori_loop(0, nb // ATTN_UNROLL, body, 0)


def _swa(nat3, sinks):
    bsz, seq, _ = nat3.shape
    kvw = SWA_KV_HEADS * HEAD_DIM
    return pl.pallas_call(
        _swa_kernel,
        out_shape=jax.ShapeDtypeStruct((bsz, seq, SWA_KV_HEADS * GROUP_W), _bf16),
        grid=(bsz, SWA_KV_HEADS),
        in_specs=[
            pl.BlockSpec(memory_space=pltpu.SMEM),
            pl.BlockSpec((None, seq, GROUP_W), lambda b, g: (b, 0, NAT_BQ // GROUP_W + g)),
            pl.BlockSpec((None, seq, kvw), lambda b, g: (b, 0, NAT_BK // kvw)),
            pl.BlockSpec((None, seq, kvw), lambda b, g: (b, 0, NAT_BV // kvw)),
        ],
        out_specs=pl.BlockSpec((None, seq, GROUP_W), lambda b, g: (b, 0, g)),
        scratch_shapes=[pltpu.VMEM((seq, GROUP_W), _bf16), pltpu.VMEM((seq, GROUP_W), _bf16),
                        pltpu.VMEM((2, BAND, 2 * BAND), _f32)],
        compiler_params=_params("arbitrary", "arbitrary"),
        name="swa_attn",
    )(sinks, nat3, nat3, nat3).reshape(bsz * seq, SWA_KV_HEADS * GROUP_W)


HGRN_TILE = 256


def _hgrn_stage_count(tm):
    return 4 + (HGRN_TILE // HGRN_CHUNK + 1) * (tm // HGRN_TILE) + 1


def _hgrn_stages(c_ref, lbl_ref, gain_ref, st_ref, o_ref, *, layer):
    tm = c_ref.shape[0]
    tc = HGRN_TILE
    col = lambda k, sl: slice(k * HGRN_W + sl.start, k * HGRN_W + sl.stop)

    logits = lbl_ref[...]
    e = jnp.exp(logits - jnp.max(logits, axis=0, keepdims=True))
    sm = e / jnp.sum(e, axis=0, keepdims=True)
    lb = jnp.zeros((1, sm.shape[1]), _f32)
    for j in range(1, layer + 1):
        lb = lb + sm[j:j + 1, :]
    one_m_lb = 1.0 - lb

    nchunk = tc // HGRN_CHUNK
    row_in_chunk = lax.broadcasted_iota(jnp.int32, (tc, HGRN_DK), 0) & (HGRN_CHUNK - 1)
    scan_steps = [(s, row_in_chunk >= s) for s in (1, 2, 4, 8, 16) if s < HGRN_CHUNK]

    def chunk_prefix(x):
        for s, keep in scan_steps:
            x = x + jnp.where(keep, pltpu.roll(x, s, axis=0), 0.0)
        return x

    blk = HGRN_ATT_BLOCK
    tb = lax.broadcasted_iota(jnp.int32, (blk, blk), 0)
    sb = lax.broadcasted_iota(jnp.int32, (blk, blk), 1)
    causal_blk = jnp.logical_and((tb // HGRN_CHUNK) == (sb // HGRN_CHUNK), sb <= tb)

    heads = [slice(h * HGRN_DK, (h + 1) * HGRN_DK) for h in range(HGRN_HEADS)]
    subs = [slice(r0, r0 + tc) for r0 in range(0, tm, tc)]
    items = [(rows, sl) for rows in subs for sl in heads]
    def gates(rows, sl):
        qraw = c_ref[rows, col(0, sl)].astype(_f32)
        fz = c_ref[rows, col(1, sl)].astype(_f32)
        t = jnp.exp(-jnp.abs(fz))
        r = 1.0 / (1.0 + t)
        tr = t * r
        pos = fz >= 0.0
        logf = jnp.log(lb[:, sl] + one_m_lb[:, sl] * jnp.where(pos, r, tr))
        k = one_m_lb[:, sl] * jnp.where(pos, tr, r)
        return qraw * _sigmoid(qraw), k, logf

    def decayed(q, k, logf):
        eb = jnp.exp(chunk_prefix(logf))
        last = [eb[n * HGRN_CHUNK + HGRN_CHUNK - 1:(n + 1) * HGRN_CHUNK, :] for n in range(nchunk)]
        dec = jnp.concatenate([jnp.broadcast_to(d, (HGRN_CHUNK, HGRN_DK)) for d in last], axis=0)
        kinv = k * (1.0 / eb)
        return (q * eb).astype(_bf16), kinv.astype(_bf16), (kinv * dec).astype(_bf16), last

    qkl = [gates(rows, sl) for rows, sl in items]
    yield
    qkd = [decayed(q, k, logf) for q, k, logf in qkl]
    vs = [c_ref[rows, col(2, sl)] for rows, sl in items]
    yield
    att = [[jnp.where(causal_blk, _dot_nt(qe[r0:r0 + blk], ke[r0:r0 + blk]), 0.0).astype(_bf16)
            for r0 in range(0, tc, blk)] for qe, ke, _, _ in qkd]
    yield
    intra = [jnp.concatenate([_dot(a, v[i * blk:(i + 1) * blk]) for i, a in enumerate(att_h)], axis=0)
             for att_h, v in zip(att, vs)]
    yield

    states = [st_ref[h] for h in range(HGRN_HEADS)]
    for s, rows in enumerate(subs):
        per_head = [(qkd[i][0], qkd[i][2], vs[i], qkd[i][3], intra[i])
                    for i in range(s * HGRN_HEADS, (s + 1) * HGRN_HEADS)]
        inter = [[] for _ in heads]
        for n in range(nchunk):
            rs = slice(n * HGRN_CHUNK, (n + 1) * HGRN_CHUNK)
            for h, (qe, kd, v, dec, _) in enumerate(per_head):
                inter[h].append(_dot_nt(qe[rs], states[h].astype(_bf16)))
                states[h] = states[h] * dec[n] + _dot_tn(v[rs], kd[rs])
            yield
        for h, sl in enumerate(heads):
            o = per_head[h][4] + jnp.concatenate(inter[h], axis=0)
            y = _rms(o, gain_ref[...])
            graw = c_ref[rows, col(3, sl)].astype(_f32)
            o_ref[rows, sl] = (y * (graw * _sigmoid(graw))).astype(o_ref.dtype)
        yield
    for h in range(HGRN_HEADS):
        st_ref[h] = states[h]
    yield


MERGE_PIECE = 256


def _merge_kernel(oa_ref, ob_ref, oc_ref, g0_ref, g1_ref, g2_ref, h_ref, wa_ref, wb_ref, wc_ref, wo_ref, gp_ref,
                  out_ref):
    d = wo_ref.shape[0]
    pieces = [slice(c, c + MERGE_PIECE) for c in range(0, d, MERGE_PIECE)]

    def branches(cols):
        return (_dot(oa_ref[...], wa_ref[:, cols]), _dot(ob_ref[...], wb_ref[:, cols]),
                _dot(oc_ref[...], wc_ref[:, cols]))

    def gated(cols, ya, yb, yc):
        sig = lambda ref: _sigmoid(ref[:, cols].astype(_f32))
        return (sig(g0_ref) * ya + sig(g1_ref) * yb + sig(g2_ref) * yc).astype(_bf16)

    y = branches(pieces[0])
    mix = []
    for c, cols in enumerate(pieces):
        y_next = branches(pieces[c + 1]) if c + 1 < len(pieces) else None
        mix.append(gated(cols, *y))
        y = y_next
    out = _dot(jnp.concatenate(mix, axis=1), wo_ref[...])
    out_ref[...] = h_ref[...] + _rms(out, gp_ref[...])


def _merge(oa, ob, oc, nat, h, wa, wb, wc, wo, gp, *, tm=1024):
    t, d = h.shape
    row = lambda a: pl.BlockSpec((tm, a.shape[1]), lambda i: (i, 0))
    full = lambda a: pl.BlockSpec(a.shape, lambda i: (0, 0))
    gate = lambda j: pl.BlockSpec((tm, d), lambda i: (i, NAT_GATES // d + j))
    return pl.pallas_call(
        _merge_kernel,
        out_shape=jax.ShapeDtypeStruct((t, d), _f32),
        grid=(t // tm,),
        in_specs=[row(oa), row(ob), row(oc), gate(0), gate(1), gate(2), row(h),
                  full(wa), full(wb), full(wc), full(wo), full(gp)],
        out_specs=row(h),
        compiler_params=_params("arbitrary"),
        name="merge_out",
    )(oa, ob, oc, nat, nat, nat, h, wa, wb, wc, wo, gp)


HALO = 8
FFN_PIECE = 256

def _shift_rows(u, carry, s):
    r = pltpu.roll(u, s, axis=0)
    c = pltpu.roll(carry, s, axis=0)
    row = lax.broadcasted_iota(jnp.int32, carry.shape, 0)
    head = jnp.where(row < s, c, r[0:HALO])
    return jnp.concatenate([head, r[HALO:]], axis=0)


def _ffn_kernel(x_ref, gpre_ref, wa_ref, wb_ref, cwa_ref, cwb_ref, cba_ref, cbb_ref, wd_ref, gpost_ref,
                out_ref, hn_ref, acc_ref, ca_ref, cb_ref, *, tiles_per_seq):
    i, j = pl.program_id(0), pl.program_id(1)
    tm = x_ref.shape[0]
    tf = wa_ref.shape[1]

    @pl.when(j == 0)
    def _():
        hn_ref[...] = _rms(x_ref[...], gpre_ref[...]).astype(_bf16)
        acc_ref[...] = jnp.zeros_like(acc_ref)

    @pl.when(i == 0)
    def _():
        ca_ref[j] = jnp.zeros(ca_ref.shape[1:], _f32)
        cb_ref[j] = jnp.zeros(cb_ref.shape[1:], _f32)

    seq_start = (i % tiles_per_seq) == 0

    def up(cols):
        return _dot(hn_ref[...], wa_ref[:, cols]), _dot(hn_ref[...], wb_ref[:, cols])

    def conv(u, cols, cw_ref, cbias_ref, carry_ref):
        carry = jnp.where(seq_start, 0.0, carry_ref[j, :, cols])
        carry_ref[j, :, cols] = u[tm - HALO:tm]
        cw = cw_ref[:, cols]
        return (cw[2:3] * u + cw[1:2] * _shift_rows(u, carry, 1) + cw[0:1] * _shift_rows(u, carry, 2)
                + cbias_ref[:, cols])

    def gate(cols, ua, ub):
        a = conv(ua, cols, cwa_ref, cba_ref, ca_ref)
        b = conv(ub, cols, cwb_ref, cbb_ref, cb_ref)
        gelu = 0.5 * a * (1.0 + jnp.tanh(0.7978845608028654 * (a + 0.044715 * (a * a * a))))
        return (gelu * b).astype(_bf16)

    pieces = [slice(c, c + FFN_PIECE) for c in range(0, tf, FFN_PIECE)]
    u = up(pieces[0])
    ts = []
    for c, cols in enumerate(pieces):
        u_next = up(pieces[c + 1]) if c + 1 < len(pieces) else None
        ts.append(gate(cols, *u))
        u = u_next
    acc_ref[...] += _dot(jnp.concatenate(ts, axis=1), wd_ref[...])

    @pl.when(j == pl.num_programs(1) - 1)
    def _():
        out_ref[...] = x_ref[...] + _rms(acc_ref[...], gpost_ref[...])


def _ffn(h, gpre, w_up, conv_w, conv_b, w_down, gpost, *, seq, tm=1024, tf=1024):
    t, d = h.shape
    nf = D_FF // tf
    cb2 = conv_b.reshape(1, 2 * D_FF)
    return pl.pallas_call(
        functools.partial(_ffn_kernel, tiles_per_seq=seq // tm),
        out_shape=jax.ShapeDtypeStruct((t, d), _f32),
        grid=(t // tm, nf),
        in_specs=[
            pl.BlockSpec((tm, d), lambda i, j: (i, 0)),
            pl.BlockSpec((1, d), lambda i, j: (0, 0)),
            pl.BlockSpec((d, tf), lambda i, j: (0, j)),
            pl.BlockSpec((d, tf), lambda i, j: (0, nf + j)),
            pl.BlockSpec((3, tf), lambda i, j: (0, j)),
            pl.BlockSpec((3, tf), lambda i, j: (0, nf + j)),
            pl.BlockSpec((1, tf), lambda i, j: (0, j)),
            pl.BlockSpec((1, tf), lambda i, j: (0, nf + j)),
            pl.BlockSpec((tf, d), lambda i, j: (j, 0)),
            pl.BlockSpec((1, d), lambda i, j: (0, 0)),
        ],
        out_specs=pl.BlockSpec((tm, d), lambda i, j: (i, 0)),
        scratch_shapes=[
            pltpu.VMEM((tm, d), _bf16),
            pltpu.VMEM((tm, d), _f32),
            pltpu.VMEM((nf, HALO, tf), _f32),
            pltpu.VMEM((nf, HALO, tf), _f32),
        ],
        compiler_params=_params("arbitrary", "arbitrary"),
        name="conv_ffn",
    )(h, gpre, w_up, w_up, conv_w, conv_w, cb2, cb2, w_down, gpost)


def kernel(x, norm_pre_mix, norm_post_mix, norm_pre_ffn, norm_post_ffn, w_in, attn_sinks, hgrn_lb_logits,
           hgrn_out_norm, w_branch_a, w_branch_b, w_branch_c, w_out, w_ffn_up, ffn_conv_w, ffn_conv_b,
           w_ffn_down):
    bsz, seq, d = x.shape
    depth = w_in.shape[0]
    h = x.reshape(bsz * seq, d)
    row = lambda a: a.reshape(1, -1)
    for l in range(depth):
        nat, oc, g1, g2 = _in_proj(h, row(norm_pre_mix[l]), _permute_w_in(w_in[l]), hgrn_lb_logits,
                                   row(hgrn_out_norm[l]), layer=l, bsz=bsz, seq=seq)
        nat3 = nat.reshape(bsz, seq, NAT_W)
        oa = _dilated(nat3, g1, g2)
        ob = _swa(nat3, attn_sinks[l])
        h = _merge(oa, ob, oc, nat, h, w_branch_a[l].astype(_bf16), w_branch_b[l].astype(_bf16),
                   w_branch_c[l].astype(_bf16), w_out[l].astype(_bf16), row(norm_post_mix[l]))
        h = _ffn(h, row(norm_pre_ffn[l]), w_ffn_up[l].astype(_bf16), ffn_conv_w[l], ffn_conv_b[l],
                 w_ffn_down[l].astype(_bf16), row(norm_post_ffn[l]), seq=seq)
    return h.reshape(bsz, seq, d)
```

```python
import functools

import jax
import jax.numpy as jnp
from jax import lax
from jax.experimental import pallas as pl
from jax.experimental.pallas import tpu as pltpu

D_MODEL = 1024
HEAD_DIM = 64
DIL_PATTERNS = ((128, 1), (512, 4), (2048, 16))
SWA_WINDOW = 128
SWA_KV_HEADS = 2
HGRN_HEADS = 4
HGRN_DK = 128
HGRN_CHUNK = 32
HGRN_ATT_BLOCK = 128
D_FF = 4 * D_MODEL
BAND = 128
GROUP_W = 256
QKV_W = 3 * GROUP_W
EPS = 1e-6
LANES = 128
NEG = -1e30
ATTN_UNROLL = 4

SRC_AQ, SRC_AK, SRC_AV = 0, 768, 1536
SRC_B, SRC_C, SRC_GATES = 2304, 3072, 5120
NAT_GATES = 0
NAT_A0 = 3072
NAT_BQ, NAT_BK, NAT_BV = 3840, 4352, 4480
NAT_W = 4608
HGRN_W = HGRN_HEADS * HGRN_DK
COL_C = NAT_W
COL_G1 = COL_C + 4 * HGRN_W
COL_G2 = COL_G1 + QKV_W
D_IN = COL_G2 + QKV_W

VMEM_LIMIT = 56 * 1024 * 1024

_f32 = jnp.float32
_bf16 = jnp.bfloat16


def _dot(a, b):
    return jnp.dot(a, b, preferred_element_type=_f32)


def _dot_nt(a, b):
    return lax.dot_general(a, b, (((1,), (1,)), ((), ())), preferred_element_type=_f32)


def _dot_tn(a, b):
    return lax.dot_general(a, b, (((0,), (0,)), ((), ())), preferred_element_type=_f32)


def _rms(x, g):
    return x * lax.rsqrt(jnp.mean(x * x, axis=-1, keepdims=True) + EPS) * g


def _sigmoid(x):
    return 1.0 / (1.0 + jnp.exp(-x))


def _params(*sem):
    return pltpu.CompilerParams(dimension_semantics=sem, vmem_limit_bytes=VMEM_LIMIT)


def _permute_w_in(w):
    cols = lambda start, width: w[:, start:start + width]
    grp = lambda gi: [cols(SRC_AQ + gi * GROUP_W, GROUP_W), cols(SRC_AK + gi * GROUP_W, GROUP_W),
                      cols(SRC_AV + gi * GROUP_W, GROUP_W)]
    parts = [cols(SRC_GATES, 3 * D_MODEL)] + grp(0) + [cols(SRC_B, SRC_C - SRC_B), cols(SRC_C, SRC_GATES - SRC_C)]
    parts += grp(1) + grp(2)
    return jnp.concatenate(parts, axis=1).astype(_bf16)


def _in_proj_kernel(x_ref, g_ref, w_ref, lbl_ref, gain_ref, nat_ref, oc_ref, g1_ref, g2_ref,
                    hn_ref, c_ref, st_ref, *, tn, layer, tiles_per_seq):
    tm = x_ref.shape[0]

    @pl.when(pl.program_id(0) % tiles_per_seq == 0)
    def _():
        st_ref[...] = jnp.zeros_like(st_ref)

    hn = _rms(x_ref[...], g_ref[...])
    nlc = hn_ref.shape[0]
    for c in range(nlc):
        hn_ref[c] = hn[:, c * LANES:(c + 1) * LANES]
    hb = hn.astype(_bf16)
    th = 2 * HGRN_W

    def hgrn_chunk(c):
        c_ref[:, c:c + th] = _dot(hb, w_ref[:, COL_C + c:COL_C + c + th]).astype(_bf16)

    def nat_chunk(c):
        nat_ref[:, c:c + tn] = _dot(hb, w_ref[:, c:c + tn]).astype(_bf16)

    def dilated(out_ref, col0):
        d = out_ref.shape[0]
        rows = tm // d
        lhs = jnp.concatenate(
            [jnp.concatenate([hn_ref[c, pl.ds(r, rows, stride=d), :] for c in range(nlc)], axis=1)
             for r in range(d)], axis=0).astype(_bf16)
        y = _dot(lhs, w_ref[:, col0:col0 + QKV_W]).astype(_bf16)
        for r in range(d):
            out_ref[r] = y[r * rows:(r + 1) * rows]

    hgrn_chunk(0)
    matmuls = [functools.partial(hgrn_chunk, th)]
    matmuls += [functools.partial(nat_chunk, c) for c in range(0, NAT_W, tn)]
    matmuls += [functools.partial(dilated, g1_ref, COL_G1), functools.partial(dilated, g2_ref, COL_G2)]
    hgrn = _hgrn_stages(c_ref, lbl_ref, gain_ref, st_ref, oc_ref, layer=layer)
    per_matmul = -(-_hgrn_stage_count(tm) // len(matmuls))
    for mm in matmuls:
        mm()
        for _ in range(per_matmul):
            next(hgrn, None)
    for _ in hgrn:
        pass


def _in_proj(x, g, w, lb_logits, gain, *, layer, bsz, seq, tm=512, tn=768):
    t, d = x.shape
    tps = seq // tm
    d1, d2 = DIL_PATTERNS[1][1], DIL_PATTERNS[2][1]
    return pl.pallas_call(
        functools.partial(_in_proj_kernel, tn=tn, layer=layer, tiles_per_seq=tps),
        out_shape=(jax.ShapeDtypeStruct((t, NAT_W), _bf16),
                   jax.ShapeDtypeStruct((t, HGRN_W), _bf16),
                   jax.ShapeDtypeStruct((bsz, d1, seq // d1, QKV_W), _bf16),
                   jax.ShapeDtypeStruct((bsz, d2, seq // d2, QKV_W), _bf16)),
        grid=(t // tm,),
        in_specs=[
            pl.BlockSpec((tm, d), lambda i: (i, 0)),
            pl.BlockSpec((1, d), lambda i: (0, 0)),
            pl.BlockSpec((d, D_IN), lambda i: (0, 0), pipeline_mode=pl.Buffered(1)),
            pl.BlockSpec(lb_logits.shape, lambda i: (0, 0)),
            pl.BlockSpec((1, HGRN_DK), lambda i: (0, 0)),
        ],
        out_specs=(
            pl.BlockSpec((tm, NAT_W), lambda i: (i, 0)),
            pl.BlockSpec((tm, HGRN_W), lambda i: (i, 0)),
            pl.BlockSpec((None, d1, tm // d1, QKV_W), lambda i: (i // tps, 0, i % tps, 0)),
            pl.BlockSpec((None, d2, tm // d2, QKV_W), lambda i: (i // tps, 0, i % tps, 0)),
        ),
        scratch_shapes=[pltpu.VMEM((d // LANES, tm, LANES), _f32),
                        pltpu.VMEM((tm, 4 * HGRN_W), _bf16),
                        pltpu.VMEM((HGRN_HEADS, HGRN_DK, HGRN_DK), _f32)],
        compiler_params=_params("arbitrary"),
        name="in_proj",
    )(x, g, w, lb_logits, gain)


def _band_bias(max_dist, nk, first):
    qi = lax.broadcasted_iota(jnp.int32, (BAND, nk), 0)
    ki = lax.broadcasted_iota(jnp.int32, (BAND, nk), 1)
    dist = qi + (nk - BAND) - ki
    ok = (dist >= 0) & (dist <= max_dist)
    if first:
        ok = ok & (ki >= nk - BAND)
    return jnp.where(ok, 0.0, NEG).astype(_f32)


def _band_blocks(blocks, *, sink_of_head, want_lse):
    nh = GROUP_W // HEAD_DIM
    nk = blocks[0][1].shape[0]
    has_prev = nk == 2 * BAND
    lane_head = lax.broadcasted_iota(jnp.int32, (BAND, GROUP_W), 1) // HEAD_DIM
    fold = (lambda x, op: op(x[..., :BAND], x[..., BAND:])) if has_prev else (lambda x, op: x)
    scale = jnp.asarray(HEAD_DIM ** -0.5, _bf16)

    def stack_heads(q):
        q = q * scale
        return jnp.concatenate([jnp.where(lane_head == h, q, jnp.zeros_like(q)) for h in range(nh)], axis=0)

    def pick_heads(x):
        out = jnp.broadcast_to(x[0], (BAND, GROUP_W))
        for h in range(1, nh):
            out = jnp.where(lane_head == h, x[h], out)
        return out

    if sink_of_head is not None:
        head = lax.broadcasted_iota(jnp.int32, (nh, BAND, 1), 0)
        sink = jnp.zeros((nh, BAND, 1), _f32)
        for h in range(nh):
            sink = jnp.where(head == h, sink_of_head(h), sink)

    def scores(qh, kcat, bias):
        x = _dot_nt(qh, kcat).reshape(nh, BAND, nk) + bias[None]
        mx = jnp.max(fold(x, jnp.maximum), axis=-1, keepdims=True)
        return x, (mx if sink_of_head is None else jnp.maximum(mx, sink))

    def probs(x, mx):
        p = jnp.exp(x - mx)
        den = jnp.sum(fold(p, jnp.add), axis=-1, keepdims=True)
        if sink_of_head is not None:
            den = den + jnp.exp(sink - mx)
        return p.reshape(nh * BAND, nk).astype(_bf16), den

    qs = [stack_heads(q) for q, _, _, _ in blocks]
    sm = [scores(qh, kcat, bias) for qh, (_, kcat, _, bias) in zip(qs, blocks)]
    pd = [probs(x, mx) for x, mx in sm]
    pv = [_dot(p, vcat).reshape(nh, BAND, GROUP_W) for (p, _), (_, _, vcat, _) in zip(pd, blocks)]
    o = [pick_heads(x * (1.0 / d)) for x, (_, d) in zip(pv, pd)]
    if not want_lse:
        return [(x, None) for x in o]
    return [(x, pick_heads(mx + jnp.log(d))) for x, (_, mx), (_, d) in zip(o, sm, pd)]


def _qkv_cols(ref, rows, lead=()):
    idx = lambda c: lead + (rows, slice(c * GROUP_W, (c + 1) * GROUP_W))
    return ref[idx(0)], ref[idx(1)], ref[idx(2)]


def _dil_kernel(a0_ref, g1_ref, g2_ref, oa_ref, o_s, lse_s, bias_s):
    seq = a0_ref.shape[0]
    blks = functools.partial(_band_blocks, sink_of_head=None, want_lse=True)
    halves = [slice(hf * LANES, (hf + 1) * LANES) for hf in range(GROUP_W // LANES)]
    nu = ATTN_UNROLL
    win0, win1, win2 = (w // d for w, d in DIL_PATTERNS)
    d1, d2 = DIL_PATTERNS[1][1], DIL_PATTERNS[2][1]
    assert win0 == win1 and seq // d2 == BAND and win2 >= BAND - 1
    bias_s[0] = _band_bias(win0, 2 * BAND, True)
    bias_s[1] = _band_bias(win0, 2 * BAND, False)
    bias_s[2, :, 0:BAND] = _band_bias(win2, BAND, True)

    def put(g, rows, o, lse):
        for hf, ls in enumerate(halves):
            o_s[g, hf, rows, :] = o[:, ls]
            lse_s[g, hf, rows, :] = lse[:, ls]

    def band(ref, lead, i):
        cur = pl.ds(pl.multiple_of(i * BAND, BAND), BAND)
        prev = pl.ds(pl.multiple_of(jnp.maximum(i - 1, 0) * BAND, BAND), BAND)
        q, kc, vc = _qkv_cols(ref, cur, lead)
        _, kp, vp = _qkv_cols(ref, prev, lead)
        return (q, jnp.concatenate([kp, kc], axis=0), jnp.concatenate([vp, vc], axis=0),
                bias_s[jnp.minimum(i, 1)])

    def g0_body(n, c):
        ids = [n * nu + u for u in range(nu)]
        for i, (o, lse) in zip(ids, blks([band(a0_ref, (), i) for i in ids])):
            put(0, pl.ds(pl.multiple_of(i * BAND, BAND), BAND), o, lse)
        return c

    lax.fori_loop(0, seq // BAND // nu, g0_body, 0)

    nb1 = seq // d1 // BAND

    def g1_body(n, c):
        ids = [((n * nu + u) // nb1, (n * nu + u) % nb1) for u in range(nu)]
        for (r, i), (o, lse) in zip(ids, blks([band(g1_ref, (r,), i) for r, i in ids])):
            put(1, pl.ds(i * (BAND * d1) + r, BAND, stride=d1), o, lse)
        return c

    lax.fori_loop(0, d1 * nb1 // nu, g1_body, 0)

    def g2_body(n, c):
        ids = [n * nu + u for u in range(nu)]
        outs = blks([_qkv_cols(g2_ref, slice(None), (r,)) + (bias_s[2, :, 0:BAND],) for r in ids])
        for r, (o, lse) in zip(ids, outs):
            put(2, pl.ds(r, BAND, stride=d2), o, lse)
        return c

    lax.fori_loop(0, d2 // nu, g2_body, 0)

    def mix_body(i, c):
        rows = pl.ds(pl.multiple_of(i * BAND, BAND), BAND)
        for hf, ls in enumerate(halves):
            l0, l1, l2 = lse_s[0, hf, rows, :], lse_s[1, hf, rows, :], lse_s[2, hf, rows, :]
            m = jnp.maximum(jnp.maximum(l0, l1), l2)
            e0, e1, e2 = jnp.exp(l0 - m), jnp.exp(l1 - m), jnp.exp(l2 - m)
            oa = (e0 * o_s[0, hf, rows, :] + e1 * o_s[1, hf, rows, :] + e2 * o_s[2, hf, rows, :]) / (e0 + e1 + e2)
            oa_ref[rows, ls] = oa.astype(oa_ref.dtype)
        return c

    lax.fori_loop(0, seq // BAND, mix_body, 0, unroll=ATTN_UNROLL)


def _dilated(nat3, g1, g2):
    bsz, seq, _ = nat3.shape
    n_grp = len(DIL_PATTERNS)
    return pl.pallas_call(
        _dil_kernel,
        out_shape=jax.ShapeDtypeStruct((bsz, seq, GROUP_W), _bf16),
        grid=(bsz,),
        in_specs=[
            pl.BlockSpec((None, seq, QKV_W), lambda b: (b, 0, NAT_A0 // QKV_W)),
            pl.BlockSpec((None,) + g1.shape[1:], lambda b: (b, 0, 0, 0)),
            pl.BlockSpec((None,) + g2.shape[1:], lambda b: (b, 0, 0, 0)),
        ],
        out_specs=pl.BlockSpec((None, seq, GROUP_W), lambda b: (b, 0, 0)),
        scratch_shapes=[pltpu.VMEM((n_grp, GROUP_W // LANES, seq, LANES), _f32)] * 2
        + [pltpu.VMEM((3, BAND, 2 * BAND), _f32)],
        compiler_params=_params("arbitrary"),
        name="dil_attn",
    )(nat3, g1, g2).reshape(bsz * seq, GROUP_W)


def _swa_kernel(sink_ref, q_ref, k_ref, v_ref, o_ref, kx_ref, vx_ref, bias_s):
    seq = q_ref.shape[0]
    nb = seq // BAND
    kvh = pl.program_id(1)
    bias_s[0] = _band_bias(SWA_WINDOW - 1, 2 * BAND, True)
    bias_s[1] = _band_bias(SWA_WINDOW - 1, 2 * BAND, False)
    lane = lax.broadcasted_iota(jnp.int32, (BAND, 2 * HEAD_DIM), 1)
    keep = (lane < HEAD_DIM) == (kvh == 0)

    def expand(i, c):
        r0 = pl.multiple_of(i * BAND, BAND)
        for src, dst in ((k_ref, kx_ref), (v_ref, vx_ref)):
            x = src[pl.ds(r0, BAND), :].astype(_f32)
            x2 = jnp.where(keep, x, pltpu.roll(x, HEAD_DIM, axis=1)).astype(_bf16)
            dst[pl.ds(r0, BAND), :] = jnp.concatenate([x2, x2], axis=1)
        return c

    lax.fori_loop(0, nb, expand, 0, unroll=ATTN_UNROLL)
    heads_per_kv = GROUP_W // HEAD_DIM

    def band(i):
        cur = pl.ds(pl.multiple_of(i * BAND, BAND), BAND)
        prev = pl.ds(pl.multiple_of(jnp.maximum(i - 1, 0) * BAND, BAND), BAND)
        return (q_ref[cur, :], jnp.concatenate([kx_ref[prev, :], kx_ref[cur, :]], axis=0),
                jnp.concatenate([vx_ref[prev, :], vx_ref[cur, :]], axis=0), bias_s[jnp.minimum(i, 1)])

    def body(n, c):
        ids = [n * ATTN_UNROLL + u for u in range(ATTN_UNROLL)]
        outs = _band_blocks([band(i) for i in ids], want_lse=False,
                            sink_of_head=lambda h: sink_ref[kvh * heads_per_kv + h])
        for i, (o, _) in zip(ids, outs):
            o_ref[pl.ds(pl.multiple_of(i * BAND, BAND), BAND), :] = o.astype(o_ref.dtype)
        return c

    lax.fori_loop(0, nb // ATTN_UNROLL, body, 0)


def _swa(nat3, sinks):
    bsz, seq, _ = nat3.shape
    kvw = SWA_KV_HEADS * HEAD_DIM
    return pl.pallas_call(
        _swa_kernel,
        out_shape=jax.ShapeDtypeStruct((bsz, seq, SWA_KV_HEADS * GROUP_W), _bf16),
        grid=(bsz, SWA_KV_HEADS),
        in_specs=[
            pl.BlockSpec(memory_space=pltpu.SMEM),
            pl.BlockSpec((None, seq, GROUP_W), lambda b, g: (b, 0, NAT_BQ // GROUP_W + g)),
            pl.BlockSpec((None, seq, kvw), lambda b, g: (b, 0, NAT_BK // kvw)),
            pl.BlockSpec((None, seq, kvw), lambda b, g: (b, 0, NAT_BV // kvw)),
        ],
        out_specs=pl.BlockSpec((None, seq, GROUP_W), lambda b, g: (b, 0, g)),
        scratch_shapes=[pltpu.VMEM((seq, GROUP_W), _bf16), pltpu.VMEM((seq, GROUP_W), _bf16),
                        pltpu.VMEM((2, BAND, 2 * BAND), _f32)],
        compiler_params=_params("arbitrary", "arbitrary"),
        name="swa_attn",
    )(sinks, nat3, nat3, nat3).reshape(bsz * seq, SWA_KV_HEADS * GROUP_W)


HGRN_TILE = 256


def _hgrn_stage_count(tm):
    return 4 + (HGRN_TILE // HGRN_CHUNK + 1) * (tm // HGRN_TILE) + 1


def _hgrn_stages(c_ref, lbl_ref, gain_ref, st_ref, o_ref, *, layer):
    tm = c_ref.shape[0]
    tc = HGRN_TILE
    col = lambda k, sl: slice(k * HGRN_W + sl.start, k * HGRN_W + sl.stop)

    logits = lbl_ref[...]
    e = jnp.exp(logits - jnp.max(logits, axis=0, keepdims=True))
    sm = e / jnp.sum(e, axis=0, keepdims=True)
    lb = jnp.zeros((1, sm.shape[1]), _f32)
    for j in range(1, layer + 1):
        lb = lb + sm[j:j + 1, :]
    one_m_lb = 1.0 - lb

    nchunk = tc // HGRN_CHUNK
    row_in_chunk = lax.broadcasted_iota(jnp.int32, (tc, HGRN_DK), 0) & (HGRN_CHUNK - 1)
    scan_steps = [(s, row_in_chunk >= s) for s in (1, 2, 4, 8, 16) if s < HGRN_CHUNK]

    def chunk_prefix(x):
        for s, keep in scan_steps:
            x = x + jnp.where(keep, pltpu.roll(x, s, axis=0), 0.0)
        return x

    blk = HGRN_ATT_BLOCK
    tb = lax.broadcasted_iota(jnp.int32, (blk, blk), 0)
    sb = lax.broadcasted_iota(jnp.int32, (blk, blk), 1)
    causal_blk = jnp.logical_and((tb // HGRN_CHUNK) == (sb // HGRN_CHUNK), sb <= tb)

    heads = [slice(h * HGRN_DK, (h + 1) * HGRN_DK) for h in range(HGRN_HEADS)]
    subs = [slice(r0, r0 + tc) for r0 in range(0, tm, tc)]
    items = [(rows, sl) for rows in subs for sl in heads]
    def gates(rows, sl):
        qraw = c_ref[rows, col(0, sl)].astype(_f32)
        fz = c_ref[rows, col(1, sl)].astype(_f32)
        t = jnp.exp(-jnp.abs(fz))
        r = 1.0 / (1.0 + t)
        tr = t * r
        pos = fz >= 0.0
        logf = jnp.log(lb[:, sl] + one_m_lb[:, sl] * jnp.where(pos, r, tr))
        k = one_m_lb[:, sl] * jnp.where(pos, tr, r)
        return qraw * _sigmoid(qraw), k, logf

    def decayed(q, k, logf):
        eb = jnp.exp(chunk_prefix(logf))
        last = [eb[n * HGRN_CHUNK + HGRN_CHUNK - 1:(n + 1) * HGRN_CHUNK, :] for n in range(nchunk)]
        dec = jnp.concatenate([jnp.broadcast_to(d, (HGRN_CHUNK, HGRN_DK)) for d in last], axis=0)
        kinv = k * (1.0 / eb)
        return (q * eb).astype(_bf16), kinv.astype(_bf16), (kinv * dec).astype(_bf16), last

    qkl = [gates(rows, sl) for rows, sl in items]
    yield
    qkd = [decayed(q, k, logf) for q, k, logf in qkl]
    vs = [c_ref[rows, col(2, sl)] for rows, sl in items]
    yield
    att = [[jnp.where(causal_blk, _dot_nt(qe[r0:r0 + blk], ke[r0:r0 + blk]), 0.0).astype(_bf16)
            for r0 in range(0, tc, blk)] for qe, ke, _, _ in qkd]
    yield
    intra = [jnp.concatenate([_dot(a, v[i * blk:(i + 1) * blk]) for i, a in enumerate(att_h)], axis=0)
             for att_h, v in zip(att, vs)]
    yield

    states = [st_ref[h] for h in range(HGRN_HEADS)]
    for s, rows in enumerate(subs):
        per_head = [(qkd[i][0], qkd[i][2], vs[i], qkd[i][3], intra[i])
                    for i in range(s * HGRN_HEADS, (s + 1) * HGRN_HEADS)]
        inter = [[] for _ in heads]
        for n in range(nchunk):
            rs = slice(n * HGRN_CHUNK, (n + 1) * HGRN_CHUNK)
            for h, (qe, kd, v, dec, _) in enumerate(per_head):
                inter[h].append(_dot_nt(qe[rs], states[h].astype(_bf16)))
                states[h] = states[h] * dec[n] + _dot_tn(v[rs], kd[rs])
            yield
        for h, sl in enumerate(heads):
            o = per_head[h][4] + jnp.concatenate(inter[h], axis=0)
            y = _rms(o, gain_ref[...])
            graw = c_ref[rows, col(3, sl)].astype(_f32)
            o_ref[rows, sl] = (y * (graw * _sigmoid(graw))).astype(o_ref.dtype)
        yield
    for h in range(HGRN_HEADS):
        st_ref[h] = states[h]
    yield


MERGE_PIECE = 256
MERGE_STREAM_BUFFERS = 3


def _merge_kernel(oa_ref, ob_ref, oc_ref, g0_ref, g1_ref, g2_ref, h_ref, wa_ref, wb_ref, wc_ref, wo_ref, gp_ref,
                  out_ref):
    d = wo_ref.shape[0]
    pieces = [slice(c, c + MERGE_PIECE) for c in range(0, d, MERGE_PIECE)]

    def branches(cols):
        return (_dot(oa_ref[...], wa_ref[:, cols]), _dot(ob_ref[...], wb_ref[:, cols]),
                _dot(oc_ref[...], wc_ref[:, cols]))

    def gated(cols, ya, yb, yc):
        sig = lambda ref: _sigmoid(ref[:, cols].astype(_f32))
        return (sig(g0_ref) * ya + sig(g1_ref) * yb + sig(g2_ref) * yc).astype(_bf16)

    y = branches(pieces[0])
    mix = []
    for c, cols in enumerate(pieces):
        y_next = branches(pieces[c + 1]) if c + 1 < len(pieces) else None
        mix.append(gated(cols, *y))
        y = y_next
    out = _dot(jnp.concatenate(mix, axis=1), wo_ref[...])
    out_ref[...] = h_ref[...] + _rms(out, gp_ref[...])


def _merge(oa, ob, oc, nat, h, wa, wb, wc, wo, gp, *, tm=1024):
    t, d = h.shape
    deep = pl.Buffered(MERGE_STREAM_BUFFERS)
    row = lambda a: pl.BlockSpec((tm, a.shape[1]), lambda i: (i, 0), pipeline_mode=deep)
    full = lambda a: pl.BlockSpec(a.shape, lambda i: (0, 0))
    gate = lambda j: pl.BlockSpec((tm, d), lambda i: (i, NAT_GATES // d + j), pipeline_mode=deep)
    in_specs = [row(oa), row(ob), row(oc), gate(0), gate(1), gate(2), row(h),
                full(wa), full(wb), full(wc), full(wo), full(gp)]
    out_specs = [pl.BlockSpec((tm, d), lambda i: (i, 0))]

    def pipelined(*refs):
        pltpu.emit_pipeline(_merge_kernel, grid=(t // tm,), in_specs=in_specs, out_specs=out_specs)(*refs)

    any_spec = pl.BlockSpec(memory_space=pl.ANY)
    return pl.pallas_call(
        pipelined,
        out_shape=jax.ShapeDtypeStruct((t, d), _f32),
        in_specs=[any_spec] * len(in_specs),
        out_specs=any_spec,
        compiler_params=pltpu.CompilerParams(vmem_limit_bytes=VMEM_LIMIT),
        name="merge_out",
    )(oa, ob, oc, nat, nat, nat, h, wa, wb, wc, wo, gp)


HALO = 8
FFN_PIECE = 256

def _shift_rows(u, carry, s):
    r = pltpu.roll(u, s, axis=0)
    c = pltpu.roll(carry, s, axis=0)
    row = lax.broadcasted_iota(jnp.int32, carry.shape, 0)
    head = jnp.where(row < s, c, r[0:HALO])
    return jnp.concatenate([head, r[HALO:]], axis=0)


def _ffn_kernel(x_ref, gpre_ref, wa_ref, wb_ref, cwa_ref, cwb_ref, cba_ref, cbb_ref, wd_ref, gpost_ref,
                out_ref, hn_ref, acc_ref, ca_ref, cb_ref, *, tiles_per_seq):
    i, j = pl.program_id(0), pl.program_id(1)
    tm = x_ref.shape[0]
    tf = wa_ref.shape[1]

    @pl.when(j == 0)
    def _():
        hn_ref[...] = _rms(x_ref[...], gpre_ref[...]).astype(_bf16)
        acc_ref[...] = jnp.zeros_like(acc_ref)

    @pl.when(i == 0)
    def _():
        ca_ref[j] = jnp.zeros(ca_ref.shape[1:], _f32)
        cb_ref[j] = jnp.zeros(cb_ref.shape[1:], _f32)

    seq_start = (i % tiles_per_seq) == 0

    def up(cols):
        return _dot(hn_ref[...], wa_ref[:, cols]), _dot(hn_ref[...], wb_ref[:, cols])

    def conv(u, cols, cw_ref, cbias_ref, carry_ref):
        carry = jnp.where(seq_start, 0.0, carry_ref[j, :, cols])
        carry_ref[j, :, cols] = u[tm - HALO:tm]
        cw = cw_ref[:, cols]
        return (cw[2:3] * u + cw[1:2] * _shift_rows(u, carry, 1) + cw[0:1] * _shift_rows(u, carry, 2)
                + cbias_ref[:, cols])

    def gate(cols, ua, ub):
        a = conv(ua, cols, cwa_ref, cba_ref, ca_ref)
        b = conv(ub, cols, cwb_ref, cbb_ref, cb_ref)
        gelu = 0.5 * a * (1.0 + jnp.tanh(0.7978845608028654 * (a + 0.044715 * (a * a * a))))
        return (gelu * b).astype(_bf16)

    pieces = [slice(c, c + FFN_PIECE) for c in range(0, tf, FFN_PIECE)]
    u = up(pieces[0])
    ts = []
    for c, cols in enumerate(pieces):
        u_next = up(pieces[c + 1]) if c + 1 < len(pieces) else None
        ts.append(gate(cols, *u))
        u = u_next
    acc_ref[...] += _dot(jnp.concatenate(ts, axis=1), wd_ref[...])

    @pl.when(j == pl.num_programs(1) - 1)
    def _():
        out_ref[...] = x_ref[...] + _rms(acc_ref[...], gpost_ref[...])


def _ffn(h, gpre, w_up, conv_w, conv_b, w_down, gpost, *, seq, tm=1024, tf=1024):
    t, d = h.shape
    nf = D_FF // tf
    cb2 = conv_b.reshape(1, 2 * D_FF)
    return pl.pallas_call(
        functools.partial(_ffn_kernel, tiles_per_seq=seq // tm),
        out_shape=jax.ShapeDtypeStruct((t, d), _f32),
        grid=(t // tm, nf),
        in_specs=[
            pl.BlockSpec((tm, d), lambda i, j: (i, 0)),
            pl.BlockSpec((1, d), lambda i, j: (0, 0)),
            pl.BlockSpec((d, tf), lambda i, j: (0, j)),
            pl.BlockSpec((d, tf), lambda i, j: (0, nf + j)),
            pl.BlockSpec((3, tf), lambda i, j: (0, j)),
            pl.BlockSpec((3, tf), lambda i, j: (0, nf + j)),
            pl.BlockSpec((1, tf), lambda i, j: (0, j)),
            pl.BlockSpec((1, tf), lambda i, j: (0, nf + j)),
            pl.BlockSpec((tf, d), lambda i, j: (j, 0)),
            pl.BlockSpec((1, d), lambda i, j: (0, 0)),
        ],
        out_specs=pl.BlockSpec((tm, d), lambda i, j: (i, 0)),
        scratch_shapes=[
            pltpu.VMEM((tm, d), _bf16),
            pltpu.VMEM((tm, d), _f32),
            pltpu.VMEM((nf, HALO, tf), _f32),
            pltpu.VMEM((nf, HALO, tf), _f32),
        ],
        compiler_params=_params("arbitrary", "arbitrary"),
        name="conv_ffn",
    )(h, gpre, w_up, w_up, conv_w, conv_w, cb2, cb2, w_down, gpost)


def kernel(x, norm_pre_mix, norm_post_mix, norm_pre_ffn, norm_post_ffn, w_in, attn_sinks, hgrn_lb_logits,
           hgrn_out_norm, w_branch_a, w_branch_b, w_branch_c, w_out, w_ffn_up, ffn_conv_w, ffn_conv_b,
           w_ffn_down):
    bsz, seq, d = x.shape
    depth = w_in.shape[0]
    h = x.reshape(bsz * seq, d)
    row = lambda a: a.reshape(1, -1)
    for l in range(depth):
        nat, oc, g1, g2 = _in_proj(h, row(norm_pre_mix[l]), _permute_w_in(w_in[l]), hgrn_lb_logits,
                                   row(hgrn_out_norm[l]), layer=l, bsz=bsz, seq=seq)
        nat3 = nat.reshape(bsz, seq, NAT_W)
        oa = _dilated(nat3, g1, g2)
        ob = _swa(nat3, attn_sinks[l])
        h = _merge(oa, ob, oc, nat, h, w_branch_a[l].astype(_bf16), w_branch_b[l].astype(_bf16),
                   w_branch_c[l].astype(_bf16), w_out[l].astype(_bf16), row(norm_post_mix[l]))
        h = _ffn(h, row(norm_pre_ffn[l]), w_ffn_up[l].astype(_bf16), ffn_conv_w[l], ffn_conv_b[l],
                 w_ffn_down[l].astype(_bf16), row(norm_post_ffn[l]), seq=seq)
    return h.reshape(bsz, seq, d)
```
